```python
import math
import jax, jax.numpy as jnp
from jax import lax
import numpy as np

D_MODEL = 2048
BATCH = 1
SEQ = 8192
DEPTH = 2
DEC_BATCH = 2
DEC_SEQ = 16384
PAST_LEN = 128

HEAD_DIM = 128
HEADS_PER_GROUP = 4
DILATED_GROUPS = ((128, 1), (512, 4), (2048, 16))
N_GROUPS = len(DILATED_GROUPS)
N_ATTN_HEADS = N_GROUPS * HEADS_PER_GROUP
D_ATTN = N_ATTN_HEADS * HEAD_DIM
D_ATTN_OUT = HEADS_PER_GROUP * HEAD_DIM
Q_BLOCK = 64
NEG_INF = -1e30
D_RNN = D_MODEL
N_RNN_BLOCKS = 16
RNN_BLOCK = D_RNN // N_RNN_BLOCKS
CONV_WIDTH = 4
CONV_LEFT = 2
LRU_C = 8.0
REL_BUCKETS = 32
REL_MAX_DIST = 1024
ALPHA = (2.0 * DEPTH) ** 0.25
BETA = (8.0 * DEPTH) ** -0.25
LN_EPS = 1e-5
SPLITS = (D_ATTN, 2 * D_ATTN, 3 * D_ATTN, 3 * D_ATTN + D_ATTN_OUT,
          3 * D_ATTN + D_ATTN_OUT + D_RNN, 3 * D_ATTN + D_ATTN_OUT + 2 * D_RNN)
N_IN = 3 * D_ATTN + D_ATTN_OUT + 2 * D_RNN + 2 * D_MODEL

kernel_name = "hybrid_dilated_attn_rglru_encoder"


def t5_bucket(rel):
    nb = REL_BUCKETS // 2
    max_exact = nb // 2
    ret = (rel > 0).astype(np.int32) * nb
    n = np.abs(rel)
    large = max_exact + (np.log(np.maximum(n, max_exact) / max_exact)
                         / np.log(REL_MAX_DIST / max_exact) * (nb - max_exact)).astype(np.int32)
    large = np.minimum(large, nb - 1)
    return (ret + np.where(n < max_exact, n, large)).astype(np.int32)


def layer_norm(x, g, b):
    xf = x.astype(jnp.float32)
    mu = jnp.mean(xf, -1, keepdims=True)
    var = jnp.mean(jnp.square(xf - mu), -1, keepdims=True)
    return ((xf - mu) * lax.rsqrt(var + LN_EPS) * g + b).astype(x.dtype)


def banded_attention(q, k, v, bias_tab, half):
    N, L, H, E = q.shape
    nb = -(-L // Q_BLOCK)
    Lp = nb * Q_BLOCK
    W = Q_BLOCK + 2 * half
    qb = jnp.pad(q, ((0, 0), (0, Lp - L), (0, 0), (0, 0))).reshape(N, nb, Q_BLOCK, H, E)
    pad_kv = ((0, 0), (half, Lp - L + half), (0, 0), (0, 0))
    idx = np.arange(nb)[:, None] * Q_BLOCK + np.arange(W)[None, :]
    kb = jnp.pad(k, pad_kv)[:, idx]
    vb = jnp.pad(v, pad_kv)[:, idx].astype(jnp.float32)
    off = np.arange(W)[None, :] - half - np.arange(Q_BLOCK)[:, None]
    kpos = idx - half
    valid = (np.abs(off) <= half)[None] & ((kpos >= 0) & (kpos < L))[:, None, :]
    bias = jnp.transpose(bias_tab[np.clip(off, -half, half) + half], (2, 0, 1)).astype(jnp.float32)
    s = jnp.einsum('nbqhe,nbkhe->nbhqk', qb, kb, preferred_element_type=jnp.float32) * (E ** -0.5)
    s = jnp.where(valid[None, :, None], s + bias[None, None], NEG_INF)
    m = jnp.max(s, -1, keepdims=True)
    p = jnp.exp(s - m)
    den = jnp.sum(p, -1, keepdims=True)
    o = jnp.einsum('nbhqk,nbkhe->nbqhe', p, vb) / jnp.transpose(den, (0, 1, 3, 2, 4))
    lse = jnp.transpose((m + jnp.log(den))[..., 0], (0, 1, 3, 2))
    return o.reshape(N, Lp, H, E)[:, :L], lse.reshape(N, Lp, H)[:, :L]


def dilated_attention(q, k, v, rel_bias):
    B, S, _, E = q.shape
    outs, lses = [], []
    for g, (window, dil) in enumerate(DILATED_GROUPS):
        half = window // (2 * dil)
        L = S // dil
        hs = slice(g * HEADS_PER_GROUP, (g + 1) * HEADS_PER_GROUP)
        tab = rel_bias[t5_bucket(np.arange(-half, half + 1) * dil)][:, hs]

        def split(t):
            return t[:, :, hs].reshape(B, L, dil, HEADS_PER_GROUP, E).transpose(0, 2, 1, 3, 4) \
                .reshape(B * dil, L, HEADS_PER_GROUP, E)

        o, lse = banded_attention(split(q), split(k), split(v), tab, half)
        outs.append(o.reshape(B, dil, L, HEADS_PER_GROUP, E).transpose(0, 2, 1, 3, 4)
                    .reshape(B, S, HEADS_PER_GROUP, E))
        lses.append(lse.reshape(B, dil, L, HEADS_PER_GROUP).transpose(0, 2, 1, 3)
                    .reshape(B, S, HEADS_PER_GROUP))
    wts = jax.nn.softmax(jnp.stack(lses), axis=0)
    return jnp.sum(wts[..., None] * jnp.stack(outs), axis=0)


def _lin_combine(c1, c2):
    a1, b1 = c1
    a2, b2 = c2
    return a1 * a2, a2 * b1 + b2


def rglru(xc, w_gate, b_gate, lam, reverse):
    B, S, _ = xc.shape
    xb = xc.reshape(B, S, N_RNN_BLOCKS, RNN_BLOCK)
    gates = jax.nn.sigmoid(jnp.einsum('bsnc,gncd->gbsnd', xb, w_gate.astype(jnp.float32))
                           + b_gate.astype(jnp.float32)[:, None, None])
    r = gates[0].reshape(B, S, D_RNN)
    i = gates[1].reshape(B, S, D_RNN)
    log_a = -LRU_C * r * jax.nn.softplus(-lam.astype(jnp.float32))
    a = jnp.exp(log_a)
    b = jnp.sqrt(-jnp.expm1(2.0 * log_a)) * (i * xc)
    _, h = lax.associative_scan(_lin_combine, (a, b), axis=1, reverse=reverse)
    return h


def encoder_layer(x, w_in, b_in, conv_w, conv_b, lru_w, lru_b, lru_lam,
                  w_attn_o, w_rnn_o, w_out, ln_g, ln_b, rel_bias):
    B, S, _ = x.shape
    z = x @ w_in + b_in
    q, k, v, ga, xr, gr, gm = jnp.split(z, SPLITS, axis=-1)
    hshape = (B, S, N_ATTN_HEADS, HEAD_DIM)
    oa = dilated_attention(q.reshape(hshape), k.reshape(hshape), v.reshape(hshape), rel_bias)
    ya = (oa.reshape(B, S, D_ATTN_OUT).astype(x.dtype) * jax.nn.silu(ga)) @ w_attn_o
    xp = jnp.pad(xr, ((0, 0), (CONV_LEFT, CONV_WIDTH - 1 - CONV_LEFT), (0, 0)))
    xc = conv_b + sum(xp[:, j:j + S] * conv_w[j] for j in range(CONV_WIDTH))
    xc = xc.astype(jnp.float32)
    h = rglru(xc, lru_w[0], lru_b[0], lru_lam[0], False) + rglru(xc, lru_w[1], lru_b[1], lru_lam[1], True)
    yr = (h.astype(x.dtype) * jax.nn.silu(gr)) @ w_rnn_o
    g = jax.nn.sigmoid(gm).reshape(B, S, 2, D_MODEL)
    out = (g[:, :, 0] * ya + g[:, :, 1] * yr) @ w_out
    return layer_norm(ALPHA * x + out, ln_g, ln_b)


def setup_inputs(seed: int = 0) -> dict:
    key = jax.random.key(seed)
    ks = jax.random.split(key, 16)
    f32 = jnp.float32
    x_prompt = jax.random.normal(ks[0], (BATCH, SEQ, D_MODEL), f32)
    x_sample = jax.random.normal(ks[1], (DEC_BATCH, DEC_SEQ, D_MODEL), f32)
    w_in = jax.random.normal(ks[2], (DEPTH, D_MODEL, N_IN), f32) * D_MODEL ** -0.5
    w_in = w_in.at[:, :, 2 * D_ATTN:3 * D_ATTN].multiply(BETA)
    b_in = 0.01 * jax.random.normal(ks[3], (DEPTH, N_IN), f32)
    conv_w = jax.random.normal(ks[4], (DEPTH, CONV_WIDTH, D_RNN), f32) * CONV_WIDTH ** -0.5
    conv_b = 0.01 * jax.random.normal(ks[5], (DEPTH, D_RNN), f32)
    lru_w = jax.random.normal(ks[6], (DEPTH, 2, 2, N_RNN_BLOCKS, RNN_BLOCK, RNN_BLOCK), f32) * RNN_BLOCK ** -0.5
    lru_b = 0.01 * jax.random.normal(ks[7], (DEPTH, 2, 2, N_RNN_BLOCKS, RNN_BLOCK), f32)
    a_c = jax.random.uniform(ks[8], (DEPTH, 2, D_RNN), f32, minval=0.9, maxval=0.999)
    s = a_c ** (1.0 / LRU_C)
    lru_lam = jnp.log(s) - jnp.log1p(-s)
    w_attn_o = jax.random.normal(ks[9], (DEPTH, D_ATTN_OUT, D_MODEL), f32) * (D_ATTN_OUT ** -0.5) * BETA
    w_rnn_o = jax.random.normal(ks[10], (DEPTH, D_RNN, D_MODEL), f32) * (D_RNN ** -0.5) * BETA
    w_out = jax.random.normal(ks[11], (DEPTH, D_MODEL, D_MODEL), f32) * (D_MODEL ** -0.5) * BETA
    ln_g = 1.0 + 0.02 * jax.random.normal(ks[12], (DEPTH, D_MODEL), f32)
    ln_b = 0.02 * jax.random.normal(ks[13], (DEPTH, D_MODEL), f32)
    rel_bias = 0.1 * jax.random.normal(ks[14], (REL_BUCKETS, N_ATTN_HEADS), f32)
    return {"x_prompt": x_prompt, "x_sample": x_sample, "w_in": w_in, "b_in": b_in,
            "conv_w": conv_w, "conv_b": conv_b, "lru_w": lru_w, "lru_b": lru_b, "lru_lam": lru_lam,
            "w_attn_o": w_attn_o, "w_rnn_o": w_rnn_o, "w_out": w_out,
            "ln_g": ln_g, "ln_b": ln_b, "rel_bias": rel_bias}


def trunk(x, w_in, b_in, conv_w, conv_b, lru_w, lru_b, lru_lam, w_attn_o, w_rnn_o, w_out, ln_g, ln_b, rel_bias):
    for l in range(DEPTH):
        x = encoder_layer(x, w_in[l], b_in[l], conv_w[l], conv_b[l], lru_w[l], lru_b[l], lru_lam[l],
                          w_attn_o[l], w_rnn_o[l], w_out[l], ln_g[l], ln_b[l], rel_bias)
    return x


def reference(x_prompt, x_sample, w_in, b_in, conv_w, conv_b, lru_w, lru_b, lru_lam,
              w_attn_o, w_rnn_o, w_out, ln_g, ln_b, rel_bias):
    y_prompt = trunk(x_prompt, w_in, b_in, conv_w, conv_b, lru_w, lru_b, lru_lam,
                     w_attn_o, w_rnn_o, w_out, ln_g, ln_b, rel_bias)
    y_sample = trunk(x_sample, w_in, b_in, conv_w, conv_b, lru_w, lru_b, lru_lam,
                     w_attn_o, w_rnn_o, w_out, ln_g, ln_b, rel_bias)
    return (y_prompt, y_sample)
```

```python
import functools

import numpy as np
import jax
import jax.numpy as jnp
from jax import lax
from jax.experimental import pallas as pl
from jax.experimental.pallas import tpu as pltpu

D_MODEL = 2048
DEPTH = 2
HEAD_DIM = 128
HEADS_PER_GROUP = 4
DILATED_GROUPS = ((128, 1), (512, 4), (2048, 16))
N_GROUPS = len(DILATED_GROUPS)
D_ATTN = N_GROUPS * HEADS_PER_GROUP * HEAD_DIM
D_GROUP = HEADS_PER_GROUP * HEAD_DIM
D_RNN = D_MODEL
N_RNN_BLOCKS = 16
RNN_BLOCK = D_RNN // N_RNN_BLOCKS
CONV_WIDTH = 4
CONV_LEFT = 2
LRU_C = 8.0
REL_BUCKETS = 32
REL_MAX_DIST = 1024
ALPHA = (2.0 * DEPTH) ** 0.25
LN_EPS = 1e-5
NEG_INF = -1e30

LANES = 128
SUBLANES = 8
VMEM_LIMIT_BYTES = 56 * 1024 * 1024

HALF = 64
Q_SUB = 128
K_WIN = Q_SUB + 2 * HALF
ATTN_TILE = 512

RNN_TILE = 512
RNN_SLAB = 1024
RNN_CHUNKS = SUBLANES

PROJ_TM = 2048
MERGE_TM = 256

REST_COLS = 2 * D_RNN + 2 * D_MODEL + D_GROUP


def _params(*sem):
    return pltpu.CompilerParams(dimension_semantics=sem, vmem_limit_bytes=VMEM_LIMIT_BYTES)


def _in_proj_kernel(x_ref, w_ref, b_ref, o_ref):
    acc = jnp.dot(x_ref[...], w_ref[...], preferred_element_type=jnp.float32)
    o_ref[...] = (acc + b_ref[...]).astype(o_ref.dtype)


def _in_proj(x_bf, w_bf, b, tn, out_dtype, name):
    t, d = x_bf.shape
    n = w_bf.shape[1]
    tm = min(PROJ_TM, t)
    return pl.pallas_call(
        _in_proj_kernel,
        grid=(t // tm, n // tn),
        in_specs=[
            pl.BlockSpec((tm, d), lambda i, j: (i, 0)),
            pl.BlockSpec((d, tn), lambda i, j: (0, j)),
            pl.BlockSpec((1, tn), lambda i, j: (0, j)),
        ],
        out_specs=pl.BlockSpec((tm, tn), lambda i, j: (i, j)),
        out_shape=jax.ShapeDtypeStruct((t, n), out_dtype),
        compiler_params=_params("parallel", "arbitrary"),
        name=name,
    )(x_bf, w_bf, b)


def _t5_bucket(rel):
    nb = REL_BUCKETS // 2
    max_exact = nb // 2
    ret = (rel > 0).astype(np.int32) * nb
    n = np.abs(rel)
    large = max_exact + (np.log(np.maximum(n, max_exact) / max_exact)
                         / np.log(REL_MAX_DIST / max_exact) * (nb - max_exact)).astype(np.int32)
    large = np.minimum(large, nb - 1)
    return (ret + np.where(n < max_exact, n, large)).astype(np.int32)


def _band_bias(rel_bias, g, dil):
    tab = rel_bias[_t5_bucket(np.arange(-HALF, HALF + 1) * dil)][:, g * HEADS_PER_GROUP:(g + 1) * HEADS_PER_GROUP]
    off = np.arange(K_WIN)[None, :] - HALF - np.arange(Q_SUB)[:, None]
    band = np.abs(off) <= HALF
    bias = jnp.transpose(tab[np.clip(off, -HALF, HALF) + HALF], (2, 0, 1)).astype(jnp.float32)
    return jnp.where(band[None], bias, NEG_INF)


def _attn_kernel(q_ref, kp_ref, kc_ref, kn_ref, vp_ref, vc_ref, vn_ref, bias_ref,
                 o_ref, lse_ref, kbuf, vbuf, *, tile, length):
    i = pl.program_id(2)
    kbuf[0:HALF] = kp_ref[...]
    kbuf[HALF:HALF + tile] = kc_ref[...]
    kbuf[HALF + tile:] = kn_ref[...]
    vbuf[0:HALF] = vp_ref[...]
    vbuf[HALF:HALF + tile] = vc_ref[...]
    vbuf[HALF + tile:] = vn_ref[...]
    scale = HEAD_DIM ** -0.5
    for sb in range(tile // Q_SUB):
        r0 = sb * Q_SUB
        kpos = i * tile + (r0 - HALF) + lax.broadcasted_iota(jnp.int32, (Q_SUB, K_WIN), 1)
        in_seq = (kpos >= 0) & (kpos < length)
        for h in range(HEADS_PER_GROUP):
            cs = slice(h * HEAD_DIM, (h + 1) * HEAD_DIM)
            q = q_ref[r0:r0 + Q_SUB, cs]
            k = kbuf[r0:r0 + K_WIN, cs]
            v = vbuf[r0:r0 + K_WIN, cs]
            s = lax.dot_general(q, k, (((1,), (1,)), ((), ())),
                                preferred_element_type=jnp.float32) * scale
            s = jnp.where(in_seq, s + bias_ref[h], NEG_INF)
            m = jnp.max(s, axis=-1, keepdims=True)
            p = jnp.exp(s - m)
            den = jnp.sum(p, axis=-1, keepdims=True)
            o = jnp.dot(p.astype(jnp.bfloat16), v, preferred_element_type=jnp.float32)
            o_ref[r0:r0 + Q_SUB, cs] = o / den
            lse_ref[r0:r0 + Q_SUB, cs] = jnp.broadcast_to(m + jnp.log(den), (Q_SUB, HEAD_DIM))


def _attention_group(qkv, bias, g, dil):
    b, s, _ = qkv.shape
    length = s // dil
    tile = min(ATTN_TILE, length)
    hb = tile // HALF
    n_half_blocks = length // HALF
    ncol = 3 * D_ATTN // D_GROUP
    view = qkv.reshape(b, length, dil * 3 * D_ATTN)

    def cur(part):
        return pl.BlockSpec((None, tile, D_GROUP), lambda bb, r, i: (bb, i, r * ncol + part * N_GROUPS + g))

    def prev(part):
        return pl.BlockSpec((None, HALF, D_GROUP),
                            lambda bb, r, i: (bb, jnp.maximum(i * hb - 1, 0), r * ncol + part * N_GROUPS + g))

    def nxt(part):
        return pl.BlockSpec((None, HALF, D_GROUP),
                            lambda bb, r, i: (bb, jnp.minimum((i + 1) * hb, n_half_blocks - 1),
                                              r * ncol + part * N_GROUPS + g))

    out_spec = pl.BlockSpec((None, tile, D_GROUP), lambda bb, r, i: (bb, i, r))
    out_sds = jax.ShapeDtypeStruct((b, length, dil * D_GROUP), jnp.float32)
    o, lse = pl.pallas_call(
        functools.partial(_attn_kernel, tile=tile, length=length),
        grid=(b, dil, length // tile),
        in_specs=[cur(0), prev(1), cur(1), nxt(1), prev(2), cur(2), nxt(2),
                  pl.BlockSpec((HEADS_PER_GROUP, Q_SUB, K_WIN), lambda bb, r, i: (0, 0, 0))],
        out_specs=[out_spec, out_spec],
        out_shape=[out_sds, out_sds],
        scratch_shapes=[pltpu.VMEM((tile + 2 * HALF, D_GROUP), jnp.bfloat16),
                        pltpu.VMEM((tile + 2 * HALF, D_GROUP), jnp.bfloat16)],
        compiler_params=_params("parallel", "parallel", "arbitrary"),
        name=f"attn_g{g}",
    )(view, view, view, view, view, view, view, bias)
    return o.reshape(b, s, D_GROUP), lse.reshape(b, s, D_GROUP)


def _rglru_kernel(*refs, tile, n_tiles, reverse):
    if reverse:
        (xp_ref, xc_ref, xn_ref, cw_ref, cb_ref, wg_ref, bg_ref, lam_ref, hf_ref, gr_ref,
         out_ref, ext, a_s, b_s, hl_s, p_s, cv_s, carry) = refs
    else:
        (xp_ref, xc_ref, xn_ref, cw_ref, cb_ref, wg_ref, bg_ref, lam_ref,
         out_ref, ext, a_s, b_s, hl_s, p_s, cv_s, carry) = refs
    step = pl.program_id(2)
    ti = (n_tiles - 1 - step) if reverse else step
    chunk = tile // RNN_CHUNKS
    nblk = RNN_SLAB // RNN_BLOCK

    @pl.when(step == 0)
    def _():
        carry[...] = jnp.zeros_like(carry)

    ext[0:SUBLANES] = jnp.where(ti > 0, xp_ref[...], 0.0)
    ext[SUBLANES:SUBLANES + tile] = xc_ref[...]
    ext[SUBLANES + tile:] = jnp.where(ti < n_tiles - 1, xn_ref[...], 0.0)

    lam = lam_ref[...]
    neg_c_softplus = -LRU_C * (jnp.maximum(-lam, 0.0) + jnp.log1p(jnp.exp(-jnp.abs(lam))))
    for n in range(nblk):
        cs = slice(n * RNN_BLOCK, (n + 1) * RNN_BLOCK)
        xc = cb_ref[:, cs]
        for j in range(CONV_WIDTH):
            lo = SUBLANES - CONV_LEFT + j
            xc = xc + ext[lo:lo + tile, cs] * cw_ref[j:j + 1, cs]
        z = jnp.dot(xc.astype(jnp.bfloat16), wg_ref[n], preferred_element_type=jnp.float32) + bg_ref[n]
        r = jax.nn.sigmoid(z[:, :RNN_BLOCK])
        ig = jax.nn.sigmoid(z[:, RNN_BLOCK:])
        log_a = r * neg_c_softplus[:, cs]
        a = jnp.exp(log_a)
        a_s[n] = a
        b_s[n] = jnp.sqrt(jnp.tanh(-log_a) * (1.0 + a * a)) * (ig * xc)

    def scan_body(t, hp):
        hs, ps = hp
        tt = (chunk - 1 - t) if reverse else t
        rows = pl.ds(tt, RNN_CHUNKS, stride=chunk)
        new_h, new_p = [], []
        for n in range(nblk):
            a = a_s[n, rows, :]
            h = a * hs[n] + b_s[n, rows, :]
            p = a * ps[n]
            hl_s[n, rows, :] = h
            p_s[n, rows, :] = p
            new_h.append(h)
            new_p.append(p)
        return tuple(new_h), tuple(new_p)

    init = (tuple(jnp.zeros((RNN_CHUNKS, RNN_BLOCK), jnp.float32) for _ in range(nblk)),
            tuple(jnp.ones((RNN_CHUNKS, RNN_BLOCK), jnp.float32) for _ in range(nblk)))
    h_end, p_end = lax.fori_loop(0, chunk, scan_body, init)
    h_end = jnp.concatenate(h_end, axis=1)
    p_end = jnp.concatenate(p_end, axis=1)

    state = carry[...]
    order = range(RNN_CHUNKS - 1, -1, -1) if reverse else range(RNN_CHUNKS)
    for c in order:
        cv_s[c:c + 1, :] = state
        state = p_end[c:c + 1, :] * state + h_end[c:c + 1, :]
    carry[...] = state

    for c in range(RNN_CHUNKS):
        rs = slice(c * chunk, (c + 1) * chunk)
        for n in range(nblk):
            cs = slice(n * RNN_BLOCK, (n + 1) * RNN_BLOCK)
            h = hl_s[n, rs, :] + p_s[n, rs, :] * cv_s[c:c + 1, cs]
            if reverse:
                gr = gr_ref[rs, cs]
                h = (h + hf_ref[rs, cs]) * (gr * jax.nn.sigmoid(gr))
            out_ref[rs, cs] = h.astype(out_ref.dtype)


def _rglru(rest, conv_w, conv_b, wg, bg, lam, reverse, h_fwd=None):
    b, s, _ = rest.shape
    tile = RNN_TILE
    n_tiles = s // tile
    nslab = D_RNN // RNN_SLAB
    sub_per_tile = tile // SUBLANES
    n_sub = s // SUBLANES
    nblk = RNN_SLAB // RNN_BLOCK

    def tidx(i):
        return (n_tiles - 1 - i) if reverse else i

    x_cur = pl.BlockSpec((None, tile, RNN_SLAB), lambda bb, c, i: (bb, tidx(i), c))
    x_prev = pl.BlockSpec((None, SUBLANES, RNN_SLAB),
                          lambda bb, c, i: (bb, jnp.maximum(tidx(i) * sub_per_tile - 1, 0), c))
    x_next = pl.BlockSpec((None, SUBLANES, RNN_SLAB),
                          lambda bb, c, i: (bb, jnp.minimum((tidx(i) + 1) * sub_per_tile, n_sub - 1), c))
    in_specs = [
        x_prev, x_cur, x_next,
        pl.BlockSpec((CONV_WIDTH, RNN_SLAB), lambda bb, c, i: (0, c)),
        pl.BlockSpec((1, RNN_SLAB), lambda bb, c, i: (0, c)),
        pl.BlockSpec((nblk, RNN_BLOCK, 2 * RNN_BLOCK), lambda bb, c, i: (c, 0, 0)),
        pl.BlockSpec((nblk, 1, 2 * RNN_BLOCK), lambda bb, c, i: (c, 0, 0)),
        pl.BlockSpec((1, RNN_SLAB), lambda bb, c, i: (0, c)),
    ]
    args = [rest, rest, rest, conv_w, conv_b, wg, bg, lam]
    if reverse:
        in_specs.append(pl.BlockSpec((None, tile, RNN_SLAB), lambda bb, c, i: (bb, tidx(i), c)))
        in_specs.append(pl.BlockSpec((None, tile, RNN_SLAB), lambda bb, c, i: (bb, tidx(i), nslab + c)))
        args += [h_fwd, rest]
    out_dtype = jnp.bfloat16 if reverse else jnp.float32
    return pl.pallas_call(
        functools.partial(_rglru_kernel, tile=tile, n_tiles=n_tiles, reverse=reverse),
        grid=(b, nslab, n_tiles),
        in_specs=in_specs,
        out_specs=pl.BlockSpec((None, tile, RNN_SLAB), lambda bb, c, i: (bb, tidx(i), c)),
        out_shape=jax.ShapeDtypeStruct((b, s, D_RNN), out_dtype),
        scratch_shapes=[
            pltpu.VMEM((tile + 2 * SUBLANES, RNN_SLAB), jnp.float32),
            pltpu.VMEM((nblk, tile, RNN_BLOCK), jnp.float32),
            pltpu.VMEM((nblk, tile, RNN_BLOCK), jnp.float32),
            pltpu.VMEM((nblk, tile, RNN_BLOCK), jnp.float32),
            pltpu.VMEM((nblk, tile, RNN_BLOCK), jnp.float32),
            pltpu.VMEM((RNN_CHUNKS, RNN_SLAB), jnp.float32),
            pltpu.VMEM((1, RNN_SLAB), jnp.float32),
        ],
        compiler_params=_params("parallel", "parallel", "arbitrary"),
        name="rglru_bwd" if reverse else "rglru_fwd",
    )(*args)


def _merge_kernel(x_ref, o0_ref, o1_ref, o2_ref, l0_ref, l1_ref, l2_ref, ga_ref, ur_ref,
                  gma_ref, gmr_ref, wa_ref, wr_ref, wo_ref, lng_ref, lnb_ref, y_ref, ybf_ref):
    l0, l1, l2 = l0_ref[...], l1_ref[...], l2_ref[...]
    mx = jnp.maximum(jnp.maximum(l0, l1), l2)
    e0, e1, e2 = jnp.exp(l0 - mx), jnp.exp(l1 - mx), jnp.exp(l2 - mx)
    inv = 1.0 / (e0 + e1 + e2)
    oa = (e0 * inv) * o0_ref[...] + (e1 * inv) * o1_ref[...] + (e2 * inv) * o2_ref[...]
    ga = ga_ref[...]
    ua = (oa * (ga * jax.nn.sigmoid(ga))).astype(jnp.bfloat16)
    ya = jnp.dot(ua, wa_ref[...], preferred_element_type=jnp.float32)
    yr = jnp.dot(ur_ref[...], wr_ref[...], preferred_element_type=jnp.float32)
    mixed = jax.nn.sigmoid(gma_ref[...]) * ya + jax.nn.sigmoid(gmr_ref[...]) * yr
    out = jnp.dot(mixed.astype(jnp.bfloat16), wo_ref[...], preferred_element_type=jnp.float32)
    v = ALPHA * x_ref[...] + out
    mu = jnp.mean(v, axis=-1, keepdims=True)
    vc = v - mu
    var = jnp.mean(vc * vc, axis=-1, keepdims=True)
    y = vc * lax.rsqrt(var + LN_EPS) * lng_ref[...] + lnb_ref[...]
    y_ref[...] = y
    ybf_ref[...] = y.astype(jnp.bfloat16)


def _merge_out(x, attn, rest, u_r, wa, wr, wo, ln_g, ln_b):
    t, d = x.shape
    tm = MERGE_TM
    wide = D_MODEL // D_GROUP
    row = lambda w: pl.BlockSpec((tm, w), lambda i: (i, 0))
    const = lambda shp: pl.BlockSpec(shp, lambda i: (0, 0), pipeline_mode=pl.Buffered(1))
    in_specs = [
        row(d),
        row(D_GROUP), row(D_GROUP), row(D_GROUP), row(D_GROUP), row(D_GROUP), row(D_GROUP),
        pl.BlockSpec((tm, D_GROUP), lambda i: (i, 4 * wide)),
        row(D_RNN),
        pl.BlockSpec((tm, D_MODEL), lambda i: (i, 2)),
        pl.BlockSpec((tm, D_MODEL), lambda i: (i, 3)),
        const((D_GROUP, D_MODEL)), const((D_RNN, D_MODEL)), const((D_MODEL, D_MODEL)),
        const((1, D_MODEL)), const((1, D_MODEL)),
    ]
    (o0, l0), (o1, l1), (o2, l2) = attn
    return pl.pallas_call(
        _merge_kernel,
        grid=(t // tm,),
        in_specs=in_specs,
        out_specs=[row(d), row(d)],
        out_shape=[jax.ShapeDtypeStruct((t, d), jnp.float32), jax.ShapeDtypeStruct((t, d), jnp.bfloat16)],
        compiler_params=_params("parallel"),
        name="merge_out",
    )(x, o0, o1, o2, l0, l1, l2, rest, u_r, rest, rest, wa, wr, wo, ln_g, ln_b)


def _prep_layer(l, w_in, b_in, conv_w, conv_b, lru_w, lru_b, lru_lam, w_attn_o, w_rnn_o, w_out, ln_g, ln_b):
    bf = jnp.bfloat16
    a3 = 3 * D_ATTN
    c_ga, c_xr, c_gr, c_gm = a3, a3 + D_GROUP, a3 + D_GROUP + D_RNN, a3 + D_GROUP + 2 * D_RNN

    def reorder(w):
        return jnp.concatenate([w[..., c_xr:c_gr], w[..., c_gr:c_gm], w[..., c_gm:], w[..., c_ga:c_xr]], axis=-1)

    def gate_w(direction):
        w = lru_w[l, direction]
        return jnp.concatenate([w[0], w[1]], axis=-1).astype(bf)

    def gate_b(direction):
        bgt = lru_b[l, direction]
        return jnp.concatenate([bgt[0], bgt[1]], axis=-1)[:, None, :]

    return dict(
        w_qkv=w_in[l, :, :a3].astype(bf), b_qkv=b_in[l, None, :a3],
        w_rest=reorder(w_in[l]).astype(bf), b_rest=reorder(b_in[l])[None, :],
        conv_w=conv_w[l], conv_b=conv_b[l][None, :],
        wg=[gate_w(0), gate_w(1)], bg=[gate_b(0), gate_b(1)],
        lam=[lru_lam[l, 0][None, :], lru_lam[l, 1][None, :]],
        wa=w_attn_o[l].astype(bf), wr=w_rnn_o[l].astype(bf), wo=w_out[l].astype(bf),
        ln_g=ln_g[l][None, :], ln_b=ln_b[l][None, :],
    )


def _layer(x, x_bf, p, biases):
    b, s, d = x.shape
    t = b * s
    xb2 = x_bf.reshape(t, d)
    qkv = _in_proj(xb2, p["w_qkv"], p["b_qkv"], D_ATTN // 2, jnp.bfloat16, "in_proj_qkv").reshape(b, s, 3 * D_ATTN)
    rest = _in_proj(xb2, p["w_rest"], p["b_rest"], D_GROUP, jnp.float32, "in_proj_rest")
    attn = []
    for g, (_, dil) in enumerate(DILATED_GROUPS):
        o, lse = _attention_group(qkv, biases[g], g, dil)
        attn.append((o.reshape(t, D_GROUP), lse.reshape(t, D_GROUP)))
    rest3 = rest.reshape(b, s, REST_COLS)
    h_fwd = _rglru(rest3, p["conv_w"], p["conv_b"], p["wg"][0], p["bg"][0], p["lam"][0], False)
    u_r = _rglru(rest3, p["conv_w"], p["conv_b"], p["wg"][1], p["bg"][1], p["lam"][1], True, h_fwd)
    y, y_bf = _merge_out(x.reshape(t, d), attn, rest, u_r.reshape(t, D_RNN),
                         p["wa"], p["wr"], p["wo"], p["ln_g"], p["ln_b"])
    return y.reshape(b, s, d), y_bf.reshape(b, s, d)


def kernel(x_prompt, x_sample, w_in, b_in, conv_w, conv_b, lru_w, lru_b, lru_lam,
           w_attn_o, w_rnn_o, w_out, ln_g, ln_b, rel_bias):
    layers = [_prep_layer(l, w_in, b_in, conv_w, conv_b, lru_w, lru_b, lru_lam,
                          w_attn_o, w_rnn_o, w_out, ln_g, ln_b) for l in range(DEPTH)]
    biases = [_band_bias(rel_bias, g, dil) for g, (_, dil) in enumerate(DILATED_GROUPS)]

    def trunk(x):
        x_bf = x.astype(jnp.bfloat16)
        for p in layers:
            x, x_bf = _layer(x, x_bf, p, biases)
        return x

    return (trunk(x_prompt), trunk(x_sample))
```

```python
import functools

import numpy as np
import jax
import jax.numpy as jnp
from jax import lax
from jax.experimental import pallas as pl
from jax.experimental.pallas import tpu as pltpu

D_MODEL = 2048
DEPTH = 2
HEAD_DIM = 128
HEADS_PER_GROUP = 4
DILATED_GROUPS = ((128, 1), (512, 4), (2048, 16))
N_GROUPS = len(DILATED_GROUPS)
D_ATTN = N_GROUPS * HEADS_PER_GROUP * HEAD_DIM
D_GROUP = HEADS_PER_GROUP * HEAD_DIM
D_RNN = D_MODEL
N_RNN_BLOCKS = 16
RNN_BLOCK = D_RNN // N_RNN_BLOCKS
CONV_WIDTH = 4
CONV_LEFT = 2
LRU_C = 8.0
REL_BUCKETS = 32
REL_MAX_DIST = 1024
ALPHA = (2.0 * DEPTH) ** 0.25
LN_EPS = 1e-5
NEG_INF = -1e30

LANES = 128
SUBLANES = 8
VMEM_LIMIT_BYTES = 56 * 1024 * 1024

HALF = 64
Q_SUB = 128
K_WIN = Q_SUB + 2 * HALF
ATTN_TOKENS = 2048

RNN_TILE = 512
RNN_SLAB = 1024
RADIX = 4

PROJ_TM = 2048
MERGE_TM = 256

REST_COLS = 2 * D_RNN + 2 * D_MODEL + D_GROUP


def _params(*sem):
    return pltpu.CompilerParams(dimension_semantics=sem, vmem_limit_bytes=VMEM_LIMIT_BYTES)


def _sigmoid(x):
    return 0.5 + 0.5 * jnp.tanh(0.5 * x)


def _rest_proj_kernel(x_ref, w_ref, b_ref, o_ref):
    acc = jnp.dot(x_ref[...], w_ref[...], preferred_element_type=jnp.float32)
    o_ref[...] = acc + b_ref[...]


def _rest_proj(x_bf, w_bf, b):
    bsz, s, d = x_bf.shape
    n = w_bf.shape[1]
    tm, tn = PROJ_TM, D_GROUP
    return pl.pallas_call(
        _rest_proj_kernel,
        grid=(bsz, s // tm, n // tn),
        in_specs=[
            pl.BlockSpec((None, tm, d), lambda bb, i, j: (bb, i, 0)),
            pl.BlockSpec((d, tn), lambda bb, i, j: (0, j)),
            pl.BlockSpec((1, tn), lambda bb, i, j: (0, j)),
        ],
        out_specs=pl.BlockSpec((None, tm, tn), lambda bb, i, j: (bb, i, j)),
        out_shape=jax.ShapeDtypeStruct((bsz, s, n), jnp.float32),
        compiler_params=_params("parallel", "parallel", "arbitrary"),
        name="rest_proj",
    )(x_bf, w_bf, b)


def _qkv_proj_kernel(x_ref, w_ref, b_ref, o_ref, *scratch, dil, tm, tn):
    acc = jnp.dot(x_ref[...], w_ref[...], preferred_element_type=jnp.float32) + b_ref[...]
    if dil == 1:
        o_ref[0] = acc.astype(o_ref.dtype)
        return
    (acc_s,) = scratch
    rows = tm // dil
    for c in range(tn // LANES):
        cs = slice(c * LANES, (c + 1) * LANES)
        acc_s[c] = acc[:, cs]
        for r in range(dil):
            o_ref[r, :, cs] = acc_s[c, pl.ds(r, rows, stride=dil), :].astype(o_ref.dtype)


def _qkv_proj(x_bf, w_bf, b, dil, g):
    bsz, s, d = x_bf.shape
    n = w_bf.shape[1]
    tm, tn = PROJ_TM, n // 2
    scratch = [] if dil == 1 else [pltpu.VMEM((tn // LANES, tm, LANES), jnp.float32)]
    return pl.pallas_call(
        functools.partial(_qkv_proj_kernel, dil=dil, tm=tm, tn=tn),
        grid=(bsz, s // tm, n // tn),
        in_specs=[
            pl.BlockSpec((None, tm, d), lambda bb, i, j: (bb, i, 0)),
            pl.BlockSpec((d, tn), lambda bb, i, j: (0, j)),
            pl.BlockSpec((1, tn), lambda bb, i, j: (0, j)),
        ],
        out_specs=pl.BlockSpec((None, dil, tm // dil, tn), lambda bb, i, j: (bb, 0, i, j)),
        out_shape=jax.ShapeDtypeStruct((bsz, dil, s // dil, n), jnp.bfloat16),
        scratch_shapes=scratch,
        compiler_params=_params("parallel", "parallel", "arbitrary"),
        name=f"qkv_proj_g{g}",
    )(x_bf, w_bf, b)


def _t5_bucket(rel):
    nb = REL_BUCKETS // 2
    max_exact = nb // 2
    ret = (rel > 0).astype(np.int32) * nb
    n = np.abs(rel)
    large = max_exact + (np.log(np.maximum(n, max_exact) / max_exact)
                         / np.log(REL_MAX_DIST / max_exact) * (nb - max_exact)).astype(np.int32)
    large = np.minimum(large, nb - 1)
    return (ret + np.where(n < max_exact, n, large)).astype(np.int32)


def _band_bias(rel_bias, g, dil):
    tab = rel_bias[_t5_bucket(np.arange(-HALF, HALF + 1) * dil)][:, g * HEADS_PER_GROUP:(g + 1) * HEADS_PER_GROUP]
    period = Q_SUB + K_WIN + LANES
    vec = jnp.full((HEADS_PER_GROUP, period), NEG_INF, jnp.float32).at[:, :2 * HALF + 1].set(tab.T)
    mat = jnp.tile(vec, (1, Q_SUB))[:, :Q_SUB * (period - 1)].reshape(HEADS_PER_GROUP, Q_SUB, period - 1)
    return mat[:, :, :K_WIN]


def _attn_kernel(q_ref, kp_ref, kc_ref, kn_ref, vp_ref, vc_ref, vn_ref, bias_ref,
                 o_ref, lse_ref, kbuf, vbuf, *, dil, tile, length):
    i = pl.program_id(1)
    scale = HEAD_DIM ** -0.5

    def residue(r, _):
        kbuf[0:HALF] = kp_ref[r]
        kbuf[HALF:HALF + tile] = kc_ref[r]
        kbuf[HALF + tile:] = kn_ref[r]
        vbuf[0:HALF] = vp_ref[r]
        vbuf[HALF:HALF + tile] = vc_ref[r]
        vbuf[HALF + tile:] = vn_ref[r]
        for sb in range(tile // Q_SUB):
            r0 = sb * Q_SUB
            kpos = i * tile + (r0 - HALF) + lax.broadcasted_iota(jnp.int32, (Q_SUB, K_WIN), 1)
            in_seq = (kpos >= 0) & (kpos < length)
            if dil == 1:
                rows = slice(r0, r0 + Q_SUB)
            else:
                rows = pl.ds(r + r0 * dil, Q_SUB, stride=dil)
            for h in range(HEADS_PER_GROUP):
                cs = slice(h * HEAD_DIM, (h + 1) * HEAD_DIM)
                q = q_ref[r, r0:r0 + Q_SUB, cs]
                k = kbuf[r0:r0 + K_WIN, cs]
                v = vbuf[r0:r0 + K_WIN, cs]
                s = lax.dot_general(q, k, (((1,), (1,)), ((), ())),
                                    preferred_element_type=jnp.float32) * scale
                s = jnp.where(in_seq, s + bias_ref[h], NEG_INF)
                m = jnp.max(s, axis=-1, keepdims=True)
                p = jnp.exp(s - m)
                den = jnp.sum(p, axis=-1, keepdims=True)
                o = jnp.dot(p.astype(jnp.bfloat16), v, preferred_element_type=jnp.float32)
                o_ref[h, rows, :] = o / den
                lse_ref[h, rows, :] = jnp.broadcast_to(m + jnp.log(den), (Q_SUB, HEAD_DIM))
        return 0

    lax.fori_loop(0, dil, residue, 0)


def _attention_group(qkv, bias, g, dil):
    bsz, _, length, _ = qkv.shape
    s = length * dil
    tile = ATTN_TOKENS // dil
    hb = tile // HALF
    n_half_blocks = length // HALF

    def cur(part):
        return pl.BlockSpec((None, dil, tile, D_GROUP), lambda bb, i: (bb, 0, i, part))

    def prev(part):
        return pl.BlockSpec((None, dil, HALF, D_GROUP),
                            lambda bb, i: (bb, 0, jnp.maximum(i * hb - 1, 0), part))

    def nxt(part):
        return pl.BlockSpec((None, dil, HALF, D_GROUP),
                            lambda bb, i: (bb, 0, jnp.minimum((i + 1) * hb, n_half_blocks - 1), part))

    out_spec = pl.BlockSpec((None, HEADS_PER_GROUP, tile * dil, HEAD_DIM), lambda bb, i: (bb, 0, i, 0))
    out_sds = jax.ShapeDtypeStruct((bsz, HEADS_PER_GROUP, s, HEAD_DIM), jnp.float32)
    return pl.pallas_call(
        functools.partial(_attn_kernel, dil=dil, tile=tile, length=length),
        grid=(bsz, length // tile),
        in_specs=[cur(0), prev(1), cur(1), nxt(1), prev(2), cur(2), nxt(2),
                  pl.BlockSpec((HEADS_PER_GROUP, Q_SUB, K_WIN), lambda bb, i: (0, 0, 0))],
        out_specs=[out_spec, out_spec],
        out_shape=[out_sds, out_sds],
        scratch_shapes=[pltpu.VMEM((tile + 2 * HALF, D_GROUP), jnp.bfloat16),
                        pltpu.VMEM((tile + 2 * HALF, D_GROUP), jnp.bfloat16)],
        compiler_params=_params("parallel", "arbitrary"),
        name=f"attn_g{g}",
    )(qkv, qkv, qkv, qkv, qkv, qkv, qkv, bias)


def _vreg_scan(a, b, state, reverse):
    groups = a.shape[0] // SUBLANES
    a3 = a.reshape(groups, SUBLANES, LANES)
    b3 = b.reshape(groups, SUBLANES, LANES)
    sub = lax.broadcasted_iota(jnp.int32, (groups, SUBLANES, LANES), 1)
    for shift in (1, 2, 4):
        keep = (sub < SUBLANES - shift) if reverse else (sub >= shift)
        amount = SUBLANES - shift if reverse else shift
        a_sh = jnp.where(keep, pltpu.roll(a3, amount, axis=1), 1.0)
        b_sh = jnp.where(keep, pltpu.roll(b3, amount, axis=1), 0.0)
        b3 = a3 * b_sh + b3
        a3 = a3 * a_sh
    sub2 = lax.broadcasted_iota(jnp.int32, (SUBLANES, LANES), 0)
    entering = [None] * groups
    for gi in (range(groups - 1, -1, -1) if reverse else range(groups)):
        sb = jnp.broadcast_to(state, (SUBLANES, LANES))
        incl = b3[gi] + a3[gi] * sb
        if reverse:
            entering[gi] = jnp.where(sub2 == SUBLANES - 1, sb, pltpu.roll(incl, SUBLANES - 1, axis=0))
            state = incl[0:1]
        else:
            entering[gi] = jnp.where(sub2 == 0, sb, pltpu.roll(incl, 1, axis=0))
            state = incl[SUBLANES - 1:SUBLANES]
    return jnp.concatenate(entering, axis=0), state


def _radix_scan(av, bv, entering_fn, reverse):
    order = tuple(range(RADIX - 1, -1, -1)) if reverse else tuple(range(RADIX))
    h = [None] * RADIX
    p = [None] * RADIX
    h[order[0]], p[order[0]] = bv[order[0]], av[order[0]]
    for before, t in zip(order[:-1], order[1:]):
        h[t] = av[t] * h[before] + bv[t]
        p[t] = av[t] * p[before]
    entering, state = entering_fn(p[order[-1]], h[order[-1]])
    incl = [h[t] + p[t] * entering for t in range(RADIX)]
    excl = [None] * RADIX
    excl[order[0]] = entering
    for before, t in zip(order[:-1], order[1:]):
        excl[t] = incl[before]
    return incl, excl, state


def _rglru_kernel(*refs, tile, n_tiles, reverse):
    if reverse:
        (xp_ref, xc_ref, xn_ref, cw_ref, cb_ref, wg_ref, bg_ref, lam_ref, hf_ref, gr_ref,
         out_ref, ext, sum_a, sum_b, ent_s, carry, ubuf) = refs
    else:
        (xp_ref, xc_ref, xn_ref, cw_ref, cb_ref, wg_ref, bg_ref, lam_ref,
         out_ref, ext, sum_a, sum_b, ent_s, carry) = refs
    step = pl.program_id(2)
    ti = (n_tiles - 1 - step) if reverse else step
    quarter = tile // RADIX
    sixteenth = quarter // RADIX
    nblk = RNN_SLAB // RNN_BLOCK

    @pl.when(step == 0)
    def _():
        carry[...] = jnp.zeros_like(carry)

    lam = lam_ref[...]
    half_csp = (0.5 * LRU_C) * (jnp.maximum(-lam, 0.0) + jnp.log1p(jnp.exp(-jnp.abs(lam))))
    prev_ok = ti > 0
    next_ok = ti < n_tiles - 1

    for n in range(nblk):
        cs = slice(n * RNN_BLOCK, (n + 1) * RNN_BLOCK)
        ext[n, 0:SUBLANES] = jnp.where(prev_ok, xp_ref[:, cs], 0.0)
        ext[n, SUBLANES:SUBLANES + tile] = xc_ref[:, cs]
        ext[n, SUBLANES + tile:] = jnp.where(next_ok, xn_ref[:, cs], 0.0)
        taps = {u: ext[n, pl.ds(SUBLANES + u, quarter, stride=RADIX), :]
                for u in range(-CONV_LEFT, RADIX + CONV_WIDTH - 1 - CONV_LEFT)}
        xc_t = []
        for t in range(RADIX):
            acc = cb_ref[:, cs]
            for j in range(CONV_WIDTH):
                acc = acc + taps[t + j - CONV_LEFT] * cw_ref[j:j + 1, cs]
            xc_t.append(acc)
        xc = jnp.concatenate(xc_t, axis=0)
        z = jnp.dot(xc.astype(jnp.bfloat16), wg_ref[n], preferred_element_type=jnp.float32) + bg_ref[n]
        th_r = jnp.tanh(z[:, :RNN_BLOCK])
        th_i = jnp.tanh(z[:, RNN_BLOCK:])
        hsp = half_csp[:, cs]
        neg_log_a = hsp + hsp * th_r
        a = jnp.exp(-neg_log_a)
        half_xc = 0.5 * xc
        b = jnp.sqrt(jnp.tanh(neg_log_a) * (1.0 + a * a)) * (half_xc + half_xc * th_i)
        av = [a[t * quarter:(t + 1) * quarter] for t in range(RADIX)]
        bv = [b[t * quarter:(t + 1) * quarter] for t in range(RADIX)]

        def level2(p, h, n=n, cs=cs):
            sum_a[...] = p
            sum_b[...] = h
            av2 = [sum_a[pl.ds(t, sixteenth, stride=RADIX), :] for t in range(RADIX)]
            bv2 = [sum_b[pl.ds(t, sixteenth, stride=RADIX), :] for t in range(RADIX)]
            _, excl2, state = _radix_scan(
                av2, bv2, lambda p3, h3: _vreg_scan(p3, h3, carry[:, cs], reverse), reverse)
            for t in range(RADIX):
                ent_s[pl.ds(t, sixteenth, stride=RADIX), :] = excl2[t]
            carry[:, cs] = state
            return ent_s[...], state

        incl, _, _ = _radix_scan(av, bv, level2, reverse)
        for t in range(RADIX):
            rows = slice(t * quarter, (t + 1) * quarter)
            if reverse:
                ubuf[n, pl.ds(t, quarter, stride=RADIX), :] = incl[t] + hf_ref[rows, cs]
            else:
                out_ref[rows, cs] = incl[t]

    if reverse:
        for n in range(nblk):
            cs = slice(n * RNN_BLOCK, (n + 1) * RNN_BLOCK)
            half_gr = 0.5 * gr_ref[:, cs]
            silu = half_gr + half_gr * jnp.tanh(half_gr)
            out_ref[:, cs] = (ubuf[n] * silu).astype(out_ref.dtype)


def _rglru(rest, conv_w, conv_b, wg, bg, lam, reverse, h_fwd=None):
    b, s, _ = rest.shape
    tile = RNN_TILE
    n_tiles = s // tile
    nslab = D_RNN // RNN_SLAB
    sub_per_tile = tile // SUBLANES
    n_sub = s // SUBLANES
    nblk = RNN_SLAB // RNN_BLOCK
    quarter = tile // RADIX

    def tidx(i):
        return (n_tiles - 1 - i) if reverse else i

    x_cur = pl.BlockSpec((None, tile, RNN_SLAB), lambda bb, c, i: (bb, tidx(i), c))
    x_prev = pl.BlockSpec((None, SUBLANES, RNN_SLAB),
                          lambda bb, c, i: (bb, jnp.maximum(tidx(i) * sub_per_tile - 1, 0), c))
    x_next = pl.BlockSpec((None, SUBLANES, RNN_SLAB),
                          lambda bb, c, i: (bb, jnp.minimum((tidx(i) + 1) * sub_per_tile, n_sub - 1), c))
    in_specs = [
        x_prev, x_cur, x_next,
        pl.BlockSpec((CONV_WIDTH, RNN_SLAB), lambda bb, c, i: (0, c)),
        pl.BlockSpec((1, RNN_SLAB), lambda bb, c, i: (0, c)),
        pl.BlockSpec((nblk, RNN_BLOCK, 2 * RNN_BLOCK), lambda bb, c, i: (c, 0, 0)),
        pl.BlockSpec((nblk, 1, 2 * RNN_BLOCK), lambda bb, c, i: (c, 0, 0)),
        pl.BlockSpec((1, RNN_SLAB), lambda bb, c, i: (0, c)),
    ]
    args = [rest, rest, rest, conv_w, conv_b, wg, bg, lam]
    scratch = [
        pltpu.VMEM((nblk, tile + 2 * SUBLANES, RNN_BLOCK), jnp.float32),
        pltpu.VMEM((quarter, RNN_BLOCK), jnp.float32),
        pltpu.VMEM((quarter, RNN_BLOCK), jnp.float32),
        pltpu.VMEM((quarter, RNN_BLOCK), jnp.float32),
        pltpu.VMEM((1, RNN_SLAB), jnp.float32),
    ]
    if reverse:
        in_specs.append(pl.BlockSpec((None, tile, RNN_SLAB), lambda bb, c, i: (bb, tidx(i), c)))
        in_specs.append(pl.BlockSpec((None, tile, RNN_SLAB), lambda bb, c, i: (bb, tidx(i), nslab + c)))
        args += [h_fwd, rest]
        scratch.append(pltpu.VMEM((nblk, tile, RNN_BLOCK), jnp.float32))
    out_dtype = jnp.bfloat16 if reverse else jnp.float32
    return pl.pallas_call(
        functools.partial(_rglru_kernel, tile=tile, n_tiles=n_tiles, reverse=reverse),
        grid=(b, nslab, n_tiles),
        in_specs=in_specs,
        out_specs=pl.BlockSpec((None, tile, RNN_SLAB), lambda bb, c, i: (bb, tidx(i), c)),
        out_shape=jax.ShapeDtypeStruct((b, s, D_RNN), out_dtype),
        scratch_shapes=scratch,
        compiler_params=_params("parallel", "parallel", "arbitrary"),
        name="rglru_bwd" if reverse else "rglru_fwd",
    )(*args)


def _merge_kernel(x_ref, o0_ref, o1_ref, o2_ref, l0_ref, l1_ref, l2_ref, ga_ref, ur_ref,
                  gma_ref, gmr_ref, wa_ref, wr_ref, wo_ref, lng_ref, lnb_ref, y_ref, ybf_ref):
    def heads(ref):
        return jnp.concatenate([ref[h] for h in range(HEADS_PER_GROUP)], axis=1)

    l0, l1, l2 = heads(l0_ref), heads(l1_ref), heads(l2_ref)
    mx = jnp.maximum(jnp.maximum(l0, l1), l2)
    e0, e1, e2 = jnp.exp(l0 - mx), jnp.exp(l1 - mx), jnp.exp(l2 - mx)
    inv = 1.0 / (e0 + e1 + e2)
    oa = (e0 * inv) * heads(o0_ref) + (e1 * inv) * heads(o1_ref) + (e2 * inv) * heads(o2_ref)
    ga = ga_ref[...]
    ua = (oa * (ga * _sigmoid(ga))).astype(jnp.bfloat16)
    ya = jnp.dot(ua, wa_ref[...], preferred_element_type=jnp.float32)
    yr = jnp.dot(ur_ref[...], wr_ref[...], preferred_element_type=jnp.float32)
    mixed = _sigmoid(gma_ref[...]) * ya + _sigmoid(gmr_ref[...]) * yr
    out = jnp.dot(mixed.astype(jnp.bfloat16), wo_ref[...], preferred_element_type=jnp.float32)
    v = ALPHA * x_ref[...] + out
    mu = jnp.mean(v, axis=-1, keepdims=True)
    vc = v - mu
    var = jnp.mean(vc * vc, axis=-1, keepdims=True)
    y = vc * lax.rsqrt(var + LN_EPS) * lng_ref[...] + lnb_ref[...]
    y_ref[...] = y
    ybf_ref[...] = y.astype(jnp.bfloat16)


def _merge_out(x, attn, rest, u_r, wa, wr, wo, ln_g, ln_b):
    bsz, s, d = x.shape
    tm = MERGE_TM
    wide = D_MODEL // D_GROUP
    row = lambda w: pl.BlockSpec((None, tm, w), lambda bb, i: (bb, i, 0))
    head = pl.BlockSpec((None, HEADS_PER_GROUP, tm, HEAD_DIM), lambda bb, i: (bb, 0, i, 0))
    const = lambda shp: pl.BlockSpec(shp, lambda bb, i: (0, 0), pipeline_mode=pl.Buffered(1))
    in_specs = [
        row(d),
        head, head, head, head, head, head,
        pl.BlockSpec((None, tm, D_GROUP), lambda bb, i: (bb, i, 4 * wide)),
        row(D_RNN),
        pl.BlockSpec((None, tm, D_MODEL), lambda bb, i: (bb, i, 2)),
        pl.BlockSpec((None, tm, D_MODEL), lambda bb, i: (bb, i, 3)),
        const((D_GROUP, D_MODEL)), const((D_RNN, D_MODEL)), const((D_MODEL, D_MODEL)),
        const((1, D_MODEL)), const((1, D_MODEL)),
    ]
    (o0, l0), (o1, l1), (o2, l2) = attn
    return pl.pallas_call(
        _merge_kernel,
        grid=(bsz, s // tm),
        in_specs=in_specs,
        out_specs=[row(d), row(d)],
        out_shape=[jax.ShapeDtypeStruct((bsz, s, d), jnp.float32), jax.ShapeDtypeStruct((bsz, s, d), jnp.bfloat16)],
        compiler_params=_params("parallel", "parallel"),
        name="merge_out",
    )(x, o0, o1, o2, l0, l1, l2, rest, u_r, rest, rest, wa, wr, wo, ln_g, ln_b)


def _prep_layer(l, w_in, b_in, conv_w, conv_b, lru_w, lru_b, lru_lam, w_attn_o, w_rnn_o, w_out, ln_g, ln_b):
    bf = jnp.bfloat16
    a3 = 3 * D_ATTN
    c_ga, c_xr, c_gr, c_gm = a3, a3 + D_GROUP, a3 + D_GROUP + D_RNN, a3 + D_GROUP + 2 * D_RNN

    def rest_cols(w):
        return jnp.concatenate([w[..., c_xr:c_gr], w[..., c_gr:c_gm], w[..., c_gm:], w[..., c_ga:c_xr]], axis=-1)

    def group_cols(w, g):
        return jnp.concatenate([w[..., part * D_ATTN + g * D_GROUP:part * D_ATTN + (g + 1) * D_GROUP]
                                for part in range(3)], axis=-1)

    def gate_w(direction):
        w = lru_w[l, direction]
        return (0.5 * jnp.concatenate([w[0], w[1]], axis=-1)).astype(bf)

    def gate_b(direction):
        bgt = lru_b[l, direction]
        return 0.5 * jnp.concatenate([bgt[0], bgt[1]], axis=-1)[:, None, :]

    return dict(
        w_qkv=[group_cols(w_in[l], g).astype(bf) for g in range(N_GROUPS)],
        b_qkv=[group_cols(b_in[l], g)[None, :] for g in range(N_GROUPS)],
        w_rest=rest_cols(w_in[l]).astype(bf), b_rest=rest_cols(b_in[l])[None, :],
        conv_w=conv_w[l], conv_b=conv_b[l][None, :],
        wg=[gate_w(0), gate_w(1)], bg=[gate_b(0), gate_b(1)],
        lam=[lru_lam[l, 0][None, :], lru_lam[l, 1][None, :]],
        wa=w_attn_o[l].astype(bf), wr=w_rnn_o[l].astype(bf), wo=w_out[l].astype(bf),
        ln_g=ln_g[l][None, :], ln_b=ln_b[l][None, :],
    )


def _layer(x, x_bf, p, biases):
    attn = []
    for g, (_, dil) in enumerate(DILATED_GROUPS):
        qkv = _qkv_proj(x_bf, p["w_qkv"][g], p["b_qkv"][g], dil, g)
        attn.append(_attention_group(qkv, biases[g], g, dil))
    rest = _rest_proj(x_bf, p["w_rest"], p["b_rest"])
    h_fwd = _rglru(rest, p["conv_w"], p["conv_b"], p["wg"][0], p["bg"][0], p["lam"][0], False)
    u_r = _rglru(rest, p["conv_w"], p["conv_b"], p["wg"][1], p["bg"][1], p["lam"][1], True, h_fwd)
    return _merge_out(x, attn, rest, u_r, p["wa"], p["wr"], p["wo"], p["ln_g"], p["ln_b"])


def kernel(x_prompt, x_sample, w_in, b_in, conv_w, conv_b, lru_w, lru_b, lru_lam,
           w_attn_o, w_rnn_o, w_out, ln_g, ln_b, rel_bias):
    layers = [_prep_layer(l, w_in, b_in, conv_w, conv_b, lru_w, lru_b, lru_lam,
                          w_attn_o, w_rnn_o, w_out, ln_g, ln_b) for l in range(DEPTH)]
    biases = [_band_bias(rel_bias, g, dil) for g, (_, dil) in enumerate(DILATED_GROUPS)]

    def trunk(x):
        x_bf = x.astype(jnp.bfloat16)
        for p in layers:
            x, x_bf = _layer(x, x_bf, p, biases)
        return x

    return (trunk(x_prompt), trunk(x_sample))
```

```python
import functools

import numpy as np
import jax
import jax.numpy as jnp
from jax import lax
from jax.experimental import pallas as pl
from jax.experimental.pallas import tpu as pltpu

D_MODEL = 2048
DEPTH = 2
HEAD_DIM = 128
HEADS_PER_GROUP = 4
DILATED_GROUPS = ((128, 1), (512, 4), (2048, 16))
N_GROUPS = len(DILATED_GROUPS)
D_ATTN = N_GROUPS * HEADS_PER_GROUP * HEAD_DIM
D_GROUP = HEADS_PER_GROUP * HEAD_DIM
D_RNN = D_MODEL
N_RNN_BLOCKS = 16
RNN_BLOCK = D_RNN // N_RNN_BLOCKS
CONV_WIDTH = 4
CONV_LEFT = 2
LRU_C = 8.0
REL_BUCKETS = 32
REL_MAX_DIST = 1024
ALPHA = (2.0 * DEPTH) ** 0.25
LN_EPS = 1e-5
NEG_INF = -1e30

LANES = 128
SUBLANES = 8
VMEM_LIMIT_BYTES = 56 * 1024 * 1024

HALF = 64
Q_SUB = 128
K_WIN = Q_SUB + 2 * HALF
ATTN_TOKENS = 2048
ATTN_RESIDUES_PER_TRIP = 4

RNN_TILE = 512
RNN_SLAB = 1024
RADIX = 4

MM_TOKENS = RNN_TILE
MM_PIECE = 256

PROJ_TM = 2048
MERGE_TM = 256

GM_SPLIT = D_MODEL // 2
GATE1_COLS = D_RNN + GM_SPLIT
GATE2_COLS = (D_MODEL - GM_SPLIT) + D_MODEL + D_GROUP


def _params(*sem):
    return pltpu.CompilerParams(dimension_semantics=sem, vmem_limit_bytes=VMEM_LIMIT_BYTES)


def _sigmoid(x):
    return 0.5 + 0.5 * jnp.tanh(0.5 * x)


def _rest_proj_kernel(x_ref, w_ref, b_ref, o_ref):
    acc = jnp.dot(x_ref[...], w_ref[...], preferred_element_type=jnp.float32)
    o_ref[...] = acc + b_ref[...]


def _rest_proj(x_bf, w_bf, b):
    bsz, s, d = x_bf.shape
    n = w_bf.shape[1]
    tm, tn = PROJ_TM, 2 * D_GROUP
    return pl.pallas_call(
        _rest_proj_kernel,
        grid=(bsz, s // tm, n // tn),
        in_specs=[
            pl.BlockSpec((None, tm, d), lambda bb, i, j: (bb, i, 0)),
            pl.BlockSpec((d, tn), lambda bb, i, j: (0, j)),
            pl.BlockSpec((1, tn), lambda bb, i, j: (0, j)),
        ],
        out_specs=pl.BlockSpec((None, tm, tn), lambda bb, i, j: (bb, i, j)),
        out_shape=jax.ShapeDtypeStruct((bsz, s, n), jnp.float32),
        compiler_params=_params("parallel", "parallel", "arbitrary"),
        name="xr_proj",
    )(x_bf, w_bf, b)


def _qkv_proj_kernel(x_ref, w_ref, b_ref, o_ref, *scratch, dil, tm, tn):
    acc = jnp.dot(x_ref[...], w_ref[...], preferred_element_type=jnp.float32) + b_ref[...]
    if dil == 1:
        o_ref[0] = acc.astype(o_ref.dtype)
        return
    (acc_s,) = scratch
    rows = tm // dil
    for c in range(tn // LANES):
        cs = slice(c * LANES, (c + 1) * LANES)
        acc_s[c] = acc[:, cs]
        for r in range(dil):
            o_ref[r, :, cs] = acc_s[c, pl.ds(r, rows, stride=dil), :].astype(o_ref.dtype)


def _qkv_proj(x_bf, w_bf, b, dil, g):
    bsz, s, d = x_bf.shape
    n = w_bf.shape[1]
    tm, tn = PROJ_TM, n // 2
    scratch = [] if dil == 1 else [pltpu.VMEM((tn // LANES, tm, LANES), jnp.float32)]
    return pl.pallas_call(
        functools.partial(_qkv_proj_kernel, dil=dil, tm=tm, tn=tn),
        grid=(bsz, s // tm, n // tn),
        in_specs=[
            pl.BlockSpec((None, tm, d), lambda bb, i, j: (bb, i, 0)),
            pl.BlockSpec((d, tn), lambda bb, i, j: (0, j)),
            pl.BlockSpec((1, tn), lambda bb, i, j: (0, j)),
        ],
        out_specs=pl.BlockSpec((None, dil, tm // dil, tn), lambda bb, i, j: (bb, 0, i, j)),
        out_shape=jax.ShapeDtypeStruct((bsz, dil, s // dil, n), jnp.bfloat16),
        scratch_shapes=scratch,
        compiler_params=_params("parallel", "parallel", "arbitrary"),
        name=f"qkv_proj_g{g}",
    )(x_bf, w_bf, b)


def _t5_bucket(rel):
    nb = REL_BUCKETS // 2
    max_exact = nb // 2
    ret = (rel > 0).astype(np.int32) * nb
    n = np.abs(rel)
    large = max_exact + (np.log(np.maximum(n, max_exact) / max_exact)
                         / np.log(REL_MAX_DIST / max_exact) * (nb - max_exact)).astype(np.int32)
    large = np.minimum(large, nb - 1)
    return (ret + np.where(n < max_exact, n, large)).astype(np.int32)


def _band_bias(rel_bias, g, dil):
    tab = rel_bias[_t5_bucket(np.arange(-HALF, HALF + 1) * dil)][:, g * HEADS_PER_GROUP:(g + 1) * HEADS_PER_GROUP]
    period = Q_SUB + K_WIN + LANES
    vec = jnp.full((HEADS_PER_GROUP, period), NEG_INF, jnp.float32).at[:, :2 * HALF + 1].set(tab.T)
    mat = jnp.tile(vec, (1, Q_SUB))[:, :Q_SUB * (period - 1)].reshape(HEADS_PER_GROUP, Q_SUB, period - 1)
    return mat[:, :, :K_WIN]


def _attn_kernel(q_ref, kp_ref, kc_ref, kn_ref, vp_ref, vc_ref, vn_ref, bias_ref,
                 o_ref, lse_ref, kbuf, vbuf, *, dil, tile, length):
    i = pl.program_id(1)
    scale = HEAD_DIM ** -0.5

    def residue(r, slot):
        kbuf[slot, 0:HALF] = kp_ref[r]
        kbuf[slot, HALF:HALF + tile] = kc_ref[r]
        kbuf[slot, HALF + tile:] = kn_ref[r]
        vbuf[slot, 0:HALF] = vp_ref[r]
        vbuf[slot, HALF:HALF + tile] = vc_ref[r]
        vbuf[slot, HALF + tile:] = vn_ref[r]
        for sb in range(tile // Q_SUB):
            r0 = sb * Q_SUB
            kpos = i * tile + (r0 - HALF) + lax.broadcasted_iota(jnp.int32, (Q_SUB, K_WIN), 1)
            in_seq = (kpos >= 0) & (kpos < length)
            if dil == 1:
                rows = slice(r0, r0 + Q_SUB)
            else:
                rows = pl.ds(r + r0 * dil, Q_SUB, stride=dil)
            for h in range(HEADS_PER_GROUP):
                cs = slice(h * HEAD_DIM, (h + 1) * HEAD_DIM)
                q = q_ref[r, r0:r0 + Q_SUB, cs]
                k = kbuf[slot, r0:r0 + K_WIN, cs]
                v = vbuf[slot, r0:r0 + K_WIN, cs]
                s = lax.dot_general(q, k, (((1,), (1,)), ((), ())),
                                    preferred_element_type=jnp.float32) * scale
                s = jnp.where(in_seq, s + bias_ref[h], NEG_INF)
                m = jnp.max(s, axis=-1, keepdims=True)
                p = jnp.exp(s - m)
                den = jnp.sum(p, axis=-1, keepdims=True)
                o = jnp.dot(p.astype(jnp.bfloat16), v, preferred_element_type=jnp.float32)
                o_ref[h, rows, :] = o / den
                lse_ref[h, rows, :] = jnp.broadcast_to(m + jnp.log(den), (Q_SUB, HEAD_DIM))

    per_trip = kbuf.shape[0]

    def trip(it, _):
        for slot in range(per_trip):
            residue(it * per_trip + slot, slot)
        return 0

    if dil == per_trip:
        trip(0, 0)
    else:
        lax.fori_loop(0, dil // per_trip, trip, 0)


def _attention_group(qkv, bias, g, dil):
    bsz, _, length, _ = qkv.shape
    s = length * dil
    tile = ATTN_TOKENS // dil
    per_trip = min(dil, ATTN_RESIDUES_PER_TRIP)
    hb = tile // HALF
    n_half_blocks = length // HALF

    def cur(part):
        return pl.BlockSpec((None, dil, tile, D_GROUP), lambda bb, i: (bb, 0, i, part))

    def prev(part):
        return pl.BlockSpec((None, dil, HALF, D_GROUP),
                            lambda bb, i: (bb, 0, jnp.maximum(i * hb - 1, 0), part))

    def nxt(part):
        return pl.BlockSpec((None, dil, HALF, D_GROUP),
                            lambda bb, i: (bb, 0, jnp.minimum((i + 1) * hb, n_half_blocks - 1), part))

    out_spec = pl.BlockSpec((None, HEADS_PER_GROUP, tile * dil, HEAD_DIM), lambda bb, i: (bb, 0, i, 0))
    out_sds = jax.ShapeDtypeStruct((bsz, HEADS_PER_GROUP, s, HEAD_DIM), jnp.float32)
    return pl.pallas_call(
        functools.partial(_attn_kernel, dil=dil, tile=tile, length=length),
        grid=(bsz, length // tile),
        in_specs=[cur(0), prev(1), cur(1), nxt(1), prev(2), cur(2), nxt(2),
                  pl.BlockSpec((HEADS_PER_GROUP, Q_SUB, K_WIN), lambda bb, i: (0, 0, 0))],
        out_specs=[out_spec, out_spec],
        out_shape=[out_sds, out_sds],
        scratch_shapes=[pltpu.VMEM((per_trip, tile + 2 * HALF, D_GROUP), jnp.bfloat16),
                        pltpu.VMEM((per_trip, tile + 2 * HALF, D_GROUP), jnp.bfloat16)],
        compiler_params=_params("parallel", "arbitrary"),
        name=f"attn_g{g}",
    )(qkv, qkv, qkv, qkv, qkv, qkv, qkv, bias)


def _vreg_scan(a, b, state, reverse):
    groups = a.shape[0] // SUBLANES
    a3 = a.reshape(groups, SUBLANES, LANES)
    b3 = b.reshape(groups, SUBLANES, LANES)
    sub = lax.broadcasted_iota(jnp.int32, (groups, SUBLANES, LANES), 1)
    for shift in (1, 2, 4):
        keep = (sub < SUBLANES - shift) if reverse else (sub >= shift)
        amount = SUBLANES - shift if reverse else shift
        a_sh = jnp.where(keep, pltpu.roll(a3, amount, axis=1), 1.0)
        b_sh = jnp.where(keep, pltpu.roll(b3, amount, axis=1), 0.0)
        b3 = a3 * b_sh + b3
        a3 = a3 * a_sh
    sub2 = lax.broadcasted_iota(jnp.int32, (SUBLANES, LANES), 0)
    entering = [None] * groups
    for gi in (range(groups - 1, -1, -1) if reverse else range(groups)):
        sb = jnp.broadcast_to(state, (SUBLANES, LANES))
        incl = b3[gi] + a3[gi] * sb
        if reverse:
            entering[gi] = jnp.where(sub2 == SUBLANES - 1, sb, pltpu.roll(incl, SUBLANES - 1, axis=0))
            state = incl[0:1]
        else:
            entering[gi] = jnp.where(sub2 == 0, sb, pltpu.roll(incl, 1, axis=0))
            state = incl[SUBLANES - 1:SUBLANES]
    return jnp.concatenate(entering, axis=0), state


def _radix_scan(av, bv, entering_fn, reverse):
    order = tuple(range(RADIX - 1, -1, -1)) if reverse else tuple(range(RADIX))
    h = [None] * RADIX
    p = [None] * RADIX
    h[order[0]], p[order[0]] = bv[order[0]], av[order[0]]
    for before, t in zip(order[:-1], order[1:]):
        h[t] = av[t] * h[before] + bv[t]
        p[t] = av[t] * p[before]
    entering, state = entering_fn(p[order[-1]], h[order[-1]])
    incl = [h[t] + p[t] * entering for t in range(RADIX)]
    excl = [None] * RADIX
    excl[order[0]] = entering
    for before, t in zip(order[:-1], order[1:]):
        excl[t] = incl[before]
    return incl, excl, state


def _rglru_kernel(*refs, tile, n_tiles, reverse):
    if reverse:
        (xm_ref, wm_ref, bm_ref, xp_ref, xc_ref, xn_ref, cw_ref, cb_ref, wg_ref, bg_ref, lam_ref, hf_ref, gr_ref,
         zm_ref, out_ref, ext, sum_a, sum_b, ent_s, carry, ubuf) = refs
    else:
        (xm_ref, wm_ref, bm_ref, xp_ref, xc_ref, xn_ref, cw_ref, cb_ref, wg_ref, bg_ref, lam_ref,
         zm_ref, out_ref, ext, sum_a, sum_b, ent_s, carry) = refs
    def gate_proj_piece(k):
        cols = slice(k * MM_PIECE, (k + 1) * MM_PIECE)
        zm_ref[:, cols] = (jnp.dot(xm_ref[...], wm_ref[:, cols], preferred_element_type=jnp.float32)
                           + bm_ref[:, cols])

    n_pieces = zm_ref.shape[1] // MM_PIECE
    step = pl.program_id(2)
    ti = (n_tiles - 1 - step) if reverse else step
    quarter = tile // RADIX
    sixteenth = quarter // RADIX
    nblk = RNN_SLAB // RNN_BLOCK

    @pl.when(step == 0)
    def _():
        carry[...] = jnp.zeros_like(carry)

    lam = lam_ref[...]
    half_csp = (0.5 * LRU_C) * (jnp.maximum(-lam, 0.0) + jnp.log1p(jnp.exp(-jnp.abs(lam))))
    prev_ok = ti > 0
    next_ok = ti < n_tiles - 1

    assert n_pieces <= nblk
    for n in range(nblk):
        cs = slice(n * RNN_BLOCK, (n + 1) * RNN_BLOCK)
        ext[n, 0:SUBLANES] = jnp.where(prev_ok, xp_ref[:, cs], 0.0)
        ext[n, SUBLANES:SUBLANES + tile] = xc_ref[:, cs]
        ext[n, SUBLANES + tile:] = jnp.where(next_ok, xn_ref[:, cs], 0.0)
        taps = {u: ext[n, pl.ds(SUBLANES + u, quarter, stride=RADIX), :]
                for u in range(-CONV_LEFT, RADIX + CONV_WIDTH - 1 - CONV_LEFT)}
        xc_t = []
        for t in range(RADIX):
            acc = cb_ref[:, cs]
            for j in range(CONV_WIDTH):
                acc = acc + taps[t + j - CONV_LEFT] * cw_ref[j:j + 1, cs]
            xc_t.append(acc)
        xc = jnp.concatenate(xc_t, axis=0)
        z = jnp.dot(xc.astype(jnp.bfloat16), wg_ref[n], preferred_element_type=jnp.float32) + bg_ref[n]
        if n < n_pieces:
            gate_proj_piece(n)
        th_r = jnp.tanh(z[:, :RNN_BLOCK])
        th_i = jnp.tanh(z[:, RNN_BLOCK:])
        hsp = half_csp[:, cs]
        neg_log_a = hsp + hsp * th_r
        a = jnp.exp(-neg_log_a)
        half_xc = 0.5 * xc
        b = jnp.sqrt(jnp.tanh(neg_log_a) * (1.0 + a * a)) * (half_xc + half_xc * th_i)
        av = [a[t * quarter:(t + 1) * quarter] for t in range(RADIX)]
        bv = [b[t * quarter:(t + 1) * quarter] for t in range(RADIX)]

        def level2(p, h, n=n, cs=cs):
            sum_a[...] = p
            sum_b[...] = h
            av2 = [sum_a[pl.ds(t, sixteenth, stride=RADIX), :] for t in range(RADIX)]
            bv2 = [sum_b[pl.ds(t, sixteenth, stride=RADIX), :] for t in range(RADIX)]
            _, excl2, state = _radix_scan(
                av2, bv2, lambda p3, h3: _vreg_scan(p3, h3, carry[:, cs], reverse), reverse)
            for t in range(RADIX):
                ent_s[pl.ds(t, sixteenth, stride=RADIX), :] = excl2[t]
            carry[:, cs] = state
            return ent_s[...], state

        incl, _, _ = _radix_scan(av, bv, level2, reverse)
        for t in range(RADIX):
            rows = slice(t * quarter, (t + 1) * quarter)
            if reverse:
                ubuf[n, pl.ds(t, quarter, stride=RADIX), :] = incl[t] + hf_ref[rows, cs]
            else:
                out_ref[rows, cs] = incl[t]

    if reverse:
        for n in range(nblk):
            cs = slice(n * RNN_BLOCK, (n + 1) * RNN_BLOCK)
            half_gr = 0.5 * gr_ref[:, cs]
            silu = half_gr + half_gr * jnp.tanh(half_gr)
            out_ref[:, cs] = (ubuf[n] * silu).astype(out_ref.dtype)


def _rglru(xr, x_bf, w_m, b_m, conv_w, conv_b, wg, bg, lam, reverse, h_fwd=None, gr_src=None):
    b, s, _ = xr.shape
    n_m = w_m.shape[1]
    tn_m = n_m // 2
    tile = RNN_TILE
    n_tiles = s // tile
    nslab = D_RNN // RNN_SLAB
    sub_per_tile = tile // SUBLANES
    n_sub = s // SUBLANES
    nblk = RNN_SLAB // RNN_BLOCK
    quarter = tile // RADIX

    def tidx(i):
        return (n_tiles - 1 - i) if reverse else i

    x_cur = pl.BlockSpec((None, tile, RNN_SLAB), lambda bb, c, i: (bb, tidx(i), c))
    x_prev = pl.BlockSpec((None, SUBLANES, RNN_SLAB),
                          lambda bb, c, i: (bb, jnp.maximum(tidx(i) * sub_per_tile - 1, 0), c))
    x_next = pl.BlockSpec((None, SUBLANES, RNN_SLAB),
                          lambda bb, c, i: (bb, jnp.minimum((tidx(i) + 1) * sub_per_tile, n_sub - 1), c))
    m_tiles = b * s // MM_TOKENS
    assert m_tiles * 2 == b * nslab * n_tiles

    def m_col(bb, c, i):
        return ((bb * nslab + c) * n_tiles + i) // m_tiles

    def m_row(bb, c, i):
        return ((bb * nslab + c) * n_tiles + i) % m_tiles

    in_specs = [
        pl.BlockSpec((MM_TOKENS, D_MODEL), lambda bb, c, i: (m_row(bb, c, i), 0)),
        pl.BlockSpec((D_MODEL, tn_m), lambda bb, c, i: (0, m_col(bb, c, i))),
        pl.BlockSpec((1, tn_m), lambda bb, c, i: (0, m_col(bb, c, i))),
        x_prev, x_cur, x_next,
        pl.BlockSpec((CONV_WIDTH, RNN_SLAB), lambda bb, c, i: (0, c)),
        pl.BlockSpec((1, RNN_SLAB), lambda bb, c, i: (0, c)),
        pl.BlockSpec((nblk, RNN_BLOCK, 2 * RNN_BLOCK), lambda bb, c, i: (c, 0, 0)),
        pl.BlockSpec((nblk, 1, 2 * RNN_BLOCK), lambda bb, c, i: (c, 0, 0)),
        pl.BlockSpec((1, RNN_SLAB), lambda bb, c, i: (0, c)),
    ]
    args = [x_bf.reshape(b * s, D_MODEL), w_m, b_m, xr, xr, xr, conv_w, conv_b, wg, bg, lam]
    scratch = [
        pltpu.VMEM((nblk, tile + 2 * SUBLANES, RNN_BLOCK), jnp.float32),
        pltpu.VMEM((quarter, RNN_BLOCK), jnp.float32),
        pltpu.VMEM((quarter, RNN_BLOCK), jnp.float32),
        pltpu.VMEM((quarter, RNN_BLOCK), jnp.float32),
        pltpu.VMEM((1, RNN_SLAB), jnp.float32),
    ]
    if reverse:
        in_specs.append(pl.BlockSpec((None, tile, RNN_SLAB), lambda bb, c, i: (bb, tidx(i), c)))
        in_specs.append(pl.BlockSpec((None, tile, RNN_SLAB), lambda bb, c, i: (bb, tidx(i), c)))
        args += [h_fwd, gr_src]
        scratch.append(pltpu.VMEM((nblk, tile, RNN_BLOCK), jnp.float32))
    out_dtype = jnp.bfloat16 if reverse else jnp.float32
    z_m, out = pl.pallas_call(
        functools.partial(_rglru_kernel, tile=tile, n_tiles=n_tiles, reverse=reverse),
        grid=(b, nslab, n_tiles),
        in_specs=in_specs,
        out_specs=[pl.BlockSpec((MM_TOKENS, tn_m), lambda bb, c, i: (m_row(bb, c, i), m_col(bb, c, i))),
                   pl.BlockSpec((None, tile, RNN_SLAB), lambda bb, c, i: (bb, tidx(i), c))],
        out_shape=[jax.ShapeDtypeStruct((b * s, n_m), jnp.float32),
                   jax.ShapeDtypeStruct((b, s, D_RNN), out_dtype)],
        scratch_shapes=scratch,
        compiler_params=_params("arbitrary", "arbitrary", "arbitrary"),
        name="rglru_bwd" if reverse else "rglru_fwd",
    )(*args)
    return z_m.reshape(b, s, n_m), out


def _merge_kernel(x_ref, o0_ref, o1_ref, o2_ref, l0_ref, l1_ref, l2_ref, ga_ref, ur_ref,
                  gma_lo_ref, gma_hi_ref, gmr_lo_ref, gmr_hi_ref,
                  wa_ref, wr_ref, wo_ref, lng_ref, lnb_ref, y_ref, ybf_ref):
    def heads(ref):
        return jnp.concatenate([ref[h] for h in range(HEADS_PER_GROUP)], axis=1)

    gma = jnp.concatenate([gma_lo_ref[...], gma_hi_ref[...]], axis=1)
    gmr = jnp.concatenate([gmr_lo_ref[...], gmr_hi_ref[...]], axis=1)

    l0, l1, l2 = heads(l0_ref), heads(l1_ref), heads(l2_ref)
    mx = jnp.maximum(jnp.maximum(l0, l1), l2)
    e0, e1, e2 = jnp.exp(l0 - mx), jnp.exp(l1 - mx), jnp.exp(l2 - mx)
    inv = 1.0 / (e0 + e1 + e2)
    oa = (e0 * inv) * heads(o0_ref) + (e1 * inv) * heads(o1_ref) + (e2 * inv) * heads(o2_ref)
    ga = ga_ref[...]
    ua = (oa * (ga * _sigmoid(ga))).astype(jnp.bfloat16)
    ya = jnp.dot(ua, wa_ref[...], preferred_element_type=jnp.float32)
    yr = jnp.dot(ur_ref[...], wr_ref[...], preferred_element_type=jnp.float32)
    mixed = _sigmoid(gma) * ya + _sigmoid(gmr) * yr
    out = jnp.dot(mixed.astype(jnp.bfloat16), wo_ref[...], preferred_element_type=jnp.float32)
    v = ALPHA * x_ref[...] + out
    mu = jnp.mean(v, axis=-1, keepdims=True)
    vc = v - mu
    var = jnp.mean(vc * vc, axis=-1, keepdims=True)
    y = vc * lax.rsqrt(var + LN_EPS) * lng_ref[...] + lnb_ref[...]
    y_ref[...] = y
    ybf_ref[...] = y.astype(jnp.bfloat16)


def _merge_out(x, attn, z1, z2, u_r, wa, wr, wo, ln_g, ln_b):
    bsz, s, d = x.shape
    tm = MERGE_TM
    row = lambda w: pl.BlockSpec((None, tm, w), lambda bb, i: (bb, i, 0))
    head = pl.BlockSpec((None, HEADS_PER_GROUP, tm, HEAD_DIM), lambda bb, i: (bb, 0, i, 0))
    const = lambda shp: pl.BlockSpec(shp, lambda bb, i: (0, 0), pipeline_mode=pl.Buffered(1))
    half = lambda j: pl.BlockSpec((None, tm, GM_SPLIT), lambda bb, i: (bb, i, j))
    in_specs = [
        row(d),
        head, head, head, head, head, head,
        pl.BlockSpec((None, tm, D_GROUP), lambda bb, i: (bb, i, (GATE2_COLS - D_GROUP) // D_GROUP)),
        row(D_RNN),
        half(D_RNN // GM_SPLIT),
        half(0),
        half(1), half(2),
        const((D_GROUP, D_MODEL)), const((D_RNN, D_MODEL)), const((D_MODEL, D_MODEL)),
        const((1, D_MODEL)), const((1, D_MODEL)),
    ]
    (o0, l0), (o1, l1), (o2, l2) = attn
    return pl.pallas_call(
        _merge_kernel,
        grid=(bsz, s // tm),
        in_specs=in_specs,
        out_specs=[row(d), row(d)],
        out_shape=[jax.ShapeDtypeStruct((bsz, s, d), jnp.float32), jax.ShapeDtypeStruct((bsz, s, d), jnp.bfloat16)],
        compiler_params=_params("parallel", "parallel"),
        name="merge_out",
    )(x, o0, o1, o2, l0, l1, l2, z2, u_r, z1, z2, z2, z2, wa, wr, wo, ln_g, ln_b)


def _prep_layer(l, w_in, b_in, conv_w, conv_b, lru_w, lru_b, lru_lam, w_attn_o, w_rnn_o, w_out, ln_g, ln_b):
    bf = jnp.bfloat16
    a3 = 3 * D_ATTN
    c_ga, c_xr, c_gr, c_gm = a3, a3 + D_GROUP, a3 + D_GROUP + D_RNN, a3 + D_GROUP + 2 * D_RNN

    def gate1_cols(w):
        return w[..., c_gr:c_gm + GM_SPLIT]

    def gate2_cols(w):
        return jnp.concatenate([w[..., c_gm + GM_SPLIT:], w[..., c_ga:c_xr]], axis=-1)

    def group_cols(w, g):
        return jnp.concatenate([w[..., part * D_ATTN + g * D_GROUP:part * D_ATTN + (g + 1) * D_GROUP]
                                for part in range(3)], axis=-1)

    def gate_w(direction):
        w = lru_w[l, direction]
        return (0.5 * jnp.concatenate([w[0], w[1]], axis=-1)).astype(bf)

    def gate_b(direction):
        bgt = lru_b[l, direction]
        return 0.5 * jnp.concatenate([bgt[0], bgt[1]], axis=-1)[:, None, :]

    return dict(
        w_qkv=[group_cols(w_in[l], g).astype(bf) for g in range(N_GROUPS)],
        b_qkv=[group_cols(b_in[l], g)[None, :] for g in range(N_GROUPS)],
        w_xr=w_in[l, :, c_xr:c_gr].astype(bf), b_xr=b_in[l, None, c_xr:c_gr],
        w_g1=gate1_cols(w_in[l]).astype(bf), b_g1=gate1_cols(b_in[l])[None, :],
        w_g2=gate2_cols(w_in[l]).astype(bf), b_g2=gate2_cols(b_in[l])[None, :],
        conv_w=conv_w[l], conv_b=conv_b[l][None, :],
        wg=[gate_w(0), gate_w(1)], bg=[gate_b(0), gate_b(1)],
        lam=[lru_lam[l, 0][None, :], lru_lam[l, 1][None, :]],
        wa=w_attn_o[l].astype(bf), wr=w_rnn_o[l].astype(bf), wo=w_out[l].astype(bf),
        ln_g=ln_g[l][None, :], ln_b=ln_b[l][None, :],
    )


def _layer(x, x_bf, p, biases):
    attn = []
    for g, (_, dil) in enumerate(DILATED_GROUPS):
        qkv = _qkv_proj(x_bf, p["w_qkv"][g], p["b_qkv"][g], dil, g)
        attn.append(_attention_group(qkv, biases[g], g, dil))
    xr = _rest_proj(x_bf, p["w_xr"], p["b_xr"])
    z1, h_fwd = _rglru(xr, x_bf, p["w_g1"], p["b_g1"], p["conv_w"], p["conv_b"],
                       p["wg"][0], p["bg"][0], p["lam"][0], False)
    z2, u_r = _rglru(xr, x_bf, p["w_g2"], p["b_g2"], p["conv_w"], p["conv_b"],
                     p["wg"][1], p["bg"][1], p["lam"][1], True, h_fwd, z1)
    return _merge_out(x, attn, z1, z2, u_r, p["wa"], p["wr"], p["wo"], p["ln_g"], p["ln_b"])


def kernel(x_prompt, x_sample, w_in, b_in, conv_w, conv_b, lru_w, lru_b, lru_lam,
           w_attn_o, w_rnn_o, w_out, ln_g, ln_b, rel_bias):
    layers = [_prep_layer(l, w_in, b_in, conv_w, conv_b, lru_w, lru_b, lru_lam,
                          w_attn_o, w_rnn_o, w_out, ln_g, ln_b) for l in range(DEPTH)]
    biases = [_band_bias(rel_bias, g, dil) for g, (_, dil) in enumerate(DILATED_GROUPS)]

    def trunk(x):
        x_bf = x.astype(jnp.bfloat16)
        for p in layers:
            x, x_bf = _layer(x, x_bf, p, biases)
        return x

    return (trunk(x_prompt), trunk(x_sample))
```

```python
import functools

import numpy as np
import jax
import jax.numpy as jnp
from jax import lax
from jax.experimental import pallas as pl
from jax.experimental.pallas import tpu as pltpu

D_MODEL = 2048
DEPTH = 2
HEAD_DIM = 128
HEADS_PER_GROUP = 4
DILATED_GROUPS = ((128, 1), (512, 4), (2048, 16))
N_GROUPS = len(DILATED_GROUPS)
D_ATTN = N_GROUPS * HEADS_PER_GROUP * HEAD_DIM
D_GROUP = HEADS_PER_GROUP * HEAD_DIM
D_RNN = D_MODEL
N_RNN_BLOCKS = 16
RNN_BLOCK = D_RNN // N_RNN_BLOCKS
CONV_WIDTH = 4
CONV_LEFT = 2
LRU_C = 8.0
REL_BUCKETS = 32
REL_MAX_DIST = 1024
ALPHA = (2.0 * DEPTH) ** 0.25
LN_EPS = 1e-5
NEG_INF = -1e30

LANES = 128
SUBLANES = 8
VMEM_LIMIT_BYTES = 56 * 1024 * 1024

HALF = 64
Q_SUB = 128
K_WIN = Q_SUB + 2 * HALF
ATTN_TOKENS = 2048
ATTN_RESIDUES_PER_TRIP = 4

RNN_TILE = 512
RNN_SLAB = 1024
RADIX = 4

MM_TOKENS = RNN_TILE
MM_PIECE = 256

PROJ_TM = 2048
MERGE_TM = 256

GM_SPLIT = D_MODEL // 2
GATE1_COLS = D_RNN + GM_SPLIT
GATE2_COLS = (D_MODEL - GM_SPLIT) + D_MODEL + D_GROUP


def _params(*sem):
    return pltpu.CompilerParams(dimension_semantics=sem, vmem_limit_bytes=VMEM_LIMIT_BYTES)


def _sigmoid(x):
    return 0.5 + 0.5 * jnp.tanh(0.5 * x)


def _rest_proj_kernel(x_ref, w_ref, b_ref, o_ref):
    acc = jnp.dot(x_ref[...], w_ref[...], preferred_element_type=jnp.float32)
    o_ref[...] = acc + b_ref[...]


def _rest_proj(x_bf, w_bf, b):
    bsz, s, d = x_bf.shape
    n = w_bf.shape[1]
    tm, tn = PROJ_TM, 2 * D_GROUP
    return pl.pallas_call(
        _rest_proj_kernel,
        grid=(bsz, s // tm, n // tn),
        in_specs=[
            pl.BlockSpec((None, tm, d), lambda bb, i, j: (bb, i, 0)),
            pl.BlockSpec((d, tn), lambda bb, i, j: (0, j)),
            pl.BlockSpec((1, tn), lambda bb, i, j: (0, j)),
        ],
        out_specs=pl.BlockSpec((None, tm, tn), lambda bb, i, j: (bb, i, j)),
        out_shape=jax.ShapeDtypeStruct((bsz, s, n), jnp.float32),
        compiler_params=_params("parallel", "parallel", "arbitrary"),
        name="xr_proj",
    )(x_bf, w_bf, b)


def _qkv_proj_kernel(x_ref, w_ref, b_ref, o_ref, *rest, dil, tm, tn, emit_bf16):
    if emit_bf16:
        xbf_ref, rest = rest[0], rest[1:]

        @pl.when(pl.program_id(2) == 0)
        def _():
            xbf_ref[...] = x_ref[...].astype(jnp.bfloat16)

        x = xbf_ref[...]
    else:
        x = x_ref[...]
    acc = jnp.dot(x, w_ref[...], preferred_element_type=jnp.float32) + b_ref[...]
    if dil == 1:
        o_ref[0] = acc.astype(o_ref.dtype)
        return
    (acc_s,) = rest
    rows = tm // dil
    for c in range(tn // LANES):
        cs = slice(c * LANES, (c + 1) * LANES)
        acc_s[c] = acc[:, cs]
        for r in range(dil):
            o_ref[r, :, cs] = acc_s[c, pl.ds(r, rows, stride=dil), :].astype(o_ref.dtype)


def _qkv_proj(x, w_bf, b, dil, g):
    bsz, s, d = x.shape
    n = w_bf.shape[1]
    emit_bf16 = x.dtype != jnp.bfloat16
    tm, tn = (PROJ_TM // 2 if emit_bf16 else PROJ_TM), n // 2
    scratch = [] if dil == 1 else [pltpu.VMEM((tn // LANES, tm, LANES), jnp.float32)]
    out_specs = [pl.BlockSpec((None, dil, tm // dil, tn), lambda bb, i, j: (bb, 0, i, j))]
    out_shape = [jax.ShapeDtypeStruct((bsz, dil, s // dil, n), jnp.bfloat16)]
    if emit_bf16:
        out_specs.append(pl.BlockSpec((None, tm, d), lambda bb, i, j: (bb, i, 0)))
        out_shape.append(jax.ShapeDtypeStruct((bsz, s, d), jnp.bfloat16))
    outs = pl.pallas_call(
        functools.partial(_qkv_proj_kernel, dil=dil, tm=tm, tn=tn, emit_bf16=emit_bf16),
        grid=(bsz, s // tm, n // tn),
        in_specs=[
            pl.BlockSpec((None, tm, d), lambda bb, i, j: (bb, i, 0)),
            pl.BlockSpec((d, tn), lambda bb, i, j: (0, j)),
            pl.BlockSpec((1, tn), lambda bb, i, j: (0, j)),
        ],
        out_specs=out_specs,
        out_shape=out_shape,
        scratch_shapes=scratch,
        compiler_params=_params("parallel", "parallel", "arbitrary"),
        name=f"qkv_proj_g{g}",
    )(x, w_bf, b)
    return outs if emit_bf16 else outs[0]


def _t5_bucket(rel):
    nb = REL_BUCKETS // 2
    max_exact = nb // 2
    ret = (rel > 0).astype(np.int32) * nb
    n = np.abs(rel)
    large = max_exact + (np.log(np.maximum(n, max_exact) / max_exact)
                         / np.log(REL_MAX_DIST / max_exact) * (nb - max_exact)).astype(np.int32)
    large = np.minimum(large, nb - 1)
    return (ret + np.where(n < max_exact, n, large)).astype(np.int32)


def _band_bias(rel_bias, g, dil):
    tab = rel_bias[_t5_bucket(np.arange(-HALF, HALF + 1) * dil)][:, g * HEADS_PER_GROUP:(g + 1) * HEADS_PER_GROUP]
    period = Q_SUB + K_WIN + LANES
    vec = jnp.full((HEADS_PER_GROUP, period), NEG_INF, jnp.float32).at[:, :2 * HALF + 1].set(tab.T)
    mat = jnp.tile(vec, (1, Q_SUB))[:, :Q_SUB * (period - 1)].reshape(HEADS_PER_GROUP, Q_SUB, period - 1)
    return mat[:, :, :K_WIN]


def _attn_kernel(q_ref, kp_ref, kc_ref, kn_ref, vp_ref, vc_ref, vn_ref, bias_ref,
                 o_ref, lse_ref, kbuf, vbuf, *, dil, tile, length):
    i = pl.program_id(1)
    scale = HEAD_DIM ** -0.5

    def residue(r, slot):
        kbuf[slot, 0:HALF] = kp_ref[r]
        kbuf[slot, HALF:HALF + tile] = kc_ref[r]
        kbuf[slot, HALF + tile:] = kn_ref[r]
        vbuf[slot, 0:HALF] = vp_ref[r]
        vbuf[slot, HALF:HALF + tile] = vc_ref[r]
        vbuf[slot, HALF + tile:] = vn_ref[r]
        for sb in range(tile // Q_SUB):
            r0 = sb * Q_SUB
            kpos = i * tile + (r0 - HALF) + lax.broadcasted_iota(jnp.int32, (Q_SUB, K_WIN), 1)
            in_seq = (kpos >= 0) & (kpos < length)
            if dil == 1:
                rows = slice(r0, r0 + Q_SUB)
            else:
                rows = pl.ds(r + r0 * dil, Q_SUB, stride=dil)
            for h in range(HEADS_PER_GROUP):
                cs = slice(h * HEAD_DIM, (h + 1) * HEAD_DIM)
                q = q_ref[r, r0:r0 + Q_SUB, cs]
                k = kbuf[slot, r0:r0 + K_WIN, cs]
                v = vbuf[slot, r0:r0 + K_WIN, cs]
                s = lax.dot_general(q, k, (((1,), (1,)), ((), ())),
                                    preferred_element_type=jnp.float32) * scale
                s = jnp.where(in_seq, s + bias_ref[h], NEG_INF)
                m = jnp.max(s, axis=-1, keepdims=True)
                p = jnp.exp(s - m)
                den = jnp.sum(p, axis=-1, keepdims=True)
                o = jnp.dot(p.astype(jnp.bfloat16), v, preferred_element_type=jnp.float32)
                o_ref[h, rows, :] = o / den
                lse_ref[h, rows, :] = jnp.broadcast_to(m + jnp.log(den), (Q_SUB, HEAD_DIM))

    per_trip = kbuf.shape[0]

    def trip(it, _):
        for slot in range(per_trip):
            residue(it * per_trip + slot, slot)
        return 0

    if dil == per_trip:
        trip(0, 0)
    else:
        lax.fori_loop(0, dil // per_trip, trip, 0)


def _attention_group(qkv, bias, g, dil):
    bsz, _, length, _ = qkv.shape
    s = length * dil
    tile = ATTN_TOKENS // dil
    per_trip = min(dil, ATTN_RESIDUES_PER_TRIP)
    hb = tile // HALF
    n_half_blocks = length // HALF

    def cur(part):
        return pl.BlockSpec((None, dil, tile, D_GROUP), lambda bb, i: (bb, 0, i, part))

    def prev(part):
        return pl.BlockSpec((None, dil, HALF, D_GROUP),
                            lambda bb, i: (bb, 0, jnp.maximum(i * hb - 1, 0), part))

    def nxt(part):
        return pl.BlockSpec((None, dil, HALF, D_GROUP),
                            lambda bb, i: (bb, 0, jnp.minimum((i + 1) * hb, n_half_blocks - 1), part))

    out_spec = pl.BlockSpec((None, HEADS_PER_GROUP, tile * dil, HEAD_DIM), lambda bb, i: (bb, 0, i, 0))
    out_sds = jax.ShapeDtypeStruct((bsz, HEADS_PER_GROUP, s, HEAD_DIM), jnp.float32)
    return pl.pallas_call(
        functools.partial(_attn_kernel, dil=dil, tile=tile, length=length),
        grid=(bsz, length // tile),
        in_specs=[cur(0), prev(1), cur(1), nxt(1), prev(2), cur(2), nxt(2),
                  pl.BlockSpec((HEADS_PER_GROUP, Q_SUB, K_WIN), lambda bb, i: (0, 0, 0))],
        out_specs=[out_spec, out_spec],
        out_shape=[out_sds, out_sds],
        scratch_shapes=[pltpu.VMEM((per_trip, tile + 2 * HALF, D_GROUP), jnp.bfloat16),
                        pltpu.VMEM((per_trip, tile + 2 * HALF, D_GROUP), jnp.bfloat16)],
        compiler_params=_params("parallel", "arbitrary"),
        name=f"attn_g{g}",
    )(qkv, qkv, qkv, qkv, qkv, qkv, qkv, bias)


def _vreg_scan(a, b, state, reverse):
    groups = a.shape[0] // SUBLANES
    a3 = a.reshape(groups, SUBLANES, LANES)
    b3 = b.reshape(groups, SUBLANES, LANES)
    sub = lax.broadcasted_iota(jnp.int32, (groups, SUBLANES, LANES), 1)
    for shift in (1, 2, 4):
        keep = (sub < SUBLANES - shift) if reverse else (sub >= shift)
        amount = SUBLANES - shift if reverse else shift
        a_sh = jnp.where(keep, pltpu.roll(a3, amount, axis=1), 1.0)
        b_sh = jnp.where(keep, pltpu.roll(b3, amount, axis=1), 0.0)
        b3 = a3 * b_sh + b3
        a3 = a3 * a_sh
    sub2 = lax.broadcasted_iota(jnp.int32, (SUBLANES, LANES), 0)
    entering = [None] * groups
    for gi in (range(groups - 1, -1, -1) if reverse else range(groups)):
        sb = jnp.broadcast_to(state, (SUBLANES, LANES))
        incl = b3[gi] + a3[gi] * sb
        if reverse:
            entering[gi] = jnp.where(sub2 == SUBLANES - 1, sb, pltpu.roll(incl, SUBLANES - 1, axis=0))
            state = incl[0:1]
        else:
            entering[gi] = jnp.where(sub2 == 0, sb, pltpu.roll(incl, 1, axis=0))
            state = incl[SUBLANES - 1:SUBLANES]
    return jnp.concatenate(entering, axis=0), state


def _radix_scan(av, bv, entering_fn, reverse):
    order = tuple(range(RADIX - 1, -1, -1)) if reverse else tuple(range(RADIX))
    h = [None] * RADIX
    p = [None] * RADIX
    h[order[0]], p[order[0]] = bv[order[0]], av[order[0]]
    for before, t in zip(order[:-1], order[1:]):
        h[t] = av[t] * h[before] + bv[t]
        p[t] = av[t] * p[before]
    entering, state = entering_fn(p[order[-1]], h[order[-1]])
    incl = [h[t] + p[t] * entering for t in range(RADIX)]
    excl = [None] * RADIX
    excl[order[0]] = entering
    for before, t in zip(order[:-1], order[1:]):
        excl[t] = incl[before]
    return incl, excl, state


def _rglru_kernel(*refs, tile, n_tiles, reverse):
    if reverse:
        (xm_ref, wm_ref, bm_ref, xp_ref, xc_ref, xn_ref, cw_ref, cb_ref, wg_ref, bg_ref, lam_ref, hf_ref, gr_ref,
         zm_ref, out_ref, ext, sum_a, sum_b, ent_s, carry, ubuf) = refs
    else:
        (xm_ref, wm_ref, bm_ref, xp_ref, xc_ref, xn_ref, cw_ref, cb_ref, wg_ref, bg_ref, lam_ref,
         zm_ref, out_ref, ext, sum_a, sum_b, ent_s, carry) = refs
    def gate_proj_piece(k):
        cols = slice(k * MM_PIECE, (k + 1) * MM_PIECE)
        zm_ref[:, cols] = (jnp.dot(xm_ref[...], wm_ref[:, cols], preferred_element_type=jnp.float32)
                           + bm_ref[:, cols])

    n_pieces = zm_ref.shape[1] // MM_PIECE
    step = pl.program_id(2)
    ti = (n_tiles - 1 - step) if reverse else step
    quarter = tile // RADIX
    sixteenth = quarter // RADIX
    nblk = RNN_SLAB // RNN_BLOCK

    @pl.when(step == 0)
    def _():
        carry[...] = jnp.zeros_like(carry)

    lam = lam_ref[...]
    half_csp = (0.5 * LRU_C) * (jnp.maximum(-lam, 0.0) + jnp.log1p(jnp.exp(-jnp.abs(lam))))
    prev_ok = ti > 0
    next_ok = ti < n_tiles - 1

    assert n_pieces <= nblk
    for n in range(nblk):
        cs = slice(n * RNN_BLOCK, (n + 1) * RNN_BLOCK)
        ext[n, 0:SUBLANES] = jnp.where(prev_ok, xp_ref[:, cs], 0.0)
        ext[n, SUBLANES:SUBLANES + tile] = xc_ref[:, cs]
        ext[n, SUBLANES + tile:] = jnp.where(next_ok, xn_ref[:, cs], 0.0)
        taps = {u: ext[n, pl.ds(SUBLANES + u, quarter, stride=RADIX), :]
                for u in range(-CONV_LEFT, RADIX + CONV_WIDTH - 1 - CONV_LEFT)}
        xc_t = []
        for t in range(RADIX):
            acc = cb_ref[:, cs]
            for j in range(CONV_WIDTH):
                acc = acc + taps[t + j - CONV_LEFT] * cw_ref[j:j + 1, cs]
            xc_t.append(acc)
        xc = jnp.concatenate(xc_t, axis=0)
        z = jnp.dot(xc.astype(jnp.bfloat16), wg_ref[n], preferred_element_type=jnp.float32) + bg_ref[n]
        if n < n_pieces:
            gate_proj_piece(n)
        th_r = jnp.tanh(z[:, :RNN_BLOCK])
        th_i = jnp.tanh(z[:, RNN_BLOCK:])
        hsp = half_csp[:, cs]
        neg_log_a = hsp + hsp * th_r
        a = jnp.exp(-neg_log_a)
        half_xc = 0.5 * xc
        one_m_a2 = jnp.tanh(neg_log_a) * (1.0 + a * a)
        root = jnp.where(one_m_a2 > 0.0, one_m_a2 * lax.rsqrt(one_m_a2), 0.0)
        b = root * (half_xc + half_xc * th_i)
        av = [a[t * quarter:(t + 1) * quarter] for t in range(RADIX)]
        bv = [b[t * quarter:(t + 1) * quarter] for t in range(RADIX)]

        def level2(p, h, n=n, cs=cs):
            sum_a[...] = p
            sum_b[...] = h
            av2 = [sum_a[pl.ds(t, sixteenth, stride=RADIX), :] for t in range(RADIX)]
            bv2 = [sum_b[pl.ds(t, sixteenth, stride=RADIX), :] for t in range(RADIX)]
            _, excl2, state = _radix_scan(
                av2, bv2, lambda p3, h3: _vreg_scan(p3, h3, carry[:, cs], reverse), reverse)
            for t in range(RADIX):
                ent_s[pl.ds(t, sixteenth, stride=RADIX), :] = excl2[t]
            carry[:, cs] = state
            return ent_s[...], state

        incl, _, _ = _radix_scan(av, bv, level2, reverse)
        for t in range(RADIX):
            rows = slice(t * quarter, (t + 1) * quarter)
            if reverse:
                ubuf[n, pl.ds(t, quarter, stride=RADIX), :] = incl[t] + hf_ref[rows, cs]
            else:
                out_ref[rows, cs] = incl[t]

    if reverse:
        for n in range(nblk):
            cs = slice(n * RNN_BLOCK, (n + 1) * RNN_BLOCK)
            half_gr = 0.5 * gr_ref[:, cs]
            silu = half_gr + half_gr * jnp.tanh(half_gr)
            out_ref[:, cs] = (ubuf[n] * silu).astype(out_ref.dtype)


def _rglru(xr, x_bf, w_m, b_m, conv_w, conv_b, wg, bg, lam, reverse, h_fwd=None, gr_src=None):
    b, s, _ = xr.shape
    n_m = w_m.shape[1]
    tn_m = n_m // 2
    tile = RNN_TILE
    n_tiles = s // tile
    nslab = D_RNN // RNN_SLAB
    sub_per_tile = tile // SUBLANES
    n_sub = s // SUBLANES
    nblk = RNN_SLAB // RNN_BLOCK
    quarter = tile // RADIX

    def tidx(i):
        return (n_tiles - 1 - i) if reverse else i

    x_cur = pl.BlockSpec((None, tile, RNN_SLAB), lambda bb, c, i: (bb, tidx(i), c))
    x_prev = pl.BlockSpec((None, SUBLANES, RNN_SLAB),
                          lambda bb, c, i: (bb, jnp.maximum(tidx(i) * sub_per_tile - 1, 0), c))
    x_next = pl.BlockSpec((None, SUBLANES, RNN_SLAB),
                          lambda bb, c, i: (bb, jnp.minimum((tidx(i) + 1) * sub_per_tile, n_sub - 1), c))
    m_tiles = b * s // MM_TOKENS
    assert m_tiles * 2 == b * nslab * n_tiles

    def m_col(bb, c, i):
        return ((bb * nslab + c) * n_tiles + i) // m_tiles

    def m_row(bb, c, i):
        return ((bb * nslab + c) * n_tiles + i) % m_tiles

    in_specs = [
        pl.BlockSpec((MM_TOKENS, D_MODEL), lambda bb, c, i: (m_row(bb, c, i), 0)),
        pl.BlockSpec((D_MODEL, tn_m), lambda bb, c, i: (0, m_col(bb, c, i))),
        pl.BlockSpec((1, tn_m), lambda bb, c, i: (0, m_col(bb, c, i))),
        x_prev, x_cur, x_next,
        pl.BlockSpec((CONV_WIDTH, RNN_SLAB), lambda bb, c, i: (0, c)),
        pl.BlockSpec((1, RNN_SLAB), lambda bb, c, i: (0, c)),
        pl.BlockSpec((nblk, RNN_BLOCK, 2 * RNN_BLOCK), lambda bb, c, i: (c, 0, 0)),
        pl.BlockSpec((nblk, 1, 2 * RNN_BLOCK), lambda bb, c, i: (c, 0, 0)),
        pl.BlockSpec((1, RNN_SLAB), lambda bb, c, i: (0, c)),
    ]
    args = [x_bf.reshape(b * s, D_MODEL), w_m, b_m, xr, xr, xr, conv_w, conv_b, wg, bg, lam]
    scratch = [
        pltpu.VMEM((nblk, tile + 2 * SUBLANES, RNN_BLOCK), jnp.float32),
        pltpu.VMEM((quarter, RNN_BLOCK), jnp.float32),
        pltpu.VMEM((quarter, RNN_BLOCK), jnp.float32),
        pltpu.VMEM((quarter, RNN_BLOCK), jnp.float32),
        pltpu.VMEM((1, RNN_SLAB), jnp.float32),
    ]
    if reverse:
        in_specs.append(pl.BlockSpec((None, tile, RNN_SLAB), lambda bb, c, i: (bb, tidx(i), c)))
        in_specs.append(pl.BlockSpec((None, tile, RNN_SLAB), lambda bb, c, i: (bb, tidx(i), c)))
        args += [h_fwd, gr_src]
        scratch.append(pltpu.VMEM((nblk, tile, RNN_BLOCK), jnp.float32))
    out_dtype = jnp.bfloat16 if reverse else jnp.float32
    z_m, out = pl.pallas_call(
        functools.partial(_rglru_kernel, tile=tile, n_tiles=n_tiles, reverse=reverse),
        grid=(b, nslab, n_tiles),
        in_specs=in_specs,
        out_specs=[pl.BlockSpec((MM_TOKENS, tn_m), lambda bb, c, i: (m_row(bb, c, i), m_col(bb, c, i))),
                   pl.BlockSpec((None, tile, RNN_SLAB), lambda bb, c, i: (bb, tidx(i), c))],
        out_shape=[jax.ShapeDtypeStruct((b * s, n_m), jnp.float32),
                   jax.ShapeDtypeStruct((b, s, D_RNN), out_dtype)],
        scratch_shapes=scratch,
        compiler_params=_params("arbitrary", "arbitrary", "arbitrary"),
        name="rglru_bwd" if reverse else "rglru_fwd",
    )(*args)
    return z_m.reshape(b, s, n_m), out


def _merge_kernel(x_ref, o0_ref, o1_ref, o2_ref, l0_ref, l1_ref, l2_ref, ga_ref, ur_ref,
                  gma_lo_ref, gma_hi_ref, gmr_lo_ref, gmr_hi_ref,
                  wa_ref, wr_ref, wo_ref, lng_ref, lnb_ref, y_ref, ybf_ref):
    def heads(ref):
        return jnp.concatenate([ref[h] for h in range(HEADS_PER_GROUP)], axis=1)

    gma = jnp.concatenate([gma_lo_ref[...], gma_hi_ref[...]], axis=1)
    gmr = jnp.concatenate([gmr_lo_ref[...], gmr_hi_ref[...]], axis=1)

    l0, l1, l2 = heads(l0_ref), heads(l1_ref), heads(l2_ref)
    mx = jnp.maximum(jnp.maximum(l0, l1), l2)
    e0, e1, e2 = jnp.exp(l0 - mx), jnp.exp(l1 - mx), jnp.exp(l2 - mx)
    inv = 1.0 / (e0 + e1 + e2)
    oa = (e0 * inv) * heads(o0_ref) + (e1 * inv) * heads(o1_ref) + (e2 * inv) * heads(o2_ref)
    ga = ga_ref[...]
    ua = (oa * (ga * _sigmoid(ga))).astype(jnp.bfloat16)
    ya = jnp.dot(ua, wa_ref[...], preferred_element_type=jnp.float32)
    yr = jnp.dot(ur_ref[...], wr_ref[...], preferred_element_type=jnp.float32)
    mixed = _sigmoid(gma) * ya + _sigmoid(gmr) * yr
    out = jnp.dot(mixed.astype(jnp.bfloat16), wo_ref[...], preferred_element_type=jnp.float32)
    v = ALPHA * x_ref[...] + out
    mu = jnp.mean(v, axis=-1, keepdims=True)
    vc = v - mu
    var = jnp.mean(vc * vc, axis=-1, keepdims=True)
    y = vc * lax.rsqrt(var + LN_EPS) * lng_ref[...] + lnb_ref[...]
    y_ref[...] = y
    ybf_ref[...] = y.astype(jnp.bfloat16)


def _merge_out(x, attn, z1, z2, u_r, wa, wr, wo, ln_g, ln_b):
    bsz, s, d = x.shape
    tm = MERGE_TM
    row = lambda w: pl.BlockSpec((None, tm, w), lambda bb, i: (bb, i, 0))
    head = pl.BlockSpec((None, HEADS_PER_GROUP, tm, HEAD_DIM), lambda bb, i: (bb, 0, i, 0))
    const = lambda shp: pl.BlockSpec(shp, lambda bb, i: (0, 0), pipeline_mode=pl.Buffered(1))
    half = lambda j: pl.BlockSpec((None, tm, GM_SPLIT), lambda bb, i: (bb, i, j))
    in_specs = [
        row(d),
        head, head, head, head, head, head,
        pl.BlockSpec((None, tm, D_GROUP), lambda bb, i: (bb, i, (GATE2_COLS - D_GROUP) // D_GROUP)),
        row(D_RNN),
        half(D_RNN // GM_SPLIT),
        half(0),
        half(1), half(2),
        const((D_GROUP, D_MODEL)), const((D_RNN, D_MODEL)), const((D_MODEL, D_MODEL)),
        const((1, D_MODEL)), const((1, D_MODEL)),
    ]
    (o0, l0), (o1, l1), (o2, l2) = attn
    return pl.pallas_call(
        _merge_kernel,
        grid=(bsz, s // tm),
        in_specs=in_specs,
        out_specs=[row(d), row(d)],
        out_shape=[jax.ShapeDtypeStruct((bsz, s, d), jnp.float32), jax.ShapeDtypeStruct((bsz, s, d), jnp.bfloat16)],
        compiler_params=_params("parallel", "parallel"),
        name="merge_out",
    )(x, o0, o1, o2, l0, l1, l2, z2, u_r, z1, z2, z2, z2, wa, wr, wo, ln_g, ln_b)


def _prep_layer(l, w_in, b_in, conv_w, conv_b, lru_w, lru_b, lru_lam, w_attn_o, w_rnn_o, w_out, ln_g, ln_b):
    bf = jnp.bfloat16
    a3 = 3 * D_ATTN
    c_ga, c_xr, c_gr, c_gm = a3, a3 + D_GROUP, a3 + D_GROUP + D_RNN, a3 + D_GROUP + 2 * D_RNN

    def gate1_cols(w):
        return w[..., c_gr:c_gm + GM_SPLIT]

    def gate2_cols(w):
        return jnp.concatenate([w[..., c_gm + GM_SPLIT:], w[..., c_ga:c_xr]], axis=-1)

    def group_cols(w, g):
        return jnp.concatenate([w[..., part * D_ATTN + g * D_GROUP:part * D_ATTN + (g + 1) * D_GROUP]
                                for part in range(3)], axis=-1)

    def gate_w(direction):
        w = lru_w[l, direction]
        return (0.5 * jnp.concatenate([w[0], w[1]], axis=-1)).astype(bf)

    def gate_b(direction):
        bgt = lru_b[l, direction]
        return 0.5 * jnp.concatenate([bgt[0], bgt[1]], axis=-1)[:, None, :]

    return dict(
        w_qkv=[group_cols(w_in[l], g).astype(bf) for g in range(N_GROUPS)],
        b_qkv=[group_cols(b_in[l], g)[None, :] for g in range(N_GROUPS)],
        w_xr=w_in[l, :, c_xr:c_gr].astype(bf), b_xr=b_in[l, None, c_xr:c_gr],
        w_g1=gate1_cols(w_in[l]).astype(bf), b_g1=gate1_cols(b_in[l])[None, :],
        w_g2=gate2_cols(w_in[l]).astype(bf), b_g2=gate2_cols(b_in[l])[None, :],
        conv_w=conv_w[l], conv_b=conv_b[l][None, :],
        wg=[gate_w(0), gate_w(1)], bg=[gate_b(0), gate_b(1)],
        lam=[lru_lam[l, 0][None, :], lru_lam[l, 1][None, :]],
        wa=w_attn_o[l].astype(bf), wr=w_rnn_o[l].astype(bf), wo=w_out[l].astype(bf),
        ln_g=ln_g[l][None, :], ln_b=ln_b[l][None, :],
    )


def _layer(x, x_bf, p, biases):
    attn = []
    for g, (_, dil) in enumerate(DILATED_GROUPS):
        if x_bf is None:
            qkv, x_bf = _qkv_proj(x, p["w_qkv"][g], p["b_qkv"][g], dil, g)
        else:
            qkv = _qkv_proj(x_bf, p["w_qkv"][g], p["b_qkv"][g], dil, g)
        attn.append(_attention_group(qkv, biases[g], g, dil))
    xr = _rest_proj(x_bf, p["w_xr"], p["b_xr"])
    z1, h_fwd = _rglru(xr, x_bf, p["w_g1"], p["b_g1"], p["conv_w"], p["conv_b"],
                       p["wg"][0], p["bg"][0], p["lam"][0], False)
    z2, u_r = _rglru(xr, x_bf, p["w_g2"], p["b_g2"], p["conv_w"], p["conv_b"],
                     p["wg"][1], p["bg"][1], p["lam"][1], True, h_fwd, z1)
    return _merge_out(x, attn, z1, z2, u_r, p["wa"], p["wr"], p["wo"], p["ln_g"], p["ln_b"])


def kernel(x_prompt, x_sample, w_in, b_in, conv_w, conv_b, lru_w, lru_b, lru_lam,
           w_attn_o, w_rnn_o, w_out, ln_g, ln_b, rel_bias):
    layers = [_prep_layer(l, w_in, b_in, conv_w, conv_b, lru_w, lru_b, lru_lam,
                          w_attn_o, w_rnn_o, w_out, ln_g, ln_b) for l in range(DEPTH)]
    biases = [_band_bias(rel_bias, g, dil) for g, (_, dil) in enumerate(DILATED_GROUPS)]

    def trunk(x):
        x_bf = None
        for p in layers:
            x, x_bf = _layer(x, x_bf, p, biases)
        return x

    return (trunk(x_prompt), trunk(x_sample))
```

```python
import functools

import numpy as np
import jax
import jax.numpy as jnp
from jax import lax
from jax.experimental import pallas as pl
from jax.experimental.pallas import tpu as pltpu

D_MODEL = 2048
DEPTH = 2
HEAD_DIM = 128
HEADS_PER_GROUP = 4
DILATED_GROUPS = ((128, 1), (512, 4), (2048, 16))
N_GROUPS = len(DILATED_GROUPS)
D_ATTN = N_GROUPS * HEADS_PER_GROUP * HEAD_DIM
D_GROUP = HEADS_PER_GROUP * HEAD_DIM
D_RNN = D_MODEL
N_RNN_BLOCKS = 16
RNN_BLOCK = D_RNN // N_RNN_BLOCKS
CONV_WIDTH = 4
CONV_LEFT = 2
LRU_C = 8.0
REL_BUCKETS = 32
REL_MAX_DIST = 1024
ALPHA = (2.0 * DEPTH) ** 0.25
LN_EPS = 1e-5
NEG_INF = -1e30

LANES = 128
SUBLANES = 8
VMEM_LIMIT_BYTES = 56 * 1024 * 1024

HALF = 64
Q_SUB = 128
K_WIN = Q_SUB + 2 * HALF
ATTN_TOKENS = 2048
ATTN_RESIDUES_PER_TRIP = 4

RNN_TILE = 512
RNN_SLAB = 1024
RADIX = 4
GATE_BIAS_ROWS = 3
LOG2_E = 1.4426950408889634

MM_TOKENS = RNN_TILE
MM_PIECE = 256

PROJ_TM = 2048
MERGE_TM = 256

GM_SPLIT = D_MODEL // 2
GATE1_COLS = D_RNN + GM_SPLIT
GATE2_COLS = (D_MODEL - GM_SPLIT) + D_MODEL + D_GROUP


def _params(*sem):
    return pltpu.CompilerParams(dimension_semantics=sem, vmem_limit_bytes=VMEM_LIMIT_BYTES)


def _sigmoid(x):
    return 0.5 + 0.5 * jnp.tanh(0.5 * x)


def _rest_proj_kernel(x_ref, w_ref, b_ref, o_ref):
    acc = jnp.dot(x_ref[...], w_ref[...], preferred_element_type=jnp.float32)
    o_ref[...] = acc + b_ref[...]


def _rest_proj(x_bf, w_bf, b):
    bsz, s, d = x_bf.shape
    n = w_bf.shape[1]
    tm, tn = PROJ_TM, 2 * D_GROUP
    return pl.pallas_call(
        _rest_proj_kernel,
        grid=(bsz, s // tm, n // tn),
        in_specs=[
            pl.BlockSpec((None, tm, d), lambda bb, i, j: (bb, i, 0)),
            pl.BlockSpec((d, tn), lambda bb, i, j: (0, j)),
            pl.BlockSpec((1, tn), lambda bb, i, j: (0, j)),
        ],
        out_specs=pl.BlockSpec((None, tm, tn), lambda bb, i, j: (bb, i, j)),
        out_shape=jax.ShapeDtypeStruct((bsz, s, n), jnp.float32),
        compiler_params=_params("parallel", "parallel", "arbitrary"),
        name="xr_proj",
    )(x_bf, w_bf, b)


def _qkv_proj_kernel(x_ref, w_ref, b_ref, o_ref, *rest, dil, tm, tn, emit_bf16):
    if emit_bf16:
        xbf_ref, rest = rest[0], rest[1:]

        @pl.when(pl.program_id(2) == 0)
        def _():
            xbf_ref[...] = x_ref[...].astype(jnp.bfloat16)

        x = xbf_ref[...]
    else:
        x = x_ref[...]
    acc = jnp.dot(x, w_ref[...], preferred_element_type=jnp.float32) + b_ref[...]
    if dil == 1:
        o_ref[0] = acc.astype(o_ref.dtype)
        return
    (acc_s,) = rest
    rows = tm // dil
    for c in range(tn // LANES):
        cs = slice(c * LANES, (c + 1) * LANES)
        acc_s[c] = acc[:, cs]
        for r in range(dil):
            o_ref[r, :, cs] = acc_s[c, pl.ds(r, rows, stride=dil), :].astype(o_ref.dtype)


def _qkv_proj(x, w_bf, b, dil, g):
    bsz, s, d = x.shape
    n = w_bf.shape[1]
    emit_bf16 = x.dtype != jnp.bfloat16
    tm, tn = (PROJ_TM // 2 if emit_bf16 else PROJ_TM), n // 2
    scratch = [] if dil == 1 else [pltpu.VMEM((tn // LANES, tm, LANES), jnp.float32)]
    out_specs = [pl.BlockSpec((None, dil, tm // dil, tn), lambda bb, i, j: (bb, 0, i, j))]
    out_shape = [jax.ShapeDtypeStruct((bsz, dil, s // dil, n), jnp.bfloat16)]
    if emit_bf16:
        out_specs.append(pl.BlockSpec((None, tm, d), lambda bb, i, j: (bb, i, 0)))
        out_shape.append(jax.ShapeDtypeStruct((bsz, s, d), jnp.bfloat16))
    outs = pl.pallas_call(
        functools.partial(_qkv_proj_kernel, dil=dil, tm=tm, tn=tn, emit_bf16=emit_bf16),
        grid=(bsz, s // tm, n // tn),
        in_specs=[
            pl.BlockSpec((None, tm, d), lambda bb, i, j: (bb, i, 0)),
            pl.BlockSpec((d, tn), lambda bb, i, j: (0, j)),
            pl.BlockSpec((1, tn), lambda bb, i, j: (0, j)),
        ],
        out_specs=out_specs,
        out_shape=out_shape,
        scratch_shapes=scratch,
        compiler_params=_params("parallel", "parallel", "arbitrary"),
        name=f"qkv_proj_g{g}",
    )(x, w_bf, b)
    return outs if emit_bf16 else outs[0]


def _t5_bucket(rel):
    nb = REL_BUCKETS // 2
    max_exact = nb // 2
    ret = (rel > 0).astype(np.int32) * nb
    n = np.abs(rel)
    large = max_exact + (np.log(np.maximum(n, max_exact) / max_exact)
                         / np.log(REL_MAX_DIST / max_exact) * (nb - max_exact)).astype(np.int32)
    large = np.minimum(large, nb - 1)
    return (ret + np.where(n < max_exact, n, large)).astype(np.int32)


def _band_bias(rel_bias, g, dil):
    tab = rel_bias[_t5_bucket(np.arange(-HALF, HALF + 1) * dil)][:, g * HEADS_PER_GROUP:(g + 1) * HEADS_PER_GROUP]
    tab = tab * (HEAD_DIM ** 0.5)
    period = Q_SUB + K_WIN + LANES
    vec = jnp.full((HEADS_PER_GROUP, period), NEG_INF, jnp.float32).at[:, :2 * HALF + 1].set(tab.T)
    mat = jnp.tile(vec, (1, Q_SUB))[:, :Q_SUB * (period - 1)].reshape(HEADS_PER_GROUP, Q_SUB, period - 1)
    return mat[:, :, :K_WIN]


def _attn_kernel(q_ref, kp_ref, kc_ref, kn_ref, vp_ref, vc_ref, vn_ref, bias_ref,
                 o_ref, lse_ref, kbuf, vbuf, *, dil, tile, length):
    i = pl.program_id(1)
    scale = HEAD_DIM ** -0.5

    def residue(r, slot):
        kbuf[slot, 0:HALF] = kp_ref[r]
        kbuf[slot, HALF:HALF + tile] = kc_ref[r]
        kbuf[slot, HALF + tile:] = kn_ref[r]
        vbuf[slot, 0:HALF] = vp_ref[r]
        vbuf[slot, HALF:HALF + tile] = vc_ref[r]
        vbuf[slot, HALF + tile:] = vn_ref[r]
        n_sub = tile // Q_SUB
        for sb in range(n_sub):
            r0 = sb * Q_SUB
            at_edge = sb == 0 or sb == n_sub - 1
            if at_edge:
                kpos = i * tile + (r0 - HALF) + lax.broadcasted_iota(jnp.int32, (Q_SUB, K_WIN), 1)
                in_seq = (kpos >= 0) & (kpos < length)
            if dil == 1:
                rows = slice(r0, r0 + Q_SUB)
            else:
                rows = pl.ds(r + r0 * dil, Q_SUB, stride=dil)
            for h in range(HEADS_PER_GROUP):
                cs = slice(h * HEAD_DIM, (h + 1) * HEAD_DIM)
                q = q_ref[r, r0:r0 + Q_SUB, cs]
                k = kbuf[slot, r0:r0 + K_WIN, cs]
                v = vbuf[slot, r0:r0 + K_WIN, cs]
                t = lax.dot_general(q, k, (((1,), (1,)), ((), ())),
                                    preferred_element_type=jnp.float32) + bias_ref[h]
                if at_edge:
                    t = jnp.where(in_seq, t, NEG_INF)
                mt = jnp.max(t, axis=-1, keepdims=True)
                p = jnp.exp2((t - mt) * (scale * LOG2_E))
                den = jnp.sum(p, axis=-1, keepdims=True)
                o = jnp.dot(p.astype(jnp.bfloat16), v, preferred_element_type=jnp.float32)
                o_ref[h, rows, :] = o / den
                lse_ref[h, rows, :] = jnp.broadcast_to(mt * scale + jnp.log(den), (Q_SUB, HEAD_DIM))

    per_trip = kbuf.shape[0]

    def trip(it, _):
        for slot in range(per_trip):
            residue(it * per_trip + slot, slot)
        return 0

    if dil == per_trip:
        trip(0, 0)
    else:
        lax.fori_loop(0, dil // per_trip, trip, 0)


def _attention_group(qkv, bias, g, dil):
    bsz, _, length, _ = qkv.shape
    s = length * dil
    tile = ATTN_TOKENS // dil
    per_trip = min(dil, ATTN_RESIDUES_PER_TRIP)
    hb = tile // HALF
    n_half_blocks = length // HALF

    def cur(part):
        return pl.BlockSpec((None, dil, tile, D_GROUP), lambda bb, i: (bb, 0, i, part))

    def prev(part):
        return pl.BlockSpec((None, dil, HALF, D_GROUP),
                            lambda bb, i: (bb, 0, jnp.maximum(i * hb - 1, 0), part))

    def nxt(part):
        return pl.BlockSpec((None, dil, HALF, D_GROUP),
                            lambda bb, i: (bb, 0, jnp.minimum((i + 1) * hb, n_half_blocks - 1), part))

    out_spec = pl.BlockSpec((None, HEADS_PER_GROUP, tile * dil, HEAD_DIM), lambda bb, i: (bb, 0, i, 0))
    out_sds = jax.ShapeDtypeStruct((bsz, HEADS_PER_GROUP, s, HEAD_DIM), jnp.float32)
    return pl.pallas_call(
        functools.partial(_attn_kernel, dil=dil, tile=tile, length=length),
        grid=(bsz, length // tile),
        in_specs=[cur(0), prev(1), cur(1), nxt(1), prev(2), cur(2), nxt(2),
                  pl.BlockSpec((HEADS_PER_GROUP, Q_SUB, K_WIN), lambda bb, i: (0, 0, 0))],
        out_specs=[out_spec, out_spec],
        out_shape=[out_sds, out_sds],
        scratch_shapes=[pltpu.VMEM((per_trip, tile + 2 * HALF, D_GROUP), jnp.bfloat16),
                        pltpu.VMEM((per_trip, tile + 2 * HALF, D_GROUP), jnp.bfloat16)],
        compiler_params=_params("parallel", "arbitrary"),
        name=f"attn_g{g}",
    )(qkv, qkv, qkv, qkv, qkv, qkv, qkv, bias)


def _vreg_scan(a, b, state, reverse):
    groups = a.shape[0] // SUBLANES
    a3 = a.reshape(groups, SUBLANES, LANES)
    b3 = b.reshape(groups, SUBLANES, LANES)
    sub = lax.broadcasted_iota(jnp.int32, (groups, SUBLANES, LANES), 1)
    for shift in (1, 2, 4):
        keep = (sub < SUBLANES - shift) if reverse else (sub >= shift)
        amount = SUBLANES - shift if reverse else shift
        a_sh = jnp.where(keep, pltpu.roll(a3, amount, axis=1), 1.0)
        b_sh = jnp.where(keep, pltpu.roll(b3, amount, axis=1), 0.0)
        b3 = a3 * b_sh + b3
        a3 = a3 * a_sh
    sub2 = lax.broadcasted_iota(jnp.int32, (SUBLANES, LANES), 0)
    entering = [None] * groups
    for gi in (range(groups - 1, -1, -1) if reverse else range(groups)):
        sb = jnp.broadcast_to(state, (SUBLANES, LANES))
        incl = b3[gi] + a3[gi] * sb
        if reverse:
            entering[gi] = jnp.where(sub2 == SUBLANES - 1, sb, pltpu.roll(incl, SUBLANES - 1, axis=0))
            state = incl[0:1]
        else:
            entering[gi] = jnp.where(sub2 == 0, sb, pltpu.roll(incl, 1, axis=0))
            state = incl[SUBLANES - 1:SUBLANES]
    return jnp.concatenate(entering, axis=0), state


def _radix_scan(av, bv, entering_fn, reverse):
    order = tuple(range(RADIX - 1, -1, -1)) if reverse else tuple(range(RADIX))
    h = [None] * RADIX
    p = [None] * RADIX
    h[order[0]], p[order[0]] = bv[order[0]], av[order[0]]
    for before, t in zip(order[:-1], order[1:]):
        h[t] = av[t] * h[before] + bv[t]
        p[t] = av[t] * p[before]
    entering, state = entering_fn(p[order[-1]], h[order[-1]])
    incl = [h[t] + p[t] * entering for t in range(RADIX)]
    excl = [None] * RADIX
    excl[order[0]] = entering
    for before, t in zip(order[:-1], order[1:]):
        excl[t] = incl[before]
    return incl, excl, state


def _rglru_kernel(*refs, tile, n_tiles, reverse):
    if reverse:
        (xm_ref, wm_ref, bm_ref, conv_ref, wg_ref, lam_ref, hf_ref, gr_ref,
         zm_ref, out_ref, sum_a, sum_b, ent_s, carry, ubuf) = refs
    else:
        (xm_ref, wm_ref, bm_ref, xp_ref, xc_ref, xn_ref, cw_ref, cb_ref, wg_ref, lam_ref,
         zm_ref, out_ref, conv_ref, ext, sum_a, sum_b, ent_s, carry) = refs

    def gate_proj_piece(k):
        cols = slice(k * MM_PIECE, (k + 1) * MM_PIECE)
        zm_ref[:, cols] = (jnp.dot(xm_ref[...], wm_ref[:, cols], preferred_element_type=jnp.float32)
                           + bm_ref[:, cols])

    n_pieces = zm_ref.shape[1] // MM_PIECE
    step = pl.program_id(2)
    ti = (n_tiles - 1 - step) if reverse else step
    quarter = tile // RADIX
    sixteenth = quarter // RADIX
    nblk = RNN_SLAB // RNN_BLOCK

    @pl.when(step == 0)
    def _():
        carry[...] = jnp.zeros_like(carry)

    lam = lam_ref[...]
    half_csp = (0.5 * LRU_C) * (jnp.maximum(-lam, 0.0) + jnp.log1p(jnp.exp(-jnp.abs(lam))))
    bias_taps = (lax.broadcasted_iota(jnp.int32, (tile, RNN_BLOCK), 1) < GATE_BIAS_ROWS).astype(jnp.bfloat16)

    assert n_pieces <= nblk
    for n in range(nblk):
        cs = slice(n * RNN_BLOCK, (n + 1) * RNN_BLOCK)
        if reverse:
            xc = conv_ref[:, cs]
        else:
            ext[n, 0:SUBLANES] = jnp.where(ti > 0, xp_ref[:, cs], 0.0)
            ext[n, SUBLANES:SUBLANES + tile] = xc_ref[:, cs]
            ext[n, SUBLANES + tile:] = jnp.where(ti < n_tiles - 1, xn_ref[:, cs], 0.0)
            taps = {u: ext[n, pl.ds(SUBLANES + u, quarter, stride=RADIX), :]
                    for u in range(-CONV_LEFT, RADIX + CONV_WIDTH - 1 - CONV_LEFT)}
            xc_t = []
            for t in range(RADIX):
                acc = cb_ref[:, cs]
                for j in range(CONV_WIDTH):
                    acc = acc + taps[t + j - CONV_LEFT] * cw_ref[j:j + 1, cs]
                xc_t.append(acc)
            xc = jnp.concatenate(xc_t, axis=0)
            conv_ref[:, cs] = xc
        lhs = jnp.concatenate([xc.astype(jnp.bfloat16), bias_taps], axis=1)
        z = jnp.dot(lhs, wg_ref[n], preferred_element_type=jnp.float32)
        if n < n_pieces:
            gate_proj_piece(n)
        th_r = jnp.tanh(z[:, :RNN_BLOCK])
        th_i = jnp.tanh(z[:, RNN_BLOCK:])
        hsp = half_csp[:, cs]
        neg_log_a = hsp + hsp * th_r
        a = jnp.exp2(neg_log_a * (-LOG2_E))
        half_xc = 0.5 * xc
        one_m_a2 = jnp.tanh(neg_log_a) * (1.0 + a * a)
        root = jnp.where(one_m_a2 > 0.0, one_m_a2 * lax.rsqrt(one_m_a2), 0.0)
        b = root * (half_xc + half_xc * th_i)
        av = [a[t * quarter:(t + 1) * quarter] for t in range(RADIX)]
        bv = [b[t * quarter:(t + 1) * quarter] for t in range(RADIX)]

        def level2(p, h, n=n, cs=cs):
            sum_a[...] = p
            sum_b[...] = h
            av2 = [sum_a[pl.ds(t, sixteenth, stride=RADIX), :] for t in range(RADIX)]
            bv2 = [sum_b[pl.ds(t, sixteenth, stride=RADIX), :] for t in range(RADIX)]
            _, excl2, state = _radix_scan(
                av2, bv2, lambda p3, h3: _vreg_scan(p3, h3, carry[:, cs], reverse), reverse)
            for t in range(RADIX):
                ent_s[pl.ds(t, sixteenth, stride=RADIX), :] = excl2[t]
            carry[:, cs] = state
            return ent_s[...], state

        incl, _, _ = _radix_scan(av, bv, level2, reverse)
        for t in range(RADIX):
            rows = slice(t * quarter, (t + 1) * quarter)
            if reverse:
                ubuf[n, pl.ds(t, quarter, stride=RADIX), :] = incl[t] + hf_ref[rows, cs]
            else:
                out_ref[rows, cs] = incl[t]

    if reverse:
        for n in range(nblk):
            cs = slice(n * RNN_BLOCK, (n + 1) * RNN_BLOCK)
            half_gr = 0.5 * gr_ref[:, cs]
            silu = half_gr + half_gr * jnp.tanh(half_gr)
            out_ref[:, cs] = (ubuf[n] * silu).astype(out_ref.dtype)


def _rglru(xr, x_bf, w_m, b_m, conv_w, conv_b, wg, lam, reverse, h_fwd=None, gr_src=None):
    b, s, _ = xr.shape
    n_m = w_m.shape[1]
    tn_m = n_m // 2
    tile = RNN_TILE
    n_tiles = s // tile
    nslab = D_RNN // RNN_SLAB
    sub_per_tile = tile // SUBLANES
    n_sub = s // SUBLANES
    nblk = RNN_SLAB // RNN_BLOCK
    quarter = tile // RADIX

    def tidx(i):
        return (n_tiles - 1 - i) if reverse else i

    x_cur = pl.BlockSpec((None, tile, RNN_SLAB), lambda bb, c, i: (bb, tidx(i), c))
    x_prev = pl.BlockSpec((None, SUBLANES, RNN_SLAB),
                          lambda bb, c, i: (bb, jnp.maximum(tidx(i) * sub_per_tile - 1, 0), c))
    x_next = pl.BlockSpec((None, SUBLANES, RNN_SLAB),
                          lambda bb, c, i: (bb, jnp.minimum((tidx(i) + 1) * sub_per_tile, n_sub - 1), c))
    m_tiles = b * s // MM_TOKENS
    assert m_tiles * 2 == b * nslab * n_tiles

    def m_col(bb, c, i):
        return ((bb * nslab + c) * n_tiles + i) // m_tiles

    def m_row(bb, c, i):
        return ((bb * nslab + c) * n_tiles + i) % m_tiles

    time_tile = pl.BlockSpec((None, tile, RNN_SLAB), lambda bb, c, i: (bb, tidx(i), c))
    gate_w = pl.BlockSpec((nblk, 2 * RNN_BLOCK, 2 * RNN_BLOCK), lambda bb, c, i: (c, 0, 0))
    lam_spec = pl.BlockSpec((1, RNN_SLAB), lambda bb, c, i: (0, c))
    in_specs = [
        pl.BlockSpec((MM_TOKENS, D_MODEL), lambda bb, c, i: (m_row(bb, c, i), 0)),
        pl.BlockSpec((D_MODEL, tn_m), lambda bb, c, i: (0, m_col(bb, c, i))),
        pl.BlockSpec((1, tn_m), lambda bb, c, i: (0, m_col(bb, c, i))),
    ]
    args = [x_bf.reshape(b * s, D_MODEL), w_m, b_m]
    scan_scratch = [
        pltpu.VMEM((quarter, RNN_BLOCK), jnp.float32),
        pltpu.VMEM((quarter, RNN_BLOCK), jnp.float32),
        pltpu.VMEM((quarter, RNN_BLOCK), jnp.float32),
        pltpu.VMEM((1, RNN_SLAB), jnp.float32),
    ]
    out_specs = [pl.BlockSpec((MM_TOKENS, tn_m), lambda bb, c, i: (m_row(bb, c, i), m_col(bb, c, i))), time_tile]
    state_sds = jax.ShapeDtypeStruct((b, s, D_RNN), jnp.float32)
    if reverse:
        in_specs += [time_tile, gate_w, lam_spec, time_tile, time_tile]
        args += [xr, wg, lam, h_fwd, gr_src]
        scratch = scan_scratch + [pltpu.VMEM((nblk, tile, RNN_BLOCK), jnp.float32)]
        out_shape = [jax.ShapeDtypeStruct((b * s, n_m), jnp.float32),
                     jax.ShapeDtypeStruct((b, s, D_RNN), jnp.bfloat16)]
    else:
        in_specs += [x_prev, x_cur, x_next,
                     pl.BlockSpec((CONV_WIDTH, RNN_SLAB), lambda bb, c, i: (0, c)),
                     pl.BlockSpec((1, RNN_SLAB), lambda bb, c, i: (0, c)),
                     gate_w, lam_spec]
        args += [xr, xr, xr, conv_w, conv_b, wg, lam]
        scratch = [pltpu.VMEM((nblk, tile + 2 * SUBLANES, RNN_BLOCK), jnp.float32)] + scan_scratch
        out_specs.append(time_tile)
        out_shape = [jax.ShapeDtypeStruct((b * s, n_m), jnp.float32), state_sds, state_sds]
    outs = pl.pallas_call(
        functools.partial(_rglru_kernel, tile=tile, n_tiles=n_tiles, reverse=reverse),
        grid=(b, nslab, n_tiles),
        in_specs=in_specs,
        out_specs=out_specs,
        out_shape=out_shape,
        scratch_shapes=scratch,
        compiler_params=_params("arbitrary", "arbitrary", "arbitrary"),
        name="rglru_bwd" if reverse else "rglru_fwd",
    )(*args)
    return (outs[0].reshape(b, s, n_m),) + tuple(outs[1:])


def _merge_kernel(x_ref, o0_ref, o1_ref, o2_ref, l0_ref, l1_ref, l2_ref, ga_ref, ur_ref,
                  gma_lo_ref, gma_hi_ref, gmr_lo_ref, gmr_hi_ref,
                  wa_ref, wr_ref, wo_ref, lng_ref, lnb_ref, y_ref, ybf_ref):
    def heads(ref):
        return jnp.concatenate([ref[h] for h in range(HEADS_PER_GROUP)], axis=1)

    gma = jnp.concatenate([gma_lo_ref[...], gma_hi_ref[...]], axis=1)
    gmr = jnp.concatenate([gmr_lo_ref[...], gmr_hi_ref[...]], axis=1)

    l0, l1, l2 = heads(l0_ref), heads(l1_ref), heads(l2_ref)
    mx = jnp.maximum(jnp.maximum(l0, l1), l2)
    e0, e1, e2 = jnp.exp(l0 - mx), jnp.exp(l1 - mx), jnp.exp(l2 - mx)
    inv = 1.0 / (e0 + e1 + e2)
    oa = (e0 * inv) * heads(o0_ref) + (e1 * inv) * heads(o1_ref) + (e2 * inv) * heads(o2_ref)
    ga = ga_ref[...]
    ua = (oa * (ga * _sigmoid(ga))).astype(jnp.bfloat16)
    ya = jnp.dot(ua, wa_ref[...], preferred_element_type=jnp.float32)
    yr = jnp.dot(ur_ref[...], wr_ref[...], preferred_element_type=jnp.float32)
    mixed = _sigmoid(gma) * ya + _sigmoid(gmr) * yr
    out = jnp.dot(mixed.astype(jnp.bfloat16), wo_ref[...], preferred_element_type=jnp.float32)
    v = ALPHA * x_ref[...] + out
    mu = jnp.mean(v, axis=-1, keepdims=True)
    vc = v - mu
    var = jnp.mean(vc * vc, axis=-1, keepdims=True)
    y = vc * lax.rsqrt(var + LN_EPS) * lng_ref[...] + lnb_ref[...]
    y_ref[...] = y
    ybf_ref[...] = y.astype(jnp.bfloat16)


def _merge_out(x, attn, z1, z2, u_r, wa, wr, wo, ln_g, ln_b):
    bsz, s, d = x.shape
    tm = MERGE_TM
    row = lambda w: pl.BlockSpec((None, tm, w), lambda bb, i: (bb, i, 0))
    head = pl.BlockSpec((None, HEADS_PER_GROUP, tm, HEAD_DIM), lambda bb, i: (bb, 0, i, 0))
    const = lambda shp: pl.BlockSpec(shp, lambda bb, i: (0, 0), pipeline_mode=pl.Buffered(1))
    half = lambda j: pl.BlockSpec((None, tm, GM_SPLIT), lambda bb, i: (bb, i, j))
    in_specs = [
        row(d),
        head, head, head, head, head, head,
        pl.BlockSpec((None, tm, D_GROUP), lambda bb, i: (bb, i, (GATE2_COLS - D_GROUP) // D_GROUP)),
        row(D_RNN),
        half(D_RNN // GM_SPLIT),
        half(0),
        half(1), half(2),
        const((D_GROUP, D_MODEL)), const((D_RNN, D_MODEL)), const((D_MODEL, D_MODEL)),
        const((1, D_MODEL)), const((1, D_MODEL)),
    ]
    (o0, l0), (o1, l1), (o2, l2) = attn
    return pl.pallas_call(
        _merge_kernel,
        grid=(bsz, s // tm),
        in_specs=in_specs,
        out_specs=[row(d), row(d)],
        out_shape=[jax.ShapeDtypeStruct((bsz, s, d), jnp.float32), jax.ShapeDtypeStruct((bsz, s, d), jnp.bfloat16)],
        compiler_params=_params("parallel", "parallel"),
        name="merge_out",
    )(x, o0, o1, o2, l0, l1, l2, z2, u_r, z1, z2, z2, z2, wa, wr, wo, ln_g, ln_b)


def _prep_layer(l, w_in, b_in, conv_w, conv_b, lru_w, lru_b, lru_lam, w_attn_o, w_rnn_o, w_out, ln_g, ln_b):
    bf = jnp.bfloat16
    a3 = 3 * D_ATTN
    c_ga, c_xr, c_gr, c_gm = a3, a3 + D_GROUP, a3 + D_GROUP + D_RNN, a3 + D_GROUP + 2 * D_RNN

    def gate1_cols(w):
        return w[..., c_gr:c_gm + GM_SPLIT]

    def gate2_cols(w):
        return jnp.concatenate([w[..., c_gm + GM_SPLIT:], w[..., c_ga:c_xr]], axis=-1)

    def group_cols(w, g):
        return jnp.concatenate([w[..., part * D_ATTN + g * D_GROUP:part * D_ATTN + (g + 1) * D_GROUP]
                                for part in range(3)], axis=-1)

    def gate_w(direction):
        w = 0.5 * jnp.concatenate([lru_w[l, direction, 0], lru_w[l, direction, 1]], axis=-1)
        bias = 0.5 * jnp.concatenate([lru_b[l, direction, 0], lru_b[l, direction, 1]], axis=-1)
        terms, rem = [], bias
        for _ in range(GATE_BIAS_ROWS):
            term = rem.astype(bf)
            terms.append(term)
            rem = rem - term.astype(jnp.float32)
        pad = jnp.zeros((N_RNN_BLOCKS, RNN_BLOCK - GATE_BIAS_ROWS, 2 * RNN_BLOCK), bf)
        return jnp.concatenate([w.astype(bf), jnp.stack(terms, axis=1), pad], axis=1)

    return dict(
        w_qkv=[group_cols(w_in[l], g).astype(bf) for g in range(N_GROUPS)],
        b_qkv=[group_cols(b_in[l], g)[None, :] for g in range(N_GROUPS)],
        w_xr=w_in[l, :, c_xr:c_gr].astype(bf), b_xr=b_in[l, None, c_xr:c_gr],
        w_g1=gate1_cols(w_in[l]).astype(bf), b_g1=gate1_cols(b_in[l])[None, :],
        w_g2=gate2_cols(w_in[l]).astype(bf), b_g2=gate2_cols(b_in[l])[None, :],
        conv_w=conv_w[l], conv_b=conv_b[l][None, :],
        wg=[gate_w(0), gate_w(1)],
        lam=[lru_lam[l, 0][None, :], lru_lam[l, 1][None, :]],
        wa=w_attn_o[l].astype(bf), wr=w_rnn_o[l].astype(bf), wo=w_out[l].astype(bf),
        ln_g=ln_g[l][None, :], ln_b=ln_b[l][None, :],
    )


def _layer(x, x_bf, p, biases):
    attn = []
    for g, (_, dil) in enumerate(DILATED_GROUPS):
        if x_bf is None:
            qkv, x_bf = _qkv_proj(x, p["w_qkv"][g], p["b_qkv"][g], dil, g)
        else:
            qkv = _qkv_proj(x_bf, p["w_qkv"][g], p["b_qkv"][g], dil, g)
        attn.append(_attention_group(qkv, biases[g], g, dil))
    xr = _rest_proj(x_bf, p["w_xr"], p["b_xr"])
    z1, h_fwd, xconv = _rglru(xr, x_bf, p["w_g1"], p["b_g1"], p["conv_w"], p["conv_b"],
                              p["wg"][0], p["lam"][0], False)
    z2, u_r = _rglru(xconv, x_bf, p["w_g2"], p["b_g2"], None, None,
                     p["wg"][1], p["lam"][1], True, h_fwd, z1)
    return _merge_out(x, attn, z1, z2, u_r, p["wa"], p["wr"], p["wo"], p["ln_g"], p["ln_b"])


def kernel(x_prompt, x_sample, w_in, b_in, conv_w, conv_b, lru_w, lru_b, lru_lam,
           w_attn_o, w_rnn_o, w_out, ln_g, ln_b, rel_bias):
    layers = [_prep_layer(l, w_in, b_in, conv_w, conv_b, lru_w, lru_b, lru_lam,
                          w_attn_o, w_rnn_o, w_out, ln_g, ln_b) for l in range(DEPTH)]
    biases = [_band_bias(rel_bias, g, dil) for g, (_, dil) in enumerate(DILATED_GROUPS)]

    def trunk(x):
        x_bf = None
        for p in layers:
            x, x_bf = _layer(x, x_bf, p, biases)
        return x

    return (trunk(x_prompt), trunk(x_sample))
```

```python
import functools

import numpy as np
import jax
import jax.numpy as jnp
from jax import lax
from jax.experimental import pallas as pl
from jax.experimental.pallas import tpu as pltpu

D_MODEL = 2048
DEPTH = 2
HEAD_DIM = 128
HEADS_PER_GROUP = 4
DILATED_GROUPS = ((128, 1), (512, 4), (2048, 16))
N_GROUPS = len(DILATED_GROUPS)
D_ATTN = N_GROUPS * HEADS_PER_GROUP * HEAD_DIM
D_GROUP = HEADS_PER_GROUP * HEAD_DIM
D_RNN = D_MODEL
N_RNN_BLOCKS = 16
RNN_BLOCK = D_RNN // N_RNN_BLOCKS
CONV_WIDTH = 4
CONV_LEFT = 2
LRU_C = 8.0
REL_BUCKETS = 32
REL_MAX_DIST = 1024
ALPHA = (2.0 * DEPTH) ** 0.25
LN_EPS = 1e-5
NEG_INF = -1e30

LANES = 128
SUBLANES = 8
VMEM_LIMIT_BYTES = 56 * 1024 * 1024

HALF = 64
Q_SUB = 128
K_WIN = Q_SUB + 2 * HALF
ATTN_TOKENS = 2048
ATTN_RESIDUES_PER_TRIP = 4

RNN_TILE = 512
RNN_SLAB = 1024
RADIX = 4
GATE_BIAS_ROWS = 3
LOG2_E = 1.4426950408889634

MM_TOKENS = RNN_TILE
MM_PIECE = 256

PROJ_TM = 2048
MERGE_TM = 256

GM_SPLIT = D_MODEL // 2
GATE1_COLS = D_RNN + GM_SPLIT
GATE2_COLS = (D_MODEL - GM_SPLIT) + D_MODEL + D_GROUP


def _params(*sem):
    return pltpu.CompilerParams(dimension_semantics=sem, vmem_limit_bytes=VMEM_LIMIT_BYTES)


def _sigmoid(x):
    return 0.5 + 0.5 * jnp.tanh(0.5 * x)


def _rest_proj_kernel(x_ref, w_ref, b_ref, o_ref):
    acc = jnp.dot(x_ref[...], w_ref[...], preferred_element_type=jnp.float32)
    o_ref[...] = acc + b_ref[...]


def _rest_proj(x_bf, w_bf, b):
    bsz, s, d = x_bf.shape
    n = w_bf.shape[1]
    tm, tn = PROJ_TM, 2 * D_GROUP
    return pl.pallas_call(
        _rest_proj_kernel,
        grid=(bsz, s // tm, n // tn),
        in_specs=[
            pl.BlockSpec((None, tm, d), lambda bb, i, j: (bb, i, 0)),
            pl.BlockSpec((d, tn), lambda bb, i, j: (0, j)),
            pl.BlockSpec((1, tn), lambda bb, i, j: (0, j)),
        ],
        out_specs=pl.BlockSpec((None, tm, tn), lambda bb, i, j: (bb, i, j)),
        out_shape=jax.ShapeDtypeStruct((bsz, s, n), jnp.float32),
        compiler_params=_params("parallel", "parallel", "arbitrary"),
        name="xr_proj",
    )(x_bf, w_bf, b)


def _qkv_proj_kernel(x_ref, w_ref, b_ref, o_ref, *rest, dil, tm, tn, emit_bf16):
    if emit_bf16:
        xbf_ref, rest = rest[0], rest[1:]

        @pl.when(pl.program_id(2) == 0)
        def _():
            xbf_ref[...] = x_ref[...].astype(jnp.bfloat16)

        x = xbf_ref[...]
    else:
        x = x_ref[...]
    acc = jnp.dot(x, w_ref[...], preferred_element_type=jnp.float32) + b_ref[...]
    if dil == 1:
        o_ref[0] = acc.astype(o_ref.dtype)
        return
    acc_s = rest[0]
    rows = tm // dil
    for c in range(tn // LANES):
        cs = slice(c * LANES, (c + 1) * LANES)
        acc_s[c] = acc[:, cs]
        if dil == RADIX:
            for r in range(dil):
                o_ref[r, :, cs] = acc_s[c, pl.ds(r, rows, stride=dil), :].astype(o_ref.dtype)
        else:
            assert dil == RADIX * RADIX
            mid_s = rest[1]
            for lo in range(RADIX):
                mid_s[lo] = acc_s[c, pl.ds(lo, tm // RADIX, stride=RADIX), :]
            for lo in range(RADIX):
                for hi in range(RADIX):
                    o_ref[lo + RADIX * hi, :, cs] = mid_s[lo, pl.ds(hi, rows, stride=RADIX), :].astype(o_ref.dtype)


def _qkv_proj(x, w_bf, b, dil, g):
    bsz, s, d = x.shape
    n = w_bf.shape[1]
    emit_bf16 = x.dtype != jnp.bfloat16
    tm, tn = (PROJ_TM // 2 if emit_bf16 else PROJ_TM), n // 2
    scratch = [] if dil == 1 else [pltpu.VMEM((tn // LANES, tm, LANES), jnp.float32)]
    if dil > RADIX:
        scratch.append(pltpu.VMEM((RADIX, tm // RADIX, LANES), jnp.float32))
    out_specs = [pl.BlockSpec((None, dil, tm // dil, tn), lambda bb, i, j: (bb, 0, i, j))]
    out_shape = [jax.ShapeDtypeStruct((bsz, dil, s // dil, n), jnp.bfloat16)]
    if emit_bf16:
        out_specs.append(pl.BlockSpec((None, tm, d), lambda bb, i, j: (bb, i, 0)))
        out_shape.append(jax.ShapeDtypeStruct((bsz, s, d), jnp.bfloat16))
    outs = pl.pallas_call(
        functools.partial(_qkv_proj_kernel, dil=dil, tm=tm, tn=tn, emit_bf16=emit_bf16),
        grid=(bsz, s // tm, n // tn),
        in_specs=[
            pl.BlockSpec((None, tm, d), lambda bb, i, j: (bb, i, 0)),
            pl.BlockSpec((d, tn), lambda bb, i, j: (0, j)),
            pl.BlockSpec((1, tn), lambda bb, i, j: (0, j)),
        ],
        out_specs=out_specs,
        out_shape=out_shape,
        scratch_shapes=scratch,
        compiler_params=_params("parallel", "parallel", "arbitrary"),
        name=f"qkv_proj_g{g}",
    )(x, w_bf, b)
    return outs if emit_bf16 else outs[0]


def _t5_bucket(rel):
    nb = REL_BUCKETS // 2
    max_exact = nb // 2
    ret = (rel > 0).astype(np.int32) * nb
    n = np.abs(rel)
    large = max_exact + (np.log(np.maximum(n, max_exact) / max_exact)
                         / np.log(REL_MAX_DIST / max_exact) * (nb - max_exact)).astype(np.int32)
    large = np.minimum(large, nb - 1)
    return (ret + np.where(n < max_exact, n, large)).astype(np.int32)


def _band_bias(rel_bias, g, dil):
    tab = rel_bias[_t5_bucket(np.arange(-HALF, HALF + 1) * dil)][:, g * HEADS_PER_GROUP:(g + 1) * HEADS_PER_GROUP]
    tab = tab * (HEAD_DIM ** 0.5)
    period = Q_SUB + K_WIN + LANES
    vec = jnp.full((HEADS_PER_GROUP, period), NEG_INF, jnp.float32).at[:, :2 * HALF + 1].set(tab.T)
    mat = jnp.tile(vec, (1, Q_SUB))[:, :Q_SUB * (period - 1)].reshape(HEADS_PER_GROUP, Q_SUB, period - 1)
    return mat[:, :, :K_WIN]


def _attn_kernel(q_ref, kp_ref, kc_ref, kn_ref, vp_ref, vc_ref, vn_ref, bias_ref,
                 o_ref, lse_ref, kbuf, vbuf, *, dil, tile, length):
    i = pl.program_id(1)
    scale = HEAD_DIM ** -0.5

    def residue(r, slot):
        kbuf[slot, 0:HALF] = kp_ref[r]
        kbuf[slot, HALF:HALF + tile] = kc_ref[r]
        kbuf[slot, HALF + tile:] = kn_ref[r]
        vbuf[slot, 0:HALF] = vp_ref[r]
        vbuf[slot, HALF:HALF + tile] = vc_ref[r]
        vbuf[slot, HALF + tile:] = vn_ref[r]
        n_sub = tile // Q_SUB
        for sb in range(n_sub):
            r0 = sb * Q_SUB
            at_edge = sb == 0 or sb == n_sub - 1
            if at_edge:
                kpos = i * tile + (r0 - HALF) + lax.broadcasted_iota(jnp.int32, (Q_SUB, K_WIN), 1)
                in_seq = (kpos >= 0) & (kpos < length)
            if dil == 1:
                rows = slice(r0, r0 + Q_SUB)
            else:
                rows = pl.ds(r + r0 * dil, Q_SUB, stride=dil)
            for h in range(HEADS_PER_GROUP):
                cs = slice(h * HEAD_DIM, (h + 1) * HEAD_DIM)
                q = q_ref[r, r0:r0 + Q_SUB, cs]
                k = kbuf[slot, r0:r0 + K_WIN, cs]
                v = vbuf[slot, r0:r0 + K_WIN, cs]
                t = lax.dot_general(q, k, (((1,), (1,)), ((), ())),
                                    preferred_element_type=jnp.float32) + bias_ref[h]
                if at_edge:
                    t = jnp.where(in_seq, t, NEG_INF)
                mt = jnp.max(t, axis=-1, keepdims=True)
                p = jnp.exp2((t - mt) * (scale * LOG2_E))
                den = jnp.sum(p, axis=-1, keepdims=True)
                o = jnp.dot(p.astype(jnp.bfloat16), v, preferred_element_type=jnp.float32)
                o_ref[h, rows, :] = o / den
                lse_ref[h, rows, :] = jnp.broadcast_to(mt * scale + jnp.log(den), (Q_SUB, HEAD_DIM))

    per_trip = kbuf.shape[0]

    def trip(it, _):
        for slot in range(per_trip):
            residue(it * per_trip + slot, slot)
        return 0

    if dil == per_trip:
        trip(0, 0)
    else:
        lax.fori_loop(0, dil // per_trip, trip, 0)


def _attention_group(qkv, bias, g, dil):
    bsz, _, length, _ = qkv.shape
    s = length * dil
    tile = ATTN_TOKENS // dil
    per_trip = min(dil, ATTN_RESIDUES_PER_TRIP)
    hb = tile // HALF
    n_half_blocks = length // HALF

    def cur(part):
        return pl.BlockSpec((None, dil, tile, D_GROUP), lambda bb, i: (bb, 0, i, part))

    def prev(part):
        return pl.BlockSpec((None, dil, HALF, D_GROUP),
                            lambda bb, i: (bb, 0, jnp.maximum(i * hb - 1, 0), part))

    def nxt(part):
        return pl.BlockSpec((None, dil, HALF, D_GROUP),
                            lambda bb, i: (bb, 0, jnp.minimum((i + 1) * hb, n_half_blocks - 1), part))

    out_spec = pl.BlockSpec((None, HEADS_PER_GROUP, tile * dil, HEAD_DIM), lambda bb, i: (bb, 0, i, 0))
    out_sds = jax.ShapeDtypeStruct((bsz, HEADS_PER_GROUP, s, HEAD_DIM), jnp.float32)
    return pl.pallas_call(
        functools.partial(_attn_kernel, dil=dil, tile=tile, length=length),
        grid=(bsz, length // tile),
        in_specs=[cur(0), prev(1), cur(1), nxt(1), prev(2), cur(2), nxt(2),
                  pl.BlockSpec((HEADS_PER_GROUP, Q_SUB, K_WIN), lambda bb, i: (0, 0, 0))],
        out_specs=[out_spec, out_spec],
        out_shape=[out_sds, out_sds],
        scratch_shapes=[pltpu.VMEM((per_trip, tile + 2 * HALF, D_GROUP), jnp.bfloat16),
                        pltpu.VMEM((per_trip, tile + 2 * HALF, D_GROUP), jnp.bfloat16)],
        compiler_params=_params("parallel", "arbitrary"),
        name=f"attn_g{g}",
    )(qkv, qkv, qkv, qkv, qkv, qkv, qkv, bias)


def _vreg_scan(a, b, state, reverse):
    groups = a.shape[0] // SUBLANES
    a3 = a.reshape(groups, SUBLANES, LANES)
    b3 = b.reshape(groups, SUBLANES, LANES)
    sub = lax.broadcasted_iota(jnp.int32, (groups, SUBLANES, LANES), 1)
    for shift in (1, 2, 4):
        keep = (sub < SUBLANES - shift) if reverse else (sub >= shift)
        amount = SUBLANES - shift if reverse else shift
        a_sh = jnp.where(keep, pltpu.roll(a3, amount, axis=1), 1.0)
        b_sh = jnp.where(keep, pltpu.roll(b3, amount, axis=1), 0.0)
        b3 = a3 * b_sh + b3
        a3 = a3 * a_sh
    sub2 = lax.broadcasted_iota(jnp.int32, (SUBLANES, LANES), 0)
    entering = [None] * groups
    for gi in (range(groups - 1, -1, -1) if reverse else range(groups)):
        sb = jnp.broadcast_to(state, (SUBLANES, LANES))
        incl = b3[gi] + a3[gi] * sb
        if reverse:
            entering[gi] = jnp.where(sub2 == SUBLANES - 1, sb, pltpu.roll(incl, SUBLANES - 1, axis=0))
            state = incl[0:1]
        else:
            entering[gi] = jnp.where(sub2 == 0, sb, pltpu.roll(incl, 1, axis=0))
            state = incl[SUBLANES - 1:SUBLANES]
    return jnp.concatenate(entering, axis=0), state


def _radix_scan(av, bv, entering_fn, reverse):
    order = tuple(range(RADIX - 1, -1, -1)) if reverse else tuple(range(RADIX))
    h = [None] * RADIX
    p = [None] * RADIX
    h[order[0]], p[order[0]] = bv[order[0]], av[order[0]]
    for before, t in zip(order[:-1], order[1:]):
        h[t] = av[t] * h[before] + bv[t]
        p[t] = av[t] * p[before]
    entering, state = entering_fn(p[order[-1]], h[order[-1]])
    incl = [h[t] + p[t] * entering for t in range(RADIX)]
    excl = [None] * RADIX
    excl[order[0]] = entering
    for before, t in zip(order[:-1], order[1:]):
        excl[t] = incl[before]
    return incl, excl, state


def _rglru_kernel(*refs, tile, n_tiles, reverse):
    if reverse:
        (xm_ref, wm_ref, bm_ref, conv_ref, wg_ref, lam_ref, hf_ref, gr_ref,
         zm_ref, out_ref, sum_a, sum_b, ent_s, carry, ubuf) = refs
    else:
        (xm_ref, wm_ref, bm_ref, xp_ref, xc_ref, xn_ref, cw_ref, cb_ref, wg_ref, lam_ref,
         zm_ref, out_ref, conv_ref, ext, sum_a, sum_b, ent_s, carry) = refs

    def gate_proj_piece(k):
        cols = slice(k * MM_PIECE, (k + 1) * MM_PIECE)
        zm_ref[:, cols] = (jnp.dot(xm_ref[...], wm_ref[:, cols], preferred_element_type=jnp.float32)
                           + bm_ref[:, cols])

    n_pieces = zm_ref.shape[1] // MM_PIECE
    step = pl.program_id(2)
    ti = (n_tiles - 1 - step) if reverse else step
    quarter = tile // RADIX
    sixteenth = quarter // RADIX
    nblk = RNN_SLAB // RNN_BLOCK

    @pl.when(step == 0)
    def _():
        carry[...] = jnp.zeros_like(carry)

    lam = lam_ref[...]
    half_csp = (0.5 * LRU_C) * (jnp.maximum(-lam, 0.0) + jnp.log1p(jnp.exp(-jnp.abs(lam))))
    bias_taps = (lax.broadcasted_iota(jnp.int32, (tile, RNN_BLOCK), 1) < GATE_BIAS_ROWS).astype(jnp.bfloat16)

    assert n_pieces <= nblk
    for n in range(nblk):
        cs = slice(n * RNN_BLOCK, (n + 1) * RNN_BLOCK)
        if reverse:
            xc = conv_ref[:, cs]
        else:
            ext[n, 0:SUBLANES] = jnp.where(ti > 0, xp_ref[:, cs], 0.0)
            ext[n, SUBLANES:SUBLANES + tile] = xc_ref[:, cs]
            ext[n, SUBLANES + tile:] = jnp.where(ti < n_tiles - 1, xn_ref[:, cs], 0.0)
            taps = {u: ext[n, pl.ds(SUBLANES + u, quarter, stride=RADIX), :]
                    for u in range(-CONV_LEFT, RADIX + CONV_WIDTH - 1 - CONV_LEFT)}
            xc_t = []
            for t in range(RADIX):
                acc = cb_ref[:, cs]
                for j in range(CONV_WIDTH):
                    acc = acc + taps[t + j - CONV_LEFT] * cw_ref[j:j + 1, cs]
                xc_t.append(acc)
            xc = jnp.concatenate(xc_t, axis=0)
            conv_ref[:, cs] = xc
        lhs = jnp.concatenate([xc.astype(jnp.bfloat16), bias_taps], axis=1)
        z = jnp.dot(lhs, wg_ref[n], preferred_element_type=jnp.float32)
        if n < n_pieces:
            gate_proj_piece(n)
        th_r = jnp.tanh(z[:, :RNN_BLOCK])
        th_i = jnp.tanh(z[:, RNN_BLOCK:])
        hsp = half_csp[:, cs]
        neg_log_a = hsp + hsp * th_r
        a = jnp.exp2(neg_log_a * (-LOG2_E))
        half_xc = 0.5 * xc
        one_m_a2 = jnp.tanh(neg_log_a) * (1.0 + a * a)
        root = jnp.where(one_m_a2 > 0.0, one_m_a2 * lax.rsqrt(one_m_a2), 0.0)
        b = root * (half_xc + half_xc * th_i)
        av = [a[t * quarter:(t + 1) * quarter] for t in range(RADIX)]
        bv = [b[t * quarter:(t + 1) * quarter] for t in range(RADIX)]

        def level2(p, h, n=n, cs=cs):
            sum_a[...] = p
            sum_b[...] = h
            av2 = [sum_a[pl.ds(t, sixteenth, stride=RADIX), :] for t in range(RADIX)]
            bv2 = [sum_b[pl.ds(t, sixteenth, stride=RADIX), :] for t in range(RADIX)]
            _, excl2, state = _radix_scan(
                av2, bv2, lambda p3, h3: _vreg_scan(p3, h3, carry[:, cs], reverse), reverse)
            for t in range(RADIX):
                ent_s[pl.ds(t, sixteenth, stride=RADIX), :] = excl2[t]
            carry[:, cs] = state
            return ent_s[...], state

        incl, _, _ = _radix_scan(av, bv, level2, reverse)
        for t in range(RADIX):
            rows = slice(t * quarter, (t + 1) * quarter)
            if reverse:
                ubuf[n, pl.ds(t, quarter, stride=RADIX), :] = incl[t] + hf_ref[rows, cs]
            else:
                out_ref[rows, cs] = incl[t]

    if reverse:
        for n in range(nblk):
            cs = slice(n * RNN_BLOCK, (n + 1) * RNN_BLOCK)
            half_gr = 0.5 * gr_ref[:, cs]
            silu = half_gr + half_gr * jnp.tanh(half_gr)
            out_ref[:, cs] = (ubuf[n] * silu).astype(out_ref.dtype)


def _rglru(xr, x_bf, w_m, b_m, conv_w, conv_b, wg, lam, reverse, h_fwd=None, gr_src=None):
    b, s, _ = xr.shape
    n_m = w_m.shape[1]
    tn_m = n_m // 2
    tile = RNN_TILE
    n_tiles = s // tile
    nslab = D_RNN // RNN_SLAB
    sub_per_tile = tile // SUBLANES
    n_sub = s // SUBLANES
    nblk = RNN_SLAB // RNN_BLOCK
    quarter = tile // RADIX

    def tidx(i):
        return (n_tiles - 1 - i) if reverse else i

    x_cur = pl.BlockSpec((None, tile, RNN_SLAB), lambda bb, c, i: (bb, tidx(i), c))
    x_prev = pl.BlockSpec((None, SUBLANES, RNN_SLAB),
                          lambda bb, c, i: (bb, jnp.maximum(tidx(i) * sub_per_tile - 1, 0), c))
    x_next = pl.BlockSpec((None, SUBLANES, RNN_SLAB),
                          lambda bb, c, i: (bb, jnp.minimum((tidx(i) + 1) * sub_per_tile, n_sub - 1), c))
    m_tiles = b * s // MM_TOKENS
    assert m_tiles * 2 == b * nslab * n_tiles

    def m_col(bb, c, i):
        return ((bb * nslab + c) * n_tiles + i) // m_tiles

    def m_row(bb, c, i):
        return ((bb * nslab + c) * n_tiles + i) % m_tiles

    time_tile = pl.BlockSpec((None, tile, RNN_SLAB), lambda bb, c, i: (bb, tidx(i), c))
    gate_w = pl.BlockSpec((nblk, 2 * RNN_BLOCK, 2 * RNN_BLOCK), lambda bb, c, i: (c, 0, 0))
    lam_spec = pl.BlockSpec((1, RNN_SLAB), lambda bb, c, i: (0, c))
    in_specs = [
        pl.BlockSpec((MM_TOKENS, D_MODEL), lambda bb, c, i: (m_row(bb, c, i), 0)),
        pl.BlockSpec((D_MODEL, tn_m), lambda bb, c, i: (0, m_col(bb, c, i))),
        pl.BlockSpec((1, tn_m), lambda bb, c, i: (0, m_col(bb, c, i))),
    ]
    args = [x_bf.reshape(b * s, D_MODEL), w_m, b_m]
    scan_scratch = [
        pltpu.VMEM((quarter, RNN_BLOCK), jnp.float32),
        pltpu.VMEM((quarter, RNN_BLOCK), jnp.float32),
        pltpu.VMEM((quarter, RNN_BLOCK), jnp.float32),
        pltpu.VMEM((1, RNN_SLAB), jnp.float32),
    ]
    out_specs = [pl.BlockSpec((MM_TOKENS, tn_m), lambda bb, c, i: (m_row(bb, c, i), m_col(bb, c, i))), time_tile]
    state_sds = jax.ShapeDtypeStruct((b, s, D_RNN), jnp.float32)
    if reverse:
        in_specs += [time_tile, gate_w, lam_spec, time_tile, time_tile]
        args += [xr, wg, lam, h_fwd, gr_src]
        scratch = scan_scratch + [pltpu.VMEM((nblk, tile, RNN_BLOCK), jnp.float32)]
        out_shape = [jax.ShapeDtypeStruct((b * s, n_m), jnp.float32),
                     jax.ShapeDtypeStruct((b, s, D_RNN), jnp.bfloat16)]
    else:
        in_specs += [x_prev, x_cur, x_next,
                     pl.BlockSpec((CONV_WIDTH, RNN_SLAB), lambda bb, c, i: (0, c)),
                     pl.BlockSpec((1, RNN_SLAB), lambda bb, c, i: (0, c)),
                     gate_w, lam_spec]
        args += [xr, xr, xr, conv_w, conv_b, wg, lam]
        scratch = [pltpu.VMEM((nblk, tile + 2 * SUBLANES, RNN_BLOCK), jnp.float32)] + scan_scratch
        out_specs.append(time_tile)
        out_shape = [jax.ShapeDtypeStruct((b * s, n_m), jnp.float32), state_sds, state_sds]
    outs = pl.pallas_call(
        functools.partial(_rglru_kernel, tile=tile, n_tiles=n_tiles, reverse=reverse),
        grid=(b, nslab, n_tiles),
        in_specs=in_specs,
        out_specs=out_specs,
        out_shape=out_shape,
        scratch_shapes=scratch,
        compiler_params=_params("arbitrary", "arbitrary", "arbitrary"),
        name="rglru_bwd" if reverse else "rglru_fwd",
    )(*args)
    return (outs[0].reshape(b, s, n_m),) + tuple(outs[1:])


def _merge_kernel(x_ref, o0_ref, o1_ref, o2_ref, l0_ref, l1_ref, l2_ref, ga_ref, ur_ref,
                  gma_lo_ref, gma_hi_ref, gmr_lo_ref, gmr_hi_ref,
                  wa_ref, wr_ref, wo_ref, lng_ref, lnb_ref, y_ref, ybf_ref):
    def heads(ref):
        return jnp.concatenate([ref[h] for h in range(HEADS_PER_GROUP)], axis=1)

    gma = jnp.concatenate([gma_lo_ref[...], gma_hi_ref[...]], axis=1)
    gmr = jnp.concatenate([gmr_lo_ref[...], gmr_hi_ref[...]], axis=1)

    l0, l1, l2 = heads(l0_ref), heads(l1_ref), heads(l2_ref)
    mx = jnp.maximum(jnp.maximum(l0, l1), l2)
    e0, e1, e2 = jnp.exp(l0 - mx), jnp.exp(l1 - mx), jnp.exp(l2 - mx)
    inv = 1.0 / (e0 + e1 + e2)
    oa = (e0 * inv) * heads(o0_ref) + (e1 * inv) * heads(o1_ref) + (e2 * inv) * heads(o2_ref)
    ga = ga_ref[...]
    ua = (oa * (ga * _sigmoid(ga))).astype(jnp.bfloat16)
    ya = jnp.dot(ua, wa_ref[...], preferred_element_type=jnp.float32)
    yr = jnp.dot(ur_ref[...], wr_ref[...], preferred_element_type=jnp.float32)
    mixed = _sigmoid(gma) * ya + _sigmoid(gmr) * yr
    out = jnp.dot(mixed.astype(jnp.bfloat16), wo_ref[...], preferred_element_type=jnp.float32)
    v = ALPHA * x_ref[...] + out
    mu = jnp.mean(v, axis=-1, keepdims=True)
    vc = v - mu
    var = jnp.mean(vc * vc, axis=-1, keepdims=True)
    y = vc * lax.rsqrt(var + LN_EPS) * lng_ref[...] + lnb_ref[...]
    y_ref[...] = y
    ybf_ref[...] = y.astype(jnp.bfloat16)


def _merge_out(x, attn, z1, z2, u_r, wa, wr, wo, ln_g, ln_b):
    bsz, s, d = x.shape
    tm = MERGE_TM
    row = lambda w: pl.BlockSpec((None, tm, w), lambda bb, i: (bb, i, 0))
    head = pl.BlockSpec((None, HEADS_PER_GROUP, tm, HEAD_DIM), lambda bb, i: (bb, 0, i, 0))
    const = lambda shp: pl.BlockSpec(shp, lambda bb, i: (0, 0), pipeline_mode=pl.Buffered(1))
    half = lambda j: pl.BlockSpec((None, tm, GM_SPLIT), lambda bb, i: (bb, i, j))
    in_specs = [
        row(d),
        head, head, head, head, head, head,
        pl.BlockSpec((None, tm, D_GROUP), lambda bb, i: (bb, i, (GATE2_COLS - D_GROUP) // D_GROUP)),
        row(D_RNN),
        half(D_RNN // GM_SPLIT),
        half(0),
        half(1), half(2),
        const((D_GROUP, D_MODEL)), const((D_RNN, D_MODEL)), const((D_MODEL, D_MODEL)),
        const((1, D_MODEL)), const((1, D_MODEL)),
    ]
    (o0, l0), (o1, l1), (o2, l2) = attn
    return pl.pallas_call(
        _merge_kernel,
        grid=(bsz, s // tm),
        in_specs=in_specs,
        out_specs=[row(d), row(d)],
        out_shape=[jax.ShapeDtypeStruct((bsz, s, d), jnp.float32), jax.ShapeDtypeStruct((bsz, s, d), jnp.bfloat16)],
        compiler_params=_params("parallel", "parallel"),
        name="merge_out",
    )(x, o0, o1, o2, l0, l1, l2, z2, u_r, z1, z2, z2, z2, wa, wr, wo, ln_g, ln_b)


def _prep_layer(l, w_in, b_in, conv_w, conv_b, lru_w, lru_b, lru_lam, w_attn_o, w_rnn_o, w_out, ln_g, ln_b):
    bf = jnp.bfloat16
    a3 = 3 * D_ATTN
    c_ga, c_xr, c_gr, c_gm = a3, a3 + D_GROUP, a3 + D_GROUP + D_RNN, a3 + D_GROUP + 2 * D_RNN

    def gate1_cols(w):
        return w[..., c_gr:c_gm + GM_SPLIT]

    def gate2_cols(w):
        return jnp.concatenate([w[..., c_gm + GM_SPLIT:], w[..., c_ga:c_xr]], axis=-1)

    def group_cols(w, g):
        return jnp.concatenate([w[..., part * D_ATTN + g * D_GROUP:part * D_ATTN + (g + 1) * D_GROUP]
                                for part in range(3)], axis=-1)

    def gate_w(direction):
        w = 0.5 * jnp.concatenate([lru_w[l, direction, 0], lru_w[l, direction, 1]], axis=-1)
        bias = 0.5 * jnp.concatenate([lru_b[l, direction, 0], lru_b[l, direction, 1]], axis=-1)
        terms, rem = [], bias
        for _ in range(GATE_BIAS_ROWS):
            term = rem.astype(bf)
            terms.append(term)
            rem = rem - term.astype(jnp.float32)
        pad = jnp.zeros((N_RNN_BLOCKS, RNN_BLOCK - GATE_BIAS_ROWS, 2 * RNN_BLOCK), bf)
        return jnp.concatenate([w.astype(bf), jnp.stack(terms, axis=1), pad], axis=1)

    return dict(
        w_qkv=[group_cols(w_in[l], g).astype(bf) for g in range(N_GROUPS)],
        b_qkv=[group_cols(b_in[l], g)[None, :] for g in range(N_GROUPS)],
        w_xr=w_in[l, :, c_xr:c_gr].astype(bf), b_xr=b_in[l, None, c_xr:c_gr],
        w_g1=gate1_cols(w_in[l]).astype(bf), b_g1=gate1_cols(b_in[l])[None, :],
        w_g2=gate2_cols(w_in[l]).astype(bf), b_g2=gate2_cols(b_in[l])[None, :],
        conv_w=conv_w[l], conv_b=conv_b[l][None, :],
        wg=[gate_w(0), gate_w(1)],
        lam=[lru_lam[l, 0][None, :], lru_lam[l, 1][None, :]],
        wa=w_attn_o[l].astype(bf), wr=w_rnn_o[l].astype(bf), wo=w_out[l].astype(bf),
        ln_g=ln_g[l][None, :], ln_b=ln_b[l][None, :],
    )


def _layer(x, x_bf, p, biases):
    attn = []
    for g, (_, dil) in enumerate(DILATED_GROUPS):
        if x_bf is None:
            qkv, x_bf = _qkv_proj(x, p["w_qkv"][g], p["b_qkv"][g], dil, g)
        else:
            qkv = _qkv_proj(x_bf, p["w_qkv"][g], p["b_qkv"][g], dil, g)
        attn.append(_attention_group(qkv, biases[g], g, dil))
    xr = _rest_proj(x_bf, p["w_xr"], p["b_xr"])
    z1, h_fwd, xconv = _rglru(xr, x_bf, p["w_g1"], p["b_g1"], p["conv_w"], p["conv_b"],
                              p["wg"][0], p["lam"][0], False)
    z2, u_r = _rglru(xconv, x_bf, p["w_g2"], p["b_g2"], None, None,
                     p["wg"][1], p["lam"][1], True, h_fwd, z1)
    return _merge_out(x, attn, z1, z2, u_r, p["wa"], p["wr"], p["wo"], p["ln_g"], p["ln_b"])


def kernel(x_prompt, x_sample, w_in, b_in, conv_w, conv_b, lru_w, lru_b, lru_lam,
           w_attn_o, w_rnn_o, w_out, ln_g, ln_b, rel_bias):
    layers = [_prep_layer(l, w_in, b_in, conv_w, conv_b, lru_w, lru_b, lru_lam,
                          w_attn_o, w_rnn_o, w_out, ln_g, ln_b) for l in range(DEPTH)]
    biases = [_band_bias(rel_bias, g, dil) for g, (_, dil) in enumerate(DILATED_GROUPS)]

    def trunk(x):
        x_bf = None
        for p in layers:
            x, x_bf = _layer(x, x_bf, p, biases)
        return x

    return (trunk(x_prompt), trunk(x_sample))
```

```python
import functools

import numpy as np
import jax
import jax.numpy as jnp
from jax import lax
from jax.experimental import pallas as pl
from jax.experimental.pallas import tpu as pltpu

D_MODEL = 2048
DEPTH = 2
HEAD_DIM = 128
HEADS_PER_GROUP = 4
DILATED_GROUPS = ((128, 1), (512, 4), (2048, 16))
N_GROUPS = len(DILATED_GROUPS)
D_ATTN = N_GROUPS * HEADS_PER_GROUP * HEAD_DIM
D_GROUP = HEADS_PER_GROUP * HEAD_DIM
D_RNN = D_MODEL
N_RNN_BLOCKS = 16
RNN_BLOCK = D_RNN // N_RNN_BLOCKS
CONV_WIDTH = 4
CONV_LEFT = 2
LRU_C = 8.0
REL_BUCKETS = 32
REL_MAX_DIST = 1024
ALPHA = (2.0 * DEPTH) ** 0.25
LN_EPS = 1e-5
NEG_INF = -1e30

LANES = 128
SUBLANES = 8
VMEM_LIMIT_BYTES = 56 * 1024 * 1024

HALF = 64
Q_SUB = 128
K_WIN = Q_SUB + 2 * HALF
ATTN_TOKENS = 2048
ATTN_RESIDUES_PER_TRIP = 16

RNN_TILE = 512
RNN_SLAB = 1024
RADIX = 4
GATE_BIAS_ROWS = 3
LOG2_E = 1.4426950408889634

MM_TOKENS = RNN_TILE
MM_PIECE = 256

PROJ_TM = 2048
MERGE_TM = 256

GM_SPLIT = D_MODEL // 2
GATE1_COLS = D_RNN + GM_SPLIT
GATE2_COLS = (D_MODEL - GM_SPLIT) + D_MODEL + D_GROUP


def _params(*sem):
    return pltpu.CompilerParams(dimension_semantics=sem, vmem_limit_bytes=VMEM_LIMIT_BYTES)


def _sigmoid(x):
    return 0.5 + 0.5 * jnp.tanh(0.5 * x)


def _rest_proj_kernel(x_ref, w_ref, b_ref, o_ref):
    acc = jnp.dot(x_ref[...], w_ref[...], preferred_element_type=jnp.float32)
    o_ref[...] = acc + b_ref[...]


def _rest_proj(x_bf, w_bf, b):
    bsz, s, d = x_bf.shape
    n = w_bf.shape[1]
    tm, tn = PROJ_TM, 2 * D_GROUP
    return pl.pallas_call(
        _rest_proj_kernel,
        grid=(bsz, s // tm, n // tn),
        in_specs=[
            pl.BlockSpec((None, tm, d), lambda bb, i, j: (bb, i, 0)),
            pl.BlockSpec((d, tn), lambda bb, i, j: (0, j)),
            pl.BlockSpec((1, tn), lambda bb, i, j: (0, j)),
        ],
        out_specs=pl.BlockSpec((None, tm, tn), lambda bb, i, j: (bb, i, j)),
        out_shape=jax.ShapeDtypeStruct((bsz, s, n), jnp.float32),
        compiler_params=_params("parallel", "parallel", "arbitrary"),
        name="xr_proj",
    )(x_bf, w_bf, b)


def _qkv_proj_kernel(x_ref, w_ref, b_ref, o_ref, *rest, dil, tm, tn, emit_bf16):
    if emit_bf16:
        xbf_ref, rest = rest[0], rest[1:]

        @pl.when(pl.program_id(2) == 0)
        def _():
            xbf_ref[...] = x_ref[...].astype(jnp.bfloat16)

        x = xbf_ref[...]
    else:
        x = x_ref[...]
    acc = jnp.dot(x, w_ref[...], preferred_element_type=jnp.float32) + b_ref[...]
    if dil == 1:
        o_ref[0] = acc.astype(o_ref.dtype)
        return
    acc_s = rest[0]
    rows = tm // dil
    for c in range(tn // LANES):
        cs = slice(c * LANES, (c + 1) * LANES)
        acc_s[c] = acc[:, cs]
        if dil == RADIX:
            for r in range(dil):
                o_ref[r, :, cs] = acc_s[c, pl.ds(r, rows, stride=dil), :].astype(o_ref.dtype)
        else:
            assert dil == RADIX * RADIX
            mid_s = rest[1]
            for lo in range(RADIX):
                mid_s[lo] = acc_s[c, pl.ds(lo, tm // RADIX, stride=RADIX), :]
            for lo in range(RADIX):
                for hi in range(RADIX):
                    o_ref[lo + RADIX * hi, :, cs] = mid_s[lo, pl.ds(hi, rows, stride=RADIX), :].astype(o_ref.dtype)


def _qkv_proj(x, w_bf, b, dil, g):
    bsz, s, d = x.shape
    n = w_bf.shape[1]
    emit_bf16 = x.dtype != jnp.bfloat16
    tm, tn = (PROJ_TM // 2, n) if emit_bf16 else (PROJ_TM, n // 2)
    scratch = [] if dil == 1 else [pltpu.VMEM((tn // LANES, tm, LANES), jnp.float32)]
    if dil > RADIX:
        scratch.append(pltpu.VMEM((RADIX, tm // RADIX, LANES), jnp.float32))
    out_specs = [pl.BlockSpec((None, dil, tm // dil, tn), lambda bb, i, j: (bb, 0, i, j))]
    out_shape = [jax.ShapeDtypeStruct((bsz, dil, s // dil, n), jnp.bfloat16)]
    if emit_bf16:
        out_specs.append(pl.BlockSpec((None, tm, d), lambda bb, i, j: (bb, i, 0)))
        out_shape.append(jax.ShapeDtypeStruct((bsz, s, d), jnp.bfloat16))
    outs = pl.pallas_call(
        functools.partial(_qkv_proj_kernel, dil=dil, tm=tm, tn=tn, emit_bf16=emit_bf16),
        grid=(bsz, s // tm, n // tn),
        in_specs=[
            pl.BlockSpec((None, tm, d), lambda bb, i, j: (bb, i, 0)),
            pl.BlockSpec((d, tn), lambda bb, i, j: (0, j)),
            pl.BlockSpec((1, tn), lambda bb, i, j: (0, j)),
        ],
        out_specs=out_specs,
        out_shape=out_shape,
        scratch_shapes=scratch,
        compiler_params=_params("parallel", "parallel", "arbitrary"),
        name=f"qkv_proj_g{g}",
    )(x, w_bf, b)
    return outs if emit_bf16 else outs[0]


def _t5_bucket(rel):
    nb = REL_BUCKETS // 2
    max_exact = nb // 2
    ret = (rel > 0).astype(np.int32) * nb
    n = np.abs(rel)
    large = max_exact + (np.log(np.maximum(n, max_exact) / max_exact)
                         / np.log(REL_MAX_DIST / max_exact) * (nb - max_exact)).astype(np.int32)
    large = np.minimum(large, nb - 1)
    return (ret + np.where(n < max_exact, n, large)).astype(np.int32)


def _band_bias(rel_bias, g, dil):
    tab = rel_bias[_t5_bucket(np.arange(-HALF, HALF + 1) * dil)][:, g * HEADS_PER_GROUP:(g + 1) * HEADS_PER_GROUP]
    tab = tab * (HEAD_DIM ** 0.5)
    period = Q_SUB + K_WIN + LANES
    vec = jnp.full((HEADS_PER_GROUP, period), NEG_INF, jnp.float32).at[:, :2 * HALF + 1].set(tab.T)
    mat = jnp.tile(vec, (1, Q_SUB))[:, :Q_SUB * (period - 1)].reshape(HEADS_PER_GROUP, Q_SUB, period - 1)
    return mat[:, :, :K_WIN]


def _attn_kernel(q_ref, kp_ref, kc_ref, kn_ref, vp_ref, vc_ref, vn_ref, bias_ref,
                 o_ref, lse_ref, kbuf, vbuf, *, dil, tile, length):
    i = pl.program_id(1)
    scale = HEAD_DIM ** -0.5

    def residue(r, slot):
        kbuf[slot, 0:HALF] = kp_ref[r]
        kbuf[slot, HALF:HALF + tile] = kc_ref[r]
        kbuf[slot, HALF + tile:] = kn_ref[r]
        vbuf[slot, 0:HALF] = vp_ref[r]
        vbuf[slot, HALF:HALF + tile] = vc_ref[r]
        vbuf[slot, HALF + tile:] = vn_ref[r]
        n_sub = tile // Q_SUB
        for sb in range(n_sub):
            r0 = sb * Q_SUB
            at_edge = sb == 0 or sb == n_sub - 1
            if at_edge:
                kpos = i * tile + (r0 - HALF) + lax.broadcasted_iota(jnp.int32, (Q_SUB, K_WIN), 1)
                in_seq = (kpos >= 0) & (kpos < length)
            if dil == 1:
                rows = slice(r0, r0 + Q_SUB)
            else:
                rows = pl.ds(r + r0 * dil, Q_SUB, stride=dil)
            for h in range(HEADS_PER_GROUP):
                cs = slice(h * HEAD_DIM, (h + 1) * HEAD_DIM)
                q = q_ref[r, r0:r0 + Q_SUB, cs]
                k = kbuf[slot, r0:r0 + K_WIN, cs]
                v = vbuf[slot, r0:r0 + K_WIN, cs]
                t = lax.dot_general(q, k, (((1,), (1,)), ((), ())),
                                    preferred_element_type=jnp.float32) + bias_ref[h]
                if at_edge:
                    t = jnp.where(in_seq, t, NEG_INF)
                mt = jnp.max(t, axis=-1, keepdims=True)
                p = jnp.exp2((t - mt) * (scale * LOG2_E))
                den = jnp.sum(p, axis=-1, keepdims=True)
                o = jnp.dot(p.astype(jnp.bfloat16), v, preferred_element_type=jnp.float32)
                o_ref[h, rows, :] = o / den
                lse_ref[h, rows, :] = jnp.broadcast_to(mt * scale + jnp.log(den), (Q_SUB, HEAD_DIM))

    per_trip = kbuf.shape[0]

    def trip(it, _):
        for slot in range(per_trip):
            residue(it * per_trip + slot, slot)
        return 0

    if dil == per_trip:
        trip(0, 0)
    else:
        lax.fori_loop(0, dil // per_trip, trip, 0)


def _attention_group(qkv, bias, g, dil):
    bsz, _, length, _ = qkv.shape
    s = length * dil
    tile = ATTN_TOKENS // dil
    per_trip = min(dil, ATTN_RESIDUES_PER_TRIP)
    hb = tile // HALF
    n_half_blocks = length // HALF

    def cur(part):
        return pl.BlockSpec((None, dil, tile, D_GROUP), lambda bb, i: (bb, 0, i, part))

    def prev(part):
        return pl.BlockSpec((None, dil, HALF, D_GROUP),
                            lambda bb, i: (bb, 0, jnp.maximum(i * hb - 1, 0), part))

    def nxt(part):
        return pl.BlockSpec((None, dil, HALF, D_GROUP),
                            lambda bb, i: (bb, 0, jnp.minimum((i + 1) * hb, n_half_blocks - 1), part))

    out_spec = pl.BlockSpec((None, HEADS_PER_GROUP, tile * dil, HEAD_DIM), lambda bb, i: (bb, 0, i, 0))
    out_sds = jax.ShapeDtypeStruct((bsz, HEADS_PER_GROUP, s, HEAD_DIM), jnp.float32)
    return pl.pallas_call(
        functools.partial(_attn_kernel, dil=dil, tile=tile, length=length),
        grid=(bsz, length // tile),
        in_specs=[cur(0), prev(1), cur(1), nxt(1), prev(2), cur(2), nxt(2),
                  pl.BlockSpec((HEADS_PER_GROUP, Q_SUB, K_WIN), lambda bb, i: (0, 0, 0))],
        out_specs=[out_spec, out_spec],
        out_shape=[out_sds, out_sds],
        scratch_shapes=[pltpu.VMEM((per_trip, tile + 2 * HALF, D_GROUP), jnp.bfloat16),
                        pltpu.VMEM((per_trip, tile + 2 * HALF, D_GROUP), jnp.bfloat16)],
        compiler_params=_params("parallel", "arbitrary"),
        name=f"attn_g{g}",
    )(qkv, qkv, qkv, qkv, qkv, qkv, qkv, bias)


def _vreg_scan(a, b, state, reverse):
    groups = a.shape[0] // SUBLANES
    a3 = a.reshape(groups, SUBLANES, LANES)
    b3 = b.reshape(groups, SUBLANES, LANES)
    sub = lax.broadcasted_iota(jnp.int32, (groups, SUBLANES, LANES), 1)
    for shift in (1, 2, 4):
        keep = (sub < SUBLANES - shift) if reverse else (sub >= shift)
        amount = SUBLANES - shift if reverse else shift
        a_sh = jnp.where(keep, pltpu.roll(a3, amount, axis=1), 1.0)
        b_sh = jnp.where(keep, pltpu.roll(b3, amount, axis=1), 0.0)
        b3 = a3 * b_sh + b3
        a3 = a3 * a_sh
    sub2 = lax.broadcasted_iota(jnp.int32, (SUBLANES, LANES), 0)
    entering = [None] * groups
    for gi in (range(groups - 1, -1, -1) if reverse else range(groups)):
        sb = jnp.broadcast_to(state, (SUBLANES, LANES))
        incl = b3[gi] + a3[gi] * sb
        if reverse:
            entering[gi] = jnp.where(sub2 == SUBLANES - 1, sb, pltpu.roll(incl, SUBLANES - 1, axis=0))
            state = incl[0:1]
        else:
            entering[gi] = jnp.where(sub2 == 0, sb, pltpu.roll(incl, 1, axis=0))
            state = incl[SUBLANES - 1:SUBLANES]
    return jnp.concatenate(entering, axis=0), state


def _radix_scan(av, bv, entering_fn, reverse):
    order = tuple(range(RADIX - 1, -1, -1)) if reverse else tuple(range(RADIX))
    h = [None] * RADIX
    p = [None] * RADIX
    h[order[0]], p[order[0]] = bv[order[0]], av[order[0]]
    for before, t in zip(order[:-1], order[1:]):
        h[t] = av[t] * h[before] + bv[t]
        p[t] = av[t] * p[before]
    entering, state = entering_fn(p[order[-1]], h[order[-1]])
    incl = [h[t] + p[t] * entering for t in range(RADIX)]
    excl = [None] * RADIX
    excl[order[0]] = entering
    for before, t in zip(order[:-1], order[1:]):
        excl[t] = incl[before]
    return incl, excl, state


def _rglru_kernel(*refs, tile, n_tiles, reverse):
    if reverse:
        (xm_ref, wm_ref, bm_ref, conv_ref, wg_ref, lam_ref, hf_ref, gr_ref,
         zm_ref, out_ref, sum_a, sum_b, ent_s, carry, ubuf) = refs
    else:
        (xm_ref, wm_ref, bm_ref, xp_ref, xc_ref, xn_ref, cw_ref, cb_ref, wg_ref, lam_ref,
         zm_ref, out_ref, conv_ref, ext, sum_a, sum_b, ent_s, carry) = refs

    def gate_proj_piece(k):
        cols = slice(k * MM_PIECE, (k + 1) * MM_PIECE)
        zm_ref[:, cols] = (jnp.dot(xm_ref[...], wm_ref[:, cols], preferred_element_type=jnp.float32)
                           + bm_ref[:, cols])

    n_pieces = zm_ref.shape[1] // MM_PIECE
    step = pl.program_id(2)
    ti = (n_tiles - 1 - step) if reverse else step
    quarter = tile // RADIX
    sixteenth = quarter // RADIX
    nblk = RNN_SLAB // RNN_BLOCK

    @pl.when(step == 0)
    def _():
        carry[...] = jnp.zeros_like(carry)

    lam = lam_ref[...]
    half_csp = (0.5 * LRU_C) * (jnp.maximum(-lam, 0.0) + jnp.log1p(jnp.exp(-jnp.abs(lam))))
    bias_taps = (lax.broadcasted_iota(jnp.int32, (tile, RNN_BLOCK), 1) < GATE_BIAS_ROWS).astype(jnp.bfloat16)

    assert n_pieces <= nblk
    for n in range(nblk):
        cs = slice(n * RNN_BLOCK, (n + 1) * RNN_BLOCK)
        if reverse:
            xc = conv_ref[:, cs]
        else:
            ext[n, 0:SUBLANES] = jnp.where(ti > 0, xp_ref[:, cs], 0.0)
            ext[n, SUBLANES:SUBLANES + tile] = xc_ref[:, cs]
            ext[n, SUBLANES + tile:] = jnp.where(ti < n_tiles - 1, xn_ref[:, cs], 0.0)
            taps = {u: ext[n, pl.ds(SUBLANES + u, quarter, stride=RADIX), :]
                    for u in range(-CONV_LEFT, RADIX + CONV_WIDTH - 1 - CONV_LEFT)}
            xc_t = []
            for t in range(RADIX):
                acc = cb_ref[:, cs]
                for j in range(CONV_WIDTH):
                    acc = acc + taps[t + j - CONV_LEFT] * cw_ref[j:j + 1, cs]
                xc_t.append(acc)
            xc = jnp.concatenate(xc_t, axis=0)
            conv_ref[:, cs] = xc
        lhs = jnp.concatenate([xc.astype(jnp.bfloat16), bias_taps], axis=1)
        z = jnp.dot(lhs, wg_ref[n], preferred_element_type=jnp.float32)
        if n < n_pieces:
            gate_proj_piece(n)
        th_r = jnp.tanh(z[:, :RNN_BLOCK])
        th_i = jnp.tanh(z[:, RNN_BLOCK:])
        hsp = half_csp[:, cs]
        neg_log_a = hsp + hsp * th_r
        a = jnp.exp2(neg_log_a * (-LOG2_E))
        half_xc = 0.5 * xc
        one_m_a2 = jnp.tanh(neg_log_a) * (1.0 + a * a)
        root = jnp.where(one_m_a2 > 0.0, one_m_a2 * lax.rsqrt(one_m_a2), 0.0)
        b = root * (half_xc + half_xc * th_i)
        av = [a[t * quarter:(t + 1) * quarter] for t in range(RADIX)]
        bv = [b[t * quarter:(t + 1) * quarter] for t in range(RADIX)]

        def level2(p, h, n=n, cs=cs):
            sum_a[...] = p
            sum_b[...] = h
            av2 = [sum_a[pl.ds(t, sixteenth, stride=RADIX), :] for t in range(RADIX)]
            bv2 = [sum_b[pl.ds(t, sixteenth, stride=RADIX), :] for t in range(RADIX)]
            _, excl2, state = _radix_scan(
                av2, bv2, lambda p3, h3: _vreg_scan(p3, h3, carry[:, cs], reverse), reverse)
            for t in range(RADIX):
                ent_s[pl.ds(t, sixteenth, stride=RADIX), :] = excl2[t]
            carry[:, cs] = state
            return ent_s[...], state

        incl, _, _ = _radix_scan(av, bv, level2, reverse)
        for t in range(RADIX):
            rows = slice(t * quarter, (t + 1) * quarter)
            if reverse:
                ubuf[n, pl.ds(t, quarter, stride=RADIX), :] = incl[t] + hf_ref[rows, cs]
            else:
                out_ref[rows, cs] = incl[t]

    if reverse:
        for n in range(nblk):
            cs = slice(n * RNN_BLOCK, (n + 1) * RNN_BLOCK)
            half_gr = 0.5 * gr_ref[:, cs]
            silu = half_gr + half_gr * jnp.tanh(half_gr)
            out_ref[:, cs] = (ubuf[n] * silu).astype(out_ref.dtype)


def _rglru(xr, x_bf, w_m, b_m, conv_w, conv_b, wg, lam, reverse, h_fwd=None, gr_src=None):
    b, s, _ = xr.shape
    n_m = w_m.shape[1]
    tn_m = n_m // 2
    tile = RNN_TILE
    n_tiles = s // tile
    nslab = D_RNN // RNN_SLAB
    sub_per_tile = tile // SUBLANES
    n_sub = s // SUBLANES
    nblk = RNN_SLAB // RNN_BLOCK
    quarter = tile // RADIX

    def tidx(i):
        return (n_tiles - 1 - i) if reverse else i

    x_cur = pl.BlockSpec((None, tile, RNN_SLAB), lambda bb, c, i: (bb, tidx(i), c))
    x_prev = pl.BlockSpec((None, SUBLANES, RNN_SLAB),
                          lambda bb, c, i: (bb, jnp.maximum(tidx(i) * sub_per_tile - 1, 0), c))
    x_next = pl.BlockSpec((None, SUBLANES, RNN_SLAB),
                          lambda bb, c, i: (bb, jnp.minimum((tidx(i) + 1) * sub_per_tile, n_sub - 1), c))
    m_tiles = b * s // MM_TOKENS
    assert m_tiles * 2 == b * nslab * n_tiles

    def m_col(bb, c, i):
        return ((bb * nslab + c) * n_tiles + i) // m_tiles

    def m_row(bb, c, i):
        return ((bb * nslab + c) * n_tiles + i) % m_tiles

    time_tile = pl.BlockSpec((None, tile, RNN_SLAB), lambda bb, c, i: (bb, tidx(i), c))
    gate_w = pl.BlockSpec((nblk, 2 * RNN_BLOCK, 2 * RNN_BLOCK), lambda bb, c, i: (c, 0, 0))
    lam_spec = pl.BlockSpec((1, RNN_SLAB), lambda bb, c, i: (0, c))
    in_specs = [
        pl.BlockSpec((MM_TOKENS, D_MODEL), lambda bb, c, i: (m_row(bb, c, i), 0)),
        pl.BlockSpec((D_MODEL, tn_m), lambda bb, c, i: (0, m_col(bb, c, i))),
        pl.BlockSpec((1, tn_m), lambda bb, c, i: (0, m_col(bb, c, i))),
    ]
    args = [x_bf.reshape(b * s, D_MODEL), w_m, b_m]
    scan_scratch = [
        pltpu.VMEM((quarter, RNN_BLOCK), jnp.float32),
        pltpu.VMEM((quarter, RNN_BLOCK), jnp.float32),
        pltpu.VMEM((quarter, RNN_BLOCK), jnp.float32),
        pltpu.VMEM((1, RNN_SLAB), jnp.float32),
    ]
    out_specs = [pl.BlockSpec((MM_TOKENS, tn_m), lambda bb, c, i: (m_row(bb, c, i), m_col(bb, c, i))), time_tile]
    state_sds = jax.ShapeDtypeStruct((b, s, D_RNN), jnp.float32)
    if reverse:
        in_specs += [time_tile, gate_w, lam_spec, time_tile, time_tile]
        args += [xr, wg, lam, h_fwd, gr_src]
        scratch = scan_scratch + [pltpu.VMEM((nblk, tile, RNN_BLOCK), jnp.float32)]
        out_shape = [jax.ShapeDtypeStruct((b * s, n_m), jnp.float32),
                     jax.ShapeDtypeStruct((b, s, D_RNN), jnp.bfloat16)]
    else:
        in_specs += [x_prev, x_cur, x_next,
                     pl.BlockSpec((CONV_WIDTH, RNN_SLAB), lambda bb, c, i: (0, c)),
                     pl.BlockSpec((1, RNN_SLAB), lambda bb, c, i: (0, c)),
                     gate_w, lam_spec]
        args += [xr, xr, xr, conv_w, conv_b, wg, lam]
        scratch = [pltpu.VMEM((nblk, tile + 2 * SUBLANES, RNN_BLOCK), jnp.float32)] + scan_scratch
        out_specs.append(time_tile)
        out_shape = [jax.ShapeDtypeStruct((b * s, n_m), jnp.float32), state_sds, state_sds]
    outs = pl.pallas_call(
        functools.partial(_rglru_kernel, tile=tile, n_tiles=n_tiles, reverse=reverse),
        grid=(b, nslab, n_tiles),
        in_specs=in_specs,
        out_specs=out_specs,
        out_shape=out_shape,
        scratch_shapes=scratch,
        compiler_params=_params("arbitrary", "arbitrary", "arbitrary"),
        name="rglru_bwd" if reverse else "rglru_fwd",
    )(*args)
    return (outs[0].reshape(b, s, n_m),) + tuple(outs[1:])


def _merge_kernel(x_ref, o0_ref, o1_ref, o2_ref, l0_ref, l1_ref, l2_ref, ga_ref, ur_ref,
                  gma_lo_ref, gma_hi_ref, gmr_lo_ref, gmr_hi_ref,
                  wa_ref, wr_ref, wo_ref, lng_ref, lnb_ref, y_ref, ybf_ref):
    def heads(ref):
        return jnp.concatenate([ref[h] for h in range(HEADS_PER_GROUP)], axis=1)

    gma = jnp.concatenate([gma_lo_ref[...], gma_hi_ref[...]], axis=1)
    gmr = jnp.concatenate([gmr_lo_ref[...], gmr_hi_ref[...]], axis=1)

    l0, l1, l2 = heads(l0_ref), heads(l1_ref), heads(l2_ref)
    mx = jnp.maximum(jnp.maximum(l0, l1), l2)
    e0, e1, e2 = jnp.exp(l0 - mx), jnp.exp(l1 - mx), jnp.exp(l2 - mx)
    inv = 1.0 / (e0 + e1 + e2)
    oa = (e0 * inv) * heads(o0_ref) + (e1 * inv) * heads(o1_ref) + (e2 * inv) * heads(o2_ref)
    ga = ga_ref[...]
    ua = (oa * (ga * _sigmoid(ga))).astype(jnp.bfloat16)
    ya = jnp.dot(ua, wa_ref[...], preferred_element_type=jnp.float32)
    yr = jnp.dot(ur_ref[...], wr_ref[...], preferred_element_type=jnp.float32)
    mixed = _sigmoid(gma) * ya + _sigmoid(gmr) * yr
    out = jnp.dot(mixed.astype(jnp.bfloat16), wo_ref[...], preferred_element_type=jnp.float32)
    v = ALPHA * x_ref[...] + out
    mu = jnp.mean(v, axis=-1, keepdims=True)
    vc = v - mu
    var = jnp.mean(vc * vc, axis=-1, keepdims=True)
    y = vc * lax.rsqrt(var + LN_EPS) * lng_ref[...] + lnb_ref[...]
    y_ref[...] = y
    ybf_ref[...] = y.astype(jnp.bfloat16)


def _merge_out(x, attn, z1, z2, u_r, wa, wr, wo, ln_g, ln_b):
    bsz, s, d = x.shape
    tm = MERGE_TM
    row = lambda w: pl.BlockSpec((None, tm, w), lambda bb, i: (bb, i, 0))
    head = pl.BlockSpec((None, HEADS_PER_GROUP, tm, HEAD_DIM), lambda bb, i: (bb, 0, i, 0))
    const = lambda shp: pl.BlockSpec(shp, lambda bb, i: (0, 0), pipeline_mode=pl.Buffered(1))
    half = lambda j: pl.BlockSpec((None, tm, GM_SPLIT), lambda bb, i: (bb, i, j))
    in_specs = [
        row(d),
        head, head, head, head, head, head,
        pl.BlockSpec((None, tm, D_GROUP), lambda bb, i: (bb, i, (GATE2_COLS - D_GROUP) // D_GROUP)),
        row(D_RNN),
        half(D_RNN // GM_SPLIT),
        half(0),
        half(1), half(2),
        const((D_GROUP, D_MODEL)), const((D_RNN, D_MODEL)), const((D_MODEL, D_MODEL)),
        const((1, D_MODEL)), const((1, D_MODEL)),
    ]
    (o0, l0), (o1, l1), (o2, l2) = attn
    return pl.pallas_call(
        _merge_kernel,
        grid=(bsz, s // tm),
        in_specs=in_specs,
        out_specs=[row(d), row(d)],
        out_shape=[jax.ShapeDtypeStruct((bsz, s, d), jnp.float32), jax.ShapeDtypeStruct((bsz, s, d), jnp.bfloat16)],
        compiler_params=_params("parallel", "parallel"),
        name="merge_out",
    )(x, o0, o1, o2, l0, l1, l2, z2, u_r, z1, z2, z2, z2, wa, wr, wo, ln_g, ln_b)


def _prep_layer(l, w_in, b_in, conv_w, conv_b, lru_w, lru_b, lru_lam, w_attn_o, w_rnn_o, w_out, ln_g, ln_b):
    bf = jnp.bfloat16
    a3 = 3 * D_ATTN
    c_ga, c_xr, c_gr, c_gm = a3, a3 + D_GROUP, a3 + D_GROUP + D_RNN, a3 + D_GROUP + 2 * D_RNN

    def gate1_cols(w):
        return w[..., c_gr:c_gm + GM_SPLIT]

    def gate2_cols(w):
        return jnp.concatenate([w[..., c_gm + GM_SPLIT:], w[..., c_ga:c_xr]], axis=-1)

    def group_cols(w, g):
        return jnp.concatenate([w[..., part * D_ATTN + g * D_GROUP:part * D_ATTN + (g + 1) * D_GROUP]
                                for part in range(3)], axis=-1)

    def gate_w(direction):
        w = 0.5 * jnp.concatenate([lru_w[l, direction, 0], lru_w[l, direction, 1]], axis=-1)
        bias = 0.5 * jnp.concatenate([lru_b[l, direction, 0], lru_b[l, direction, 1]], axis=-1)
        terms, rem = [], bias
        for _ in range(GATE_BIAS_ROWS):
            term = rem.astype(bf)
            terms.append(term)
            rem = rem - term.astype(jnp.float32)
        pad = jnp.zeros((N_RNN_BLOCKS, RNN_BLOCK - GATE_BIAS_ROWS, 2 * RNN_BLOCK), bf)
        return jnp.concatenate([w.astype(bf), jnp.stack(terms, axis=1), pad], axis=1)

    return dict(
        w_qkv=[group_cols(w_in[l], g).astype(bf) for g in range(N_GROUPS)],
        b_qkv=[group_cols(b_in[l], g)[None, :] for g in range(N_GROUPS)],
        w_xr=w_in[l, :, c_xr:c_gr].astype(bf), b_xr=b_in[l, None, c_xr:c_gr],
        w_g1=gate1_cols(w_in[l]).astype(bf), b_g1=gate1_cols(b_in[l])[None, :],
        w_g2=gate2_cols(w_in[l]).astype(bf), b_g2=gate2_cols(b_in[l])[None, :],
        conv_w=conv_w[l], conv_b=conv_b[l][None, :],
        wg=[gate_w(0), gate_w(1)],
        lam=[lru_lam[l, 0][None, :], lru_lam[l, 1][None, :]],
        wa=w_attn_o[l].astype(bf), wr=w_rnn_o[l].astype(bf), wo=w_out[l].astype(bf),
        ln_g=ln_g[l][None, :], ln_b=ln_b[l][None, :],
    )


def _layer(x, x_bf, p, biases):
    attn = []
    for g, (_, dil) in enumerate(DILATED_GROUPS):
        if x_bf is None:
            qkv, x_bf = _qkv_proj(x, p["w_qkv"][g], p["b_qkv"][g], dil, g)
        else:
            qkv = _qkv_proj(x_bf, p["w_qkv"][g], p["b_qkv"][g], dil, g)
        attn.append(_attention_group(qkv, biases[g], g, dil))
    xr = _rest_proj(x_bf, p["w_xr"], p["b_xr"])
    z1, h_fwd, xconv = _rglru(xr, x_bf, p["w_g1"], p["b_g1"], p["conv_w"], p["conv_b"],
                              p["wg"][0], p["lam"][0], False)
    z2, u_r = _rglru(xconv, x_bf, p["w_g2"], p["b_g2"], None, None,
                     p["wg"][1], p["lam"][1], True, h_fwd, z1)
    return _merge_out(x, attn, z1, z2, u_r, p["wa"], p["wr"], p["wo"], p["ln_g"], p["ln_b"])


def kernel(x_prompt, x_sample, w_in, b_in, conv_w, conv_b, lru_w, lru_b, lru_lam,
           w_attn_o, w_rnn_o, w_out, ln_g, ln_b, rel_bias):
    layers = [_prep_layer(l, w_in, b_in, conv_w, conv_b, lru_w, lru_b, lru_lam,
                          w_attn_o, w_rnn_o, w_out, ln_g, ln_b) for l in range(DEPTH)]
    biases = [_band_bias(rel_bias, g, dil) for g, (_, dil) in enumerate(DILATED_GROUPS)]

    def trunk(x):
        x_bf = None
        for p in layers:
            x, x_bf = _layer(x, x_bf, p, biases)
        return x

    return (trunk(x_prompt), trunk(x_sample))
```

```python
import functools

import numpy as np
import jax
import jax.numpy as jnp
from jax import lax
from jax.experimental import pallas as pl
from jax.experimental.pallas import tpu as pltpu

D_MODEL = 2048
DEPTH = 2
HEAD_DIM = 128
HEADS_PER_GROUP = 4
DILATED_GROUPS = ((128, 1), (512, 4), (2048, 16))
N_GROUPS = len(DILATED_GROUPS)
D_ATTN = N_GROUPS * HEADS_PER_GROUP * HEAD_DIM
D_GROUP = HEADS_PER_GROUP * HEAD_DIM
D_RNN = D_MODEL
N_RNN_BLOCKS = 16
RNN_BLOCK = D_RNN // N_RNN_BLOCKS
CONV_WIDTH = 4
CONV_LEFT = 2
LRU_C = 8.0
REL_BUCKETS = 32
REL_MAX_DIST = 1024
ALPHA = (2.0 * DEPTH) ** 0.25
LN_EPS = 1e-5
NEG_INF = -1e30

LANES = 128
SUBLANES = 8
VMEM_LIMIT_BYTES = 56 * 1024 * 1024

HALF = 64
Q_SUB = 128
K_WIN = Q_SUB + 2 * HALF
ATTN_TOKENS = 2048
ATTN_RESIDUES_PER_TRIP = 16

RNN_TILE = 512
RNN_SLAB = 1024
RADIX = 4
GATE_BIAS_ROWS = 3
LOG2_E = 1.4426950408889634

MM_TOKENS = RNN_TILE
MM_PIECE = 256

PROJ_TM = 2048
MERGE_TM = 256

GM_SPLIT = D_MODEL // 2
GATE1_COLS = D_RNN + GM_SPLIT
GATE2_COLS = (D_MODEL - GM_SPLIT) + D_MODEL + D_GROUP


def _params(*sem):
    return pltpu.CompilerParams(dimension_semantics=sem, vmem_limit_bytes=VMEM_LIMIT_BYTES)


def _sigmoid(x):
    return 0.5 + 0.5 * jnp.tanh(0.5 * x)


def _rest_proj_kernel(x_ref, w_ref, b_ref, o_ref):
    acc = jnp.dot(x_ref[...], w_ref[...], preferred_element_type=jnp.float32)
    o_ref[...] = acc + b_ref[...]


def _rest_proj(x_bf, w_bf, b):
    bsz, s, d = x_bf.shape
    n = w_bf.shape[1]
    tm, tn = PROJ_TM // 2, n
    return pl.pallas_call(
        _rest_proj_kernel,
        grid=(bsz, s // tm, n // tn),
        in_specs=[
            pl.BlockSpec((None, tm, d), lambda bb, i, j: (bb, i, 0)),
            pl.BlockSpec((d, tn), lambda bb, i, j: (0, j)),
            pl.BlockSpec((1, tn), lambda bb, i, j: (0, j)),
        ],
        out_specs=pl.BlockSpec((None, tm, tn), lambda bb, i, j: (bb, i, j)),
        out_shape=jax.ShapeDtypeStruct((bsz, s, n), jnp.float32),
        compiler_params=_params("parallel", "parallel", "arbitrary"),
        name="xr_proj",
    )(x_bf, w_bf, b)


def _qkv_proj_kernel(x_ref, w_ref, b_ref, o_ref, *rest, dil, tm, tn, emit_bf16):
    if emit_bf16:
        xbf_ref, rest = rest[0], rest[1:]

        @pl.when(pl.program_id(2) == 0)
        def _():
            xbf_ref[...] = x_ref[...].astype(jnp.bfloat16)

        x = xbf_ref[...]
    else:
        x = x_ref[...]
    acc = jnp.dot(x, w_ref[...], preferred_element_type=jnp.float32) + b_ref[...]
    if dil == 1:
        o_ref[0] = acc.astype(o_ref.dtype)
        return
    acc_s = rest[0]
    rows = tm // dil
    for c in range(tn // LANES):
        cs = slice(c * LANES, (c + 1) * LANES)
        acc_s[c] = acc[:, cs]
        if dil == RADIX:
            for r in range(dil):
                o_ref[r, :, cs] = acc_s[c, pl.ds(r, rows, stride=dil), :].astype(o_ref.dtype)
        else:
            assert dil == RADIX * RADIX
            mid_s = rest[1]
            for lo in range(RADIX):
                mid_s[lo] = acc_s[c, pl.ds(lo, tm // RADIX, stride=RADIX), :]
            for lo in range(RADIX):
                for hi in range(RADIX):
                    o_ref[lo + RADIX * hi, :, cs] = mid_s[lo, pl.ds(hi, rows, stride=RADIX), :].astype(o_ref.dtype)


def _qkv_proj(x, w_bf, b, dil, g):
    bsz, s, d = x.shape
    n = w_bf.shape[1]
    emit_bf16 = x.dtype != jnp.bfloat16
    tm, tn = PROJ_TM // 2, n
    scratch = [] if dil == 1 else [pltpu.VMEM((tn // LANES, tm, LANES), jnp.float32)]
    if dil > RADIX:
        scratch.append(pltpu.VMEM((RADIX, tm // RADIX, LANES), jnp.float32))
    out_specs = [pl.BlockSpec((None, dil, tm // dil, tn), lambda bb, i, j: (bb, 0, i, j))]
    out_shape = [jax.ShapeDtypeStruct((bsz, dil, s // dil, n), jnp.bfloat16)]
    if emit_bf16:
        out_specs.append(pl.BlockSpec((None, tm, d), lambda bb, i, j: (bb, i, 0)))
        out_shape.append(jax.ShapeDtypeStruct((bsz, s, d), jnp.bfloat16))
    outs = pl.pallas_call(
        functools.partial(_qkv_proj_kernel, dil=dil, tm=tm, tn=tn, emit_bf16=emit_bf16),
        grid=(bsz, s // tm, n // tn),
        in_specs=[
            pl.BlockSpec((None, tm, d), lambda bb, i, j: (bb, i, 0)),
            pl.BlockSpec((d, tn), lambda bb, i, j: (0, j)),
            pl.BlockSpec((1, tn), lambda bb, i, j: (0, j)),
        ],
        out_specs=out_specs,
        out_shape=out_shape,
        scratch_shapes=scratch,
        compiler_params=_params("parallel", "parallel", "arbitrary"),
        name=f"qkv_proj_g{g}",
    )(x, w_bf, b)
    return outs if emit_bf16 else outs[0]


def _t5_bucket(rel):
    nb = REL_BUCKETS // 2
    max_exact = nb // 2
    ret = (rel > 0).astype(np.int32) * nb
    n = np.abs(rel)
    large = max_exact + (np.log(np.maximum(n, max_exact) / max_exact)
                         / np.log(REL_MAX_DIST / max_exact) * (nb - max_exact)).astype(np.int32)
    large = np.minimum(large, nb - 1)
    return (ret + np.where(n < max_exact, n, large)).astype(np.int32)


def _band_bias(rel_bias, g, dil):
    tab = rel_bias[_t5_bucket(np.arange(-HALF, HALF + 1) * dil)][:, g * HEADS_PER_GROUP:(g + 1) * HEADS_PER_GROUP]
    tab = tab * (HEAD_DIM ** 0.5)
    period = Q_SUB + K_WIN + LANES
    vec = jnp.full((HEADS_PER_GROUP, period), NEG_INF, jnp.float32).at[:, :2 * HALF + 1].set(tab.T)
    mat = jnp.tile(vec, (1, Q_SUB))[:, :Q_SUB * (period - 1)].reshape(HEADS_PER_GROUP, Q_SUB, period - 1)
    return mat[:, :, :K_WIN]


def _attn_kernel(q_ref, kp_ref, kc_ref, kn_ref, vp_ref, vc_ref, vn_ref, bias_ref,
                 o_ref, lse_ref, kbuf, vbuf, *, dil, tile, length):
    i = pl.program_id(1)
    scale = HEAD_DIM ** -0.5

    def residue(r, slot):
        kbuf[slot, 0:HALF] = kp_ref[r]
        kbuf[slot, HALF:HALF + tile] = kc_ref[r]
        kbuf[slot, HALF + tile:] = kn_ref[r]
        vbuf[slot, 0:HALF] = vp_ref[r]
        vbuf[slot, HALF:HALF + tile] = vc_ref[r]
        vbuf[slot, HALF + tile:] = vn_ref[r]
        n_sub = tile // Q_SUB
        for sb in range(n_sub):
            r0 = sb * Q_SUB
            at_edge = sb == 0 or sb == n_sub - 1
            if at_edge:
                kpos = i * tile + (r0 - HALF) + lax.broadcasted_iota(jnp.int32, (Q_SUB, K_WIN), 1)
                in_seq = (kpos >= 0) & (kpos < length)
            if dil == 1:
                rows = slice(r0, r0 + Q_SUB)
            else:
                rows = pl.ds(r + r0 * dil, Q_SUB, stride=dil)
            for h in range(HEADS_PER_GROUP):
                cs = slice(h * HEAD_DIM, (h + 1) * HEAD_DIM)
                q = q_ref[r, r0:r0 + Q_SUB, cs]
                k = kbuf[slot, r0:r0 + K_WIN, cs]
                v = vbuf[slot, r0:r0 + K_WIN, cs]
                t = lax.dot_general(q, k, (((1,), (1,)), ((), ())),
                                    preferred_element_type=jnp.float32) + bias_ref[h]
                if at_edge:
                    t = jnp.where(in_seq, t, NEG_INF)
                mt = jnp.max(t, axis=-1, keepdims=True)
                p = jnp.exp2((t - mt) * (scale * LOG2_E))
                den = jnp.sum(p, axis=-1, keepdims=True)
                o = jnp.dot(p.astype(jnp.bfloat16), v, preferred_element_type=jnp.float32)
                o_ref[h, rows, :] = o / den
                lse_ref[h, rows, :] = jnp.broadcast_to(mt * scale + jnp.log(den), (Q_SUB, HEAD_DIM))

    per_trip = kbuf.shape[0]

    def trip(it, _):
        for slot in range(per_trip):
            residue(it * per_trip + slot, slot)
        return 0

    if dil == per_trip:
        trip(0, 0)
    else:
        lax.fori_loop(0, dil // per_trip, trip, 0)


def _attention_group(qkv, bias, g, dil):
    bsz, _, length, _ = qkv.shape
    s = length * dil
    tile = ATTN_TOKENS // dil
    per_trip = min(dil, ATTN_RESIDUES_PER_TRIP)
    hb = tile // HALF
    n_half_blocks = length // HALF

    def cur(part):
        return pl.BlockSpec((None, dil, tile, D_GROUP), lambda bb, i: (bb, 0, i, part))

    def prev(part):
        return pl.BlockSpec((None, dil, HALF, D_GROUP),
                            lambda bb, i: (bb, 0, jnp.maximum(i * hb - 1, 0), part))

    def nxt(part):
        return pl.BlockSpec((None, dil, HALF, D_GROUP),
                            lambda bb, i: (bb, 0, jnp.minimum((i + 1) * hb, n_half_blocks - 1), part))

    out_spec = pl.BlockSpec((None, HEADS_PER_GROUP, tile * dil, HEAD_DIM), lambda bb, i: (bb, 0, i, 0))
    out_sds = jax.ShapeDtypeStruct((bsz, HEADS_PER_GROUP, s, HEAD_DIM), jnp.float32)
    return pl.pallas_call(
        functools.partial(_attn_kernel, dil=dil, tile=tile, length=length),
        grid=(bsz, length // tile),
        in_specs=[cur(0), prev(1), cur(1), nxt(1), prev(2), cur(2), nxt(2),
                  pl.BlockSpec((HEADS_PER_GROUP, Q_SUB, K_WIN), lambda bb, i: (0, 0, 0))],
        out_specs=[out_spec, out_spec],
        out_shape=[out_sds, out_sds],
        scratch_shapes=[pltpu.VMEM((per_trip, tile + 2 * HALF, D_GROUP), jnp.bfloat16),
                        pltpu.VMEM((per_trip, tile + 2 * HALF, D_GROUP), jnp.bfloat16)],
        compiler_params=_params("parallel", "arbitrary"),
        name=f"attn_g{g}",
    )(qkv, qkv, qkv, qkv, qkv, qkv, qkv, bias)


def _vreg_scan(a, b, state, reverse):
    groups = a.shape[0] // SUBLANES
    a3 = a.reshape(groups, SUBLANES, LANES)
    b3 = b.reshape(groups, SUBLANES, LANES)
    sub = lax.broadcasted_iota(jnp.int32, (groups, SUBLANES, LANES), 1)
    for shift in (1, 2, 4):
        keep = (sub < SUBLANES - shift) if reverse else (sub >= shift)
        amount = SUBLANES - shift if reverse else shift
        a_sh = jnp.where(keep, pltpu.roll(a3, amount, axis=1), 1.0)
        b_sh = jnp.where(keep, pltpu.roll(b3, amount, axis=1), 0.0)
        b3 = a3 * b_sh + b3
        a3 = a3 * a_sh
    sub2 = lax.broadcasted_iota(jnp.int32, (SUBLANES, LANES), 0)
    entering = [None] * groups
    for gi in (range(groups - 1, -1, -1) if reverse else range(groups)):
        sb = jnp.broadcast_to(state, (SUBLANES, LANES))
        incl = b3[gi] + a3[gi] * sb
        if reverse:
            entering[gi] = jnp.where(sub2 == SUBLANES - 1, sb, pltpu.roll(incl, SUBLANES - 1, axis=0))
            state = incl[0:1]
        else:
            entering[gi] = jnp.where(sub2 == 0, sb, pltpu.roll(incl, 1, axis=0))
            state = incl[SUBLANES - 1:SUBLANES]
    return jnp.concatenate(entering, axis=0), state


def _radix_scan(av, bv, entering_fn, reverse):
    order = tuple(range(RADIX - 1, -1, -1)) if reverse else tuple(range(RADIX))
    h = [None] * RADIX
    p = [None] * RADIX
    h[order[0]], p[order[0]] = bv[order[0]], av[order[0]]
    for before, t in zip(order[:-1], order[1:]):
        h[t] = av[t] * h[before] + bv[t]
        p[t] = av[t] * p[before]
    entering, state = entering_fn(p[order[-1]], h[order[-1]])
    incl = [h[t] + p[t] * entering for t in range(RADIX)]
    excl = [None] * RADIX
    excl[order[0]] = entering
    for before, t in zip(order[:-1], order[1:]):
        excl[t] = incl[before]
    return incl, excl, state


def _rglru_kernel(*refs, tile, n_tiles, reverse):
    if reverse:
        (xm_ref, wm_ref, bm_ref, conv_ref, wg_ref, lam_ref, hf_ref, gr_ref,
         zm_ref, out_ref, sum_a, sum_b, ent_s, carry, ubuf) = refs
    else:
        (xm_ref, wm_ref, bm_ref, xp_ref, xc_ref, xn_ref, cw_ref, cb_ref, wg_ref, lam_ref,
         zm_ref, out_ref, conv_ref, ext, sum_a, sum_b, ent_s, carry) = refs

    def gate_proj_piece(k):
        cols = slice(k * MM_PIECE, (k + 1) * MM_PIECE)
        zm_ref[:, cols] = (jnp.dot(xm_ref[...], wm_ref[:, cols], preferred_element_type=jnp.float32)
                           + bm_ref[:, cols])

    n_pieces = zm_ref.shape[1] // MM_PIECE
    step = pl.program_id(2)
    ti = (n_tiles - 1 - step) if reverse else step
    quarter = tile // RADIX
    sixteenth = quarter // RADIX
    nblk = RNN_SLAB // RNN_BLOCK

    @pl.when(step == 0)
    def _():
        carry[...] = jnp.zeros_like(carry)

    lam = lam_ref[...]
    half_csp = (0.5 * LRU_C) * (jnp.maximum(-lam, 0.0) + jnp.log1p(jnp.exp(-jnp.abs(lam))))
    bias_taps = (lax.broadcasted_iota(jnp.int32, (tile, RNN_BLOCK), 1) < GATE_BIAS_ROWS).astype(jnp.bfloat16)

    assert n_pieces <= nblk
    for n in range(nblk):
        cs = slice(n * RNN_BLOCK, (n + 1) * RNN_BLOCK)
        if reverse:
            xc = conv_ref[:, cs]
        else:
            ext[n, 0:SUBLANES] = jnp.where(ti > 0, xp_ref[:, cs], 0.0)
            ext[n, SUBLANES:SUBLANES + tile] = xc_ref[:, cs]
            ext[n, SUBLANES + tile:] = jnp.where(ti < n_tiles - 1, xn_ref[:, cs], 0.0)
            taps = {u: ext[n, pl.ds(SUBLANES + u, quarter, stride=RADIX), :]
                    for u in range(-CONV_LEFT, RADIX + CONV_WIDTH - 1 - CONV_LEFT)}
            xc_t = []
            for t in range(RADIX):
                acc = cb_ref[:, cs]
                for j in range(CONV_WIDTH):
                    acc = acc + taps[t + j - CONV_LEFT] * cw_ref[j:j + 1, cs]
                xc_t.append(acc)
            xc = jnp.concatenate(xc_t, axis=0)
            conv_ref[:, cs] = xc
        lhs = jnp.concatenate([xc.astype(jnp.bfloat16), bias_taps], axis=1)
        z = jnp.dot(lhs, wg_ref[n], preferred_element_type=jnp.float32)
        if n < n_pieces:
            gate_proj_piece(n)
        th_r = jnp.tanh(z[:, :RNN_BLOCK])
        th_i = jnp.tanh(z[:, RNN_BLOCK:])
        hsp = half_csp[:, cs]
        neg_log_a = hsp + hsp * th_r
        a = jnp.exp2(neg_log_a * (-LOG2_E))
        half_xc = 0.5 * xc
        one_m_a2 = jnp.tanh(neg_log_a) * (1.0 + a * a)
        root = jnp.where(one_m_a2 > 0.0, one_m_a2 * lax.rsqrt(one_m_a2), 0.0)
        b = root * (half_xc + half_xc * th_i)
        av = [a[t * quarter:(t + 1) * quarter] for t in range(RADIX)]
        bv = [b[t * quarter:(t + 1) * quarter] for t in range(RADIX)]

        def level2(p, h, n=n, cs=cs):
            sum_a[...] = p
            sum_b[...] = h
            av2 = [sum_a[pl.ds(t, sixteenth, stride=RADIX), :] for t in range(RADIX)]
            bv2 = [sum_b[pl.ds(t, sixteenth, stride=RADIX), :] for t in range(RADIX)]
            _, excl2, state = _radix_scan(
                av2, bv2, lambda p3, h3: _vreg_scan(p3, h3, carry[:, cs], reverse), reverse)
            for t in range(RADIX):
                ent_s[pl.ds(t, sixteenth, stride=RADIX), :] = excl2[t]
            carry[:, cs] = state
            return ent_s[...], state

        incl, _, _ = _radix_scan(av, bv, level2, reverse)
        for t in range(RADIX):
            rows = slice(t * quarter, (t + 1) * quarter)
            if reverse:
                ubuf[n, pl.ds(t, quarter, stride=RADIX), :] = incl[t] + hf_ref[rows, cs]
            else:
                out_ref[rows, cs] = incl[t]

    if reverse:
        for n in range(nblk):
            cs = slice(n * RNN_BLOCK, (n + 1) * RNN_BLOCK)
            half_gr = 0.5 * gr_ref[:, cs]
            silu = half_gr + half_gr * jnp.tanh(half_gr)
            out_ref[:, cs] = (ubuf[n] * silu).astype(out_ref.dtype)


def _rglru(xr, x_bf, w_m, b_m, conv_w, conv_b, wg, lam, reverse, h_fwd=None, gr_src=None):
    b, s, _ = xr.shape
    n_m = w_m.shape[1]
    tn_m = n_m // 2
    tile = RNN_TILE
    n_tiles = s // tile
    nslab = D_RNN // RNN_SLAB
    sub_per_tile = tile // SUBLANES
    n_sub = s // SUBLANES
    nblk = RNN_SLAB // RNN_BLOCK
    quarter = tile // RADIX

    def tidx(i):
        return (n_tiles - 1 - i) if reverse else i

    x_cur = pl.BlockSpec((None, tile, RNN_SLAB), lambda bb, c, i: (bb, tidx(i), c))
    x_prev = pl.BlockSpec((None, SUBLANES, RNN_SLAB),
                          lambda bb, c, i: (bb, jnp.maximum(tidx(i) * sub_per_tile - 1, 0), c))
    x_next = pl.BlockSpec((None, SUBLANES, RNN_SLAB),
                          lambda bb, c, i: (bb, jnp.minimum((tidx(i) + 1) * sub_per_tile, n_sub - 1), c))
    m_tiles = b * s // MM_TOKENS
    assert m_tiles * 2 == b * nslab * n_tiles

    def m_col(bb, c, i):
        return ((bb * nslab + c) * n_tiles + i) // m_tiles

    def m_row(bb, c, i):
        return ((bb * nslab + c) * n_tiles + i) % m_tiles

    time_tile = pl.BlockSpec((None, tile, RNN_SLAB), lambda bb, c, i: (bb, tidx(i), c))
    gate_w = pl.BlockSpec((nblk, 2 * RNN_BLOCK, 2 * RNN_BLOCK), lambda bb, c, i: (c, 0, 0))
    lam_spec = pl.BlockSpec((1, RNN_SLAB), lambda bb, c, i: (0, c))
    in_specs = [
        pl.BlockSpec((MM_TOKENS, D_MODEL), lambda bb, c, i: (m_row(bb, c, i), 0)),
        pl.BlockSpec((D_MODEL, tn_m), lambda bb, c, i: (0, m_col(bb, c, i))),
        pl.BlockSpec((1, tn_m), lambda bb, c, i: (0, m_col(bb, c, i))),
    ]
    args = [x_bf.reshape(b * s, D_MODEL), w_m, b_m]
    scan_scratch = [
        pltpu.VMEM((quarter, RNN_BLOCK), jnp.float32),
        pltpu.VMEM((quarter, RNN_BLOCK), jnp.float32),
        pltpu.VMEM((quarter, RNN_BLOCK), jnp.float32),
        pltpu.VMEM((1, RNN_SLAB), jnp.float32),
    ]
    out_specs = [pl.BlockSpec((MM_TOKENS, tn_m), lambda bb, c, i: (m_row(bb, c, i), m_col(bb, c, i))), time_tile]
    state_sds = jax.ShapeDtypeStruct((b, s, D_RNN), jnp.float32)
    if reverse:
        in_specs += [time_tile, gate_w, lam_spec, time_tile, time_tile]
        args += [xr, wg, lam, h_fwd, gr_src]
        scratch = scan_scratch + [pltpu.VMEM((nblk, tile, RNN_BLOCK), jnp.float32)]
        out_shape = [jax.ShapeDtypeStruct((b * s, n_m), jnp.float32),
                     jax.ShapeDtypeStruct((b, s, D_RNN), jnp.bfloat16)]
    else:
        in_specs += [x_prev, x_cur, x_next,
                     pl.BlockSpec((CONV_WIDTH, RNN_SLAB), lambda bb, c, i: (0, c)),
                     pl.BlockSpec((1, RNN_SLAB), lambda bb, c, i: (0, c)),
                     gate_w, lam_spec]
        args += [xr, xr, xr, conv_w, conv_b, wg, lam]
        scratch = [pltpu.VMEM((nblk, tile + 2 * SUBLANES, RNN_BLOCK), jnp.float32)] + scan_scratch
        out_specs.append(time_tile)
        out_shape = [jax.ShapeDtypeStruct((b * s, n_m), jnp.float32), state_sds, state_sds]
    outs = pl.pallas_call(
        functools.partial(_rglru_kernel, tile=tile, n_tiles=n_tiles, reverse=reverse),
        grid=(b, nslab, n_tiles),
        in_specs=in_specs,
        out_specs=out_specs,
        out_shape=out_shape,
        scratch_shapes=scratch,
        compiler_params=_params("arbitrary", "arbitrary", "arbitrary"),
        name="rglru_bwd" if reverse else "rglru_fwd",
    )(*args)
    return (outs[0].reshape(b, s, n_m),) + tuple(outs[1:])


def _merge_kernel(x_ref, o0_ref, o1_ref, o2_ref, l0_ref, l1_ref, l2_ref, ga_ref, ur_ref,
                  gma_lo_ref, gma_hi_ref, gmr_lo_ref, gmr_hi_ref,
                  wa_ref, wr_ref, wo_ref, lng_ref, lnb_ref, y_ref, ybf_ref):
    def heads(ref):
        return jnp.concatenate([ref[h] for h in range(HEADS_PER_GROUP)], axis=1)

    gma = jnp.concatenate([gma_lo_ref[...], gma_hi_ref[...]], axis=1)
    gmr = jnp.concatenate([gmr_lo_ref[...], gmr_hi_ref[...]], axis=1)

    l0, l1, l2 = heads(l0_ref), heads(l1_ref), heads(l2_ref)
    mx = jnp.maximum(jnp.maximum(l0, l1), l2)
    e0, e1, e2 = jnp.exp(l0 - mx), jnp.exp(l1 - mx), jnp.exp(l2 - mx)
    inv = 1.0 / (e0 + e1 + e2)
    oa = (e0 * inv) * heads(o0_ref) + (e1 * inv) * heads(o1_ref) + (e2 * inv) * heads(o2_ref)
    ga = ga_ref[...]
    ua = (oa * (ga * _sigmoid(ga))).astype(jnp.bfloat16)
    ya = jnp.dot(ua, wa_ref[...], preferred_element_type=jnp.float32)
    yr = jnp.dot(ur_ref[...], wr_ref[...], preferred_element_type=jnp.float32)
    mixed = _sigmoid(gma) * ya + _sigmoid(gmr) * yr
    out = jnp.dot(mixed.astype(jnp.bfloat16), wo_ref[...], preferred_element_type=jnp.float32)
    v = ALPHA * x_ref[...] + out
    mu = jnp.mean(v, axis=-1, keepdims=True)
    vc = v - mu
    var = jnp.mean(vc * vc, axis=-1, keepdims=True)
    y = vc * lax.rsqrt(var + LN_EPS) * lng_ref[...] + lnb_ref[...]
    y_ref[...] = y
    ybf_ref[...] = y.astype(jnp.bfloat16)


def _merge_out(x, attn, z1, z2, u_r, wa, wr, wo, ln_g, ln_b):
    bsz, s, d = x.shape
    tm = MERGE_TM
    row = lambda w: pl.BlockSpec((None, tm, w), lambda bb, i: (bb, i, 0))
    head = pl.BlockSpec((None, HEADS_PER_GROUP, tm, HEAD_DIM), lambda bb, i: (bb, 0, i, 0))
    const = lambda shp: pl.BlockSpec(shp, lambda bb, i: (0, 0), pipeline_mode=pl.Buffered(1))
    half = lambda j: pl.BlockSpec((None, tm, GM_SPLIT), lambda bb, i: (bb, i, j))
    in_specs = [
        row(d),
        head, head, head, head, head, head,
        pl.BlockSpec((None, tm, D_GROUP), lambda bb, i: (bb, i, (GATE2_COLS - D_GROUP) // D_GROUP)),
        row(D_RNN),
        half(D_RNN // GM_SPLIT),
        half(0),
        half(1), half(2),
        const((D_GROUP, D_MODEL)), const((D_RNN, D_MODEL)), const((D_MODEL, D_MODEL)),
        const((1, D_MODEL)), const((1, D_MODEL)),
    ]
    (o0, l0), (o1, l1), (o2, l2) = attn
    return pl.pallas_call(
        _merge_kernel,
        grid=(bsz, s // tm),
        in_specs=in_specs,
        out_specs=[row(d), row(d)],
        out_shape=[jax.ShapeDtypeStruct((bsz, s, d), jnp.float32), jax.ShapeDtypeStruct((bsz, s, d), jnp.bfloat16)],
        compiler_params=_params("parallel", "parallel"),
        name="merge_out",
    )(x, o0, o1, o2, l0, l1, l2, z2, u_r, z1, z2, z2, z2, wa, wr, wo, ln_g, ln_b)


def _prep_layer(l, w_in, b_in, conv_w, conv_b, lru_w, lru_b, lru_lam, w_attn_o, w_rnn_o, w_out, ln_g, ln_b):
    bf = jnp.bfloat16
    a3 = 3 * D_ATTN
    c_ga, c_xr, c_gr, c_gm = a3, a3 + D_GROUP, a3 + D_GROUP + D_RNN, a3 + D_GROUP + 2 * D_RNN

    def gate1_cols(w):
        return w[..., c_gr:c_gm + GM_SPLIT]

    def gate2_cols(w):
        return jnp.concatenate([w[..., c_gm + GM_SPLIT:], w[..., c_ga:c_xr]], axis=-1)

    def group_cols(w, g):
        return jnp.concatenate([w[..., part * D_ATTN + g * D_GROUP:part * D_ATTN + (g + 1) * D_GROUP]
                                for part in range(3)], axis=-1)

    def gate_w(direction):
        w = 0.5 * jnp.concatenate([lru_w[l, direction, 0], lru_w[l, direction, 1]], axis=-1)
        bias = 0.5 * jnp.concatenate([lru_b[l, direction, 0], lru_b[l, direction, 1]], axis=-1)
        terms, rem = [], bias
        for _ in range(GATE_BIAS_ROWS):
            term = rem.astype(bf)
            terms.append(term)
            rem = rem - term.astype(jnp.float32)
        pad = jnp.zeros((N_RNN_BLOCKS, RNN_BLOCK - GATE_BIAS_ROWS, 2 * RNN_BLOCK), bf)
        return jnp.concatenate([w.astype(bf), jnp.stack(terms, axis=1), pad], axis=1)

    return dict(
        w_qkv=[group_cols(w_in[l], g).astype(bf) for g in range(N_GROUPS)],
        b_qkv=[group_cols(b_in[l], g)[None, :] for g in range(N_GROUPS)],
        w_xr=w_in[l, :, c_xr:c_gr].astype(bf), b_xr=b_in[l, None, c_xr:c_gr],
        w_g1=gate1_cols(w_in[l]).astype(bf), b_g1=gate1_cols(b_in[l])[None, :],
        w_g2=gate2_cols(w_in[l]).astype(bf), b_g2=gate2_cols(b_in[l])[None, :],
        conv_w=conv_w[l], conv_b=conv_b[l][None, :],
        wg=[gate_w(0), gate_w(1)],
        lam=[lru_lam[l, 0][None, :], lru_lam[l, 1][None, :]],
        wa=w_attn_o[l].astype(bf), wr=w_rnn_o[l].astype(bf), wo=w_out[l].astype(bf),
        ln_g=ln_g[l][None, :], ln_b=ln_b[l][None, :],
    )


def _layer(x, x_bf, p, biases):
    attn = []
    for g, (_, dil) in enumerate(DILATED_GROUPS):
        if x_bf is None:
            qkv, x_bf = _qkv_proj(x, p["w_qkv"][g], p["b_qkv"][g], dil, g)
        else:
            qkv = _qkv_proj(x_bf, p["w_qkv"][g], p["b_qkv"][g], dil, g)
        attn.append(_attention_group(qkv, biases[g], g, dil))
    xr = _rest_proj(x_bf, p["w_xr"], p["b_xr"])
    z1, h_fwd, xconv = _rglru(xr, x_bf, p["w_g1"], p["b_g1"], p["conv_w"], p["conv_b"],
                              p["wg"][0], p["lam"][0], False)
    z2, u_r = _rglru(xconv, x_bf, p["w_g2"], p["b_g2"], None, None,
                     p["wg"][1], p["lam"][1], True, h_fwd, z1)
    return _merge_out(x, attn, z1, z2, u_r, p["wa"], p["wr"], p["wo"], p["ln_g"], p["ln_b"])


def kernel(x_prompt, x_sample, w_in, b_in, conv_w, conv_b, lru_w, lru_b, lru_lam,
           w_attn_o, w_rnn_o, w_out, ln_g, ln_b, rel_bias):
    layers = [_prep_layer(l, w_in, b_in, conv_w, conv_b, lru_w, lru_b, lru_lam,
                          w_attn_o, w_rnn_o, w_out, ln_g, ln_b) for l in range(DEPTH)]
    biases = [_band_bias(rel_bias, g, dil) for g, (_, dil) in enumerate(DILATED_GROUPS)]

    def trunk(x):
        x_bf = None
        for p in layers:
            x, x_bf = _layer(x, x_bf, p, biases)
        return x

    return (trunk(x_prompt), trunk(x_sample))
```

```python
import functools

import numpy as np
import jax
import jax.numpy as jnp
from jax import lax
from jax.experimental import pallas as pl
from jax.experimental.pallas import tpu as pltpu

D_MODEL = 2048
DEPTH = 2
HEAD_DIM = 128
HEADS_PER_GROUP = 4
DILATED_GROUPS = ((128, 1), (512, 4), (2048, 16))
N_GROUPS = len(DILATED_GROUPS)
D_ATTN = N_GROUPS * HEADS_PER_GROUP * HEAD_DIM
D_GROUP = HEADS_PER_GROUP * HEAD_DIM
D_RNN = D_MODEL
N_RNN_BLOCKS = 16
RNN_BLOCK = D_RNN // N_RNN_BLOCKS
CONV_WIDTH = 4
CONV_LEFT = 2
LRU_C = 8.0
REL_BUCKETS = 32
REL_MAX_DIST = 1024
ALPHA = (2.0 * DEPTH) ** 0.25
LN_EPS = 1e-5
NEG_INF = -1e30

LANES = 128
SUBLANES = 8
VMEM_LIMIT_BYTES = 56 * 1024 * 1024

HALF = 64
Q_SUB = 128
K_WIN = Q_SUB + 2 * HALF
ATTN_TOKENS = 2048
ATTN_RESIDUES_PER_TRIP = 16

RNN_TILE = 512
RNN_SLAB = 1024
RADIX = 4
GATE_BIAS_ROWS = 3
LOG2_E = 1.4426950408889634

MM_TOKENS = RNN_TILE
MM_PIECE = 256

PROJ_TM = 2048
MERGE_TM = 256

GM_SPLIT = D_MODEL // 2
GATE1_COLS = D_RNN + GM_SPLIT
GATE2_COLS = (D_MODEL - GM_SPLIT) + D_MODEL + D_GROUP


def _params(*sem):
    return pltpu.CompilerParams(dimension_semantics=sem, vmem_limit_bytes=VMEM_LIMIT_BYTES)


def _silu_of_half(h):
    return h + h * jnp.tanh(h)


def _xr_proj_kernel(x_ref, w_ref, b_ref, o_ref):
    acc = jnp.dot(x_ref[...], w_ref[...], preferred_element_type=jnp.float32)
    o_ref[...] = acc + b_ref[...]


def _xr_proj(x_bf, w_bf, b):
    bsz, s, d = x_bf.shape
    n = w_bf.shape[1]
    tm, tn = PROJ_TM // 2, n
    return pl.pallas_call(
        _xr_proj_kernel,
        grid=(bsz, s // tm, n // tn),
        in_specs=[
            pl.BlockSpec((None, tm, d), lambda bb, i, j: (bb, i, 0)),
            pl.BlockSpec((d, tn), lambda bb, i, j: (0, j)),
            pl.BlockSpec((1, tn), lambda bb, i, j: (0, j)),
        ],
        out_specs=pl.BlockSpec((None, tm, tn), lambda bb, i, j: (bb, i, j)),
        out_shape=jax.ShapeDtypeStruct((bsz, s, n), jnp.float32),
        compiler_params=_params("parallel", "parallel", "arbitrary"),
        name="xr_proj",
    )(x_bf, w_bf, b)


def _qkv_proj_kernel(x_ref, w_ref, b_ref, o_ref, *rest, dil, tm, tn, emit_bf16):
    if emit_bf16:
        xbf_ref, rest = rest[0], rest[1:]

        @pl.when(pl.program_id(2) == 0)
        def _():
            xbf_ref[...] = x_ref[...].astype(jnp.bfloat16)

        x = xbf_ref[...]
    else:
        x = x_ref[...]
    acc = jnp.dot(x, w_ref[...], preferred_element_type=jnp.float32) + b_ref[...]
    if dil == 1:
        o_ref[0] = acc.astype(o_ref.dtype)
        return
    acc_s = rest[0]
    rows = tm // dil
    for c in range(tn // LANES):
        cs = slice(c * LANES, (c + 1) * LANES)
        acc_s[c] = acc[:, cs]
        if dil == RADIX:
            for r in range(dil):
                o_ref[r, :, cs] = acc_s[c, pl.ds(r, rows, stride=dil), :].astype(o_ref.dtype)
        else:
            assert dil == RADIX * RADIX
            mid_s = rest[1]
            for lo in range(RADIX):
                mid_s[lo] = acc_s[c, pl.ds(lo, tm // RADIX, stride=RADIX), :]
            for lo in range(RADIX):
                for hi in range(RADIX):
                    o_ref[lo + RADIX * hi, :, cs] = mid_s[lo, pl.ds(hi, rows, stride=RADIX), :].astype(o_ref.dtype)


def _qkv_proj(x, w_bf, b, dil, g):
    bsz, s, d = x.shape
    n = w_bf.shape[1]
    emit_bf16 = x.dtype != jnp.bfloat16
    tm, tn = PROJ_TM // 2, n
    scratch = [] if dil == 1 else [pltpu.VMEM((tn // LANES, tm, LANES), jnp.float32)]
    if dil > RADIX:
        scratch.append(pltpu.VMEM((RADIX, tm // RADIX, LANES), jnp.float32))
    out_specs = [pl.BlockSpec((None, dil, tm // dil, tn), lambda bb, i, j: (bb, 0, i, j))]
    out_shape = [jax.ShapeDtypeStruct((bsz, dil, s // dil, n), jnp.bfloat16)]
    if emit_bf16:
        out_specs.append(pl.BlockSpec((None, tm, d), lambda bb, i, j: (bb, i, 0)))
        out_shape.append(jax.ShapeDtypeStruct((bsz, s, d), jnp.bfloat16))
    outs = pl.pallas_call(
        functools.partial(_qkv_proj_kernel, dil=dil, tm=tm, tn=tn, emit_bf16=emit_bf16),
        grid=(bsz, s // tm, n // tn),
        in_specs=[
            pl.BlockSpec((None, tm, d), lambda bb, i, j: (bb, i, 0)),
            pl.BlockSpec((d, tn), lambda bb, i, j: (0, j)),
            pl.BlockSpec((1, tn), lambda bb, i, j: (0, j)),
        ],
        out_specs=out_specs,
        out_shape=out_shape,
        scratch_shapes=scratch,
        compiler_params=_params("parallel", "parallel", "arbitrary"),
        name=f"qkv_proj_g{g}",
    )(x, w_bf, b)
    return outs if emit_bf16 else outs[0]


def _t5_bucket(rel):
    nb = REL_BUCKETS // 2
    max_exact = nb // 2
    ret = (rel > 0).astype(np.int32) * nb
    n = np.abs(rel)
    large = max_exact + (np.log(np.maximum(n, max_exact) / max_exact)
                         / np.log(REL_MAX_DIST / max_exact) * (nb - max_exact)).astype(np.int32)
    large = np.minimum(large, nb - 1)
    return (ret + np.where(n < max_exact, n, large)).astype(np.int32)


def _band_bias(rel_bias, g, dil):
    tab = rel_bias[_t5_bucket(np.arange(-HALF, HALF + 1) * dil)][:, g * HEADS_PER_GROUP:(g + 1) * HEADS_PER_GROUP]
    tab = tab * (HEAD_DIM ** 0.5)
    period = Q_SUB + K_WIN + LANES
    vec = jnp.full((HEADS_PER_GROUP, period), NEG_INF, jnp.float32).at[:, :2 * HALF + 1].set(tab.T)
    mat = jnp.tile(vec, (1, Q_SUB))[:, :Q_SUB * (period - 1)].reshape(HEADS_PER_GROUP, Q_SUB, period - 1)
    return mat[:, :, :K_WIN]


def _attn_kernel(q_ref, kp_ref, kc_ref, kn_ref, vp_ref, vc_ref, vn_ref, bias_ref,
                 o_ref, lse_ref, kbuf, vbuf, *, dil, tile, length):
    i = pl.program_id(1)
    scale = HEAD_DIM ** -0.5

    def residue(r, slot):
        kbuf[slot, 0:HALF] = kp_ref[r]
        kbuf[slot, HALF:HALF + tile] = kc_ref[r]
        kbuf[slot, HALF + tile:] = kn_ref[r]
        vbuf[slot, 0:HALF] = vp_ref[r]
        vbuf[slot, HALF:HALF + tile] = vc_ref[r]
        vbuf[slot, HALF + tile:] = vn_ref[r]
        n_sub = tile // Q_SUB
        for sb in range(n_sub):
            r0 = sb * Q_SUB
            at_edge = sb == 0 or sb == n_sub - 1
            if at_edge:
                kpos = i * tile + (r0 - HALF) + lax.broadcasted_iota(jnp.int32, (Q_SUB, K_WIN), 1)
                in_seq = (kpos >= 0) & (kpos < length)
            if dil == 1:
                rows = slice(r0, r0 + Q_SUB)
            else:
                rows = pl.ds(r + r0 * dil, Q_SUB, stride=dil)
            for h in range(HEADS_PER_GROUP):
                cs = slice(h * HEAD_DIM, (h + 1) * HEAD_DIM)
                q = q_ref[r, r0:r0 + Q_SUB, cs]
                k = kbuf[slot, r0:r0 + K_WIN, cs]
                v = vbuf[slot, r0:r0 + K_WIN, cs]
                t = lax.dot_general(q, k, (((1,), (1,)), ((), ())),
                                    preferred_element_type=jnp.float32) + bias_ref[h]
                if at_edge:
                    t = jnp.where(in_seq, t, NEG_INF)
                mt = jnp.max(t, axis=-1, keepdims=True)
                p = jnp.exp2((t - mt) * (scale * LOG2_E))
                den = jnp.sum(p, axis=-1, keepdims=True)
                o = jnp.dot(p.astype(jnp.bfloat16), v, preferred_element_type=jnp.float32)
                o_ref[h, rows, :] = o / den
                lse_ref[h, rows, :] = jnp.broadcast_to(mt * scale + jnp.log(den), (Q_SUB, HEAD_DIM))

    per_trip = kbuf.shape[0]

    def trip(it, _):
        for slot in range(per_trip):
            residue(it * per_trip + slot, slot)
        return 0

    if dil == per_trip:
        trip(0, 0)
    else:
        lax.fori_loop(0, dil // per_trip, trip, 0)


def _attention_group(qkv, bias, g, dil):
    bsz, _, length, _ = qkv.shape
    s = length * dil
    tile = ATTN_TOKENS // dil
    per_trip = min(dil, ATTN_RESIDUES_PER_TRIP)
    hb = tile // HALF
    n_half_blocks = length // HALF

    def cur(part):
        return pl.BlockSpec((None, dil, tile, D_GROUP), lambda bb, i: (bb, 0, i, part))

    def prev(part):
        return pl.BlockSpec((None, dil, HALF, D_GROUP),
                            lambda bb, i: (bb, 0, jnp.maximum(i * hb - 1, 0), part))

    def nxt(part):
        return pl.BlockSpec((None, dil, HALF, D_GROUP),
                            lambda bb, i: (bb, 0, jnp.minimum((i + 1) * hb, n_half_blocks - 1), part))

    out_spec = pl.BlockSpec((None, HEADS_PER_GROUP, tile * dil, HEAD_DIM), lambda bb, i: (bb, 0, i, 0))
    out_sds = jax.ShapeDtypeStruct((bsz, HEADS_PER_GROUP, s, HEAD_DIM), jnp.float32)
    return pl.pallas_call(
        functools.partial(_attn_kernel, dil=dil, tile=tile, length=length),
        grid=(bsz, length // tile),
        in_specs=[cur(0), prev(1), cur(1), nxt(1), prev(2), cur(2), nxt(2),
                  pl.BlockSpec((HEADS_PER_GROUP, Q_SUB, K_WIN), lambda bb, i: (0, 0, 0))],
        out_specs=[out_spec, out_spec],
        out_shape=[out_sds, out_sds],
        scratch_shapes=[pltpu.VMEM((per_trip, tile + 2 * HALF, D_GROUP), jnp.bfloat16),
                        pltpu.VMEM((per_trip, tile + 2 * HALF, D_GROUP), jnp.bfloat16)],
        compiler_params=_params("parallel", "arbitrary"),
        name=f"attn_g{g}",
    )(qkv, qkv, qkv, qkv, qkv, qkv, qkv, bias)


def _vreg_scan(a, b, state, reverse):
    groups = a.shape[0] // SUBLANES
    a3 = a.reshape(groups, SUBLANES, LANES)
    b3 = b.reshape(groups, SUBLANES, LANES)
    sub = lax.broadcasted_iota(jnp.int32, (groups, SUBLANES, LANES), 1)
    for shift in (1, 2, 4):
        keep = (sub < SUBLANES - shift) if reverse else (sub >= shift)
        amount = SUBLANES - shift if reverse else shift
        a_sh = jnp.where(keep, pltpu.roll(a3, amount, axis=1), 1.0)
        b_sh = jnp.where(keep, pltpu.roll(b3, amount, axis=1), 0.0)
        b3 = a3 * b_sh + b3
        a3 = a3 * a_sh
    sub2 = lax.broadcasted_iota(jnp.int32, (SUBLANES, LANES), 0)
    entering = [None] * groups
    for gi in (range(groups - 1, -1, -1) if reverse else range(groups)):
        sb = jnp.broadcast_to(state, (SUBLANES, LANES))
        incl = b3[gi] + a3[gi] * sb
        if reverse:
            entering[gi] = jnp.where(sub2 == SUBLANES - 1, sb, pltpu.roll(incl, SUBLANES - 1, axis=0))
            state = incl[0:1]
        else:
            entering[gi] = jnp.where(sub2 == 0, sb, pltpu.roll(incl, 1, axis=0))
            state = incl[SUBLANES - 1:SUBLANES]
    return jnp.concatenate(entering, axis=0), state


def _radix_scan(av, bv, entering_fn, reverse):
    order = tuple(range(RADIX - 1, -1, -1)) if reverse else tuple(range(RADIX))
    h = [None] * RADIX
    p = [None] * RADIX
    h[order[0]], p[order[0]] = bv[order[0]], av[order[0]]
    for before, t in zip(order[:-1], order[1:]):
        h[t] = av[t] * h[before] + bv[t]
        p[t] = av[t] * p[before]
    entering, state = entering_fn(p[order[-1]], h[order[-1]])
    incl = [h[t] + p[t] * entering for t in range(RADIX)]
    excl = [None] * RADIX
    excl[order[0]] = entering
    for before, t in zip(order[:-1], order[1:]):
        excl[t] = incl[before]
    return incl, excl, state


def _rglru_kernel(*refs, tile, n_tiles, reverse):
    if reverse:
        (xm_ref, wm_ref, bm_ref, conv_ref, wg_ref, lam_ref, hf_ref, gr_ref,
         zm_ref, out_ref, sum_a, sum_b, ent_s, carry, ubuf) = refs
    else:
        (xm_ref, wm_ref, bm_ref, xp_ref, xc_ref, xn_ref, cw_ref, cb_ref, wg_ref, lam_ref,
         zm_ref, out_ref, conv_ref, ext, sum_a, sum_b, ent_s, carry) = refs

    def gate_proj_piece(k):
        cols = slice(k * MM_PIECE, (k + 1) * MM_PIECE)
        zm_ref[:, cols] = (jnp.dot(xm_ref[...], wm_ref[:, cols], preferred_element_type=jnp.float32)
                           + bm_ref[:, cols])

    n_pieces = zm_ref.shape[1] // MM_PIECE
    step = pl.program_id(2)
    ti = (n_tiles - 1 - step) if reverse else step
    quarter = tile // RADIX
    sixteenth = quarter // RADIX
    nblk = RNN_SLAB // RNN_BLOCK

    @pl.when(step == 0)
    def _():
        carry[...] = jnp.zeros_like(carry)

    lam = lam_ref[...]
    half_csp = (0.5 * LRU_C) * (jnp.maximum(-lam, 0.0) + jnp.log1p(jnp.exp(-jnp.abs(lam))))
    bias_taps = (lax.broadcasted_iota(jnp.int32, (tile, RNN_BLOCK), 1) < GATE_BIAS_ROWS).astype(jnp.bfloat16)

    assert n_pieces <= nblk
    for n in range(nblk):
        cs = slice(n * RNN_BLOCK, (n + 1) * RNN_BLOCK)
        if reverse:
            xc = conv_ref[:, cs]
        else:
            ext[n, 0:SUBLANES] = jnp.where(ti > 0, xp_ref[:, cs], 0.0)
            ext[n, SUBLANES:SUBLANES + tile] = xc_ref[:, cs]
            ext[n, SUBLANES + tile:] = jnp.where(ti < n_tiles - 1, xn_ref[:, cs], 0.0)
            taps = {u: ext[n, pl.ds(SUBLANES + u, quarter, stride=RADIX), :]
                    for u in range(-CONV_LEFT, RADIX + CONV_WIDTH - 1 - CONV_LEFT)}
            xc_t = []
            for t in range(RADIX):
                acc = cb_ref[:, cs]
                for j in range(CONV_WIDTH):
                    acc = acc + taps[t + j - CONV_LEFT] * cw_ref[j:j + 1, cs]
                xc_t.append(acc)
            xc = jnp.concatenate(xc_t, axis=0)
            conv_ref[:, cs] = xc
        lhs = jnp.concatenate([xc.astype(jnp.bfloat16), bias_taps], axis=1)
        z = jnp.dot(lhs, wg_ref[n], preferred_element_type=jnp.float32)
        if n < n_pieces:
            gate_proj_piece(n)
        th_r = jnp.tanh(z[:, :RNN_BLOCK])
        th_i = jnp.tanh(z[:, RNN_BLOCK:])
        hsp = half_csp[:, cs]
        neg_log_a = hsp + hsp * th_r
        a = jnp.exp2(neg_log_a * (-LOG2_E))
        half_xc = 0.5 * xc
        one_m_a2 = jnp.tanh(neg_log_a) * (1.0 + a * a)
        root = jnp.where(one_m_a2 > 0.0, one_m_a2 * lax.rsqrt(one_m_a2), 0.0)
        b = root * (half_xc + half_xc * th_i)
        av = [a[t * quarter:(t + 1) * quarter] for t in range(RADIX)]
        bv = [b[t * quarter:(t + 1) * quarter] for t in range(RADIX)]

        def level2(p, h, n=n, cs=cs):
            sum_a[...] = p
            sum_b[...] = h
            av2 = [sum_a[pl.ds(t, sixteenth, stride=RADIX), :] for t in range(RADIX)]
            bv2 = [sum_b[pl.ds(t, sixteenth, stride=RADIX), :] for t in range(RADIX)]
            _, excl2, state = _radix_scan(
                av2, bv2, lambda p3, h3: _vreg_scan(p3, h3, carry[:, cs], reverse), reverse)
            for t in range(RADIX):
                ent_s[pl.ds(t, sixteenth, stride=RADIX), :] = excl2[t]
            carry[:, cs] = state
            return ent_s[...], state

        incl, _, _ = _radix_scan(av, bv, level2, reverse)
        for t in range(RADIX):
            rows = slice(t * quarter, (t + 1) * quarter)
            if reverse:
                ubuf[n, pl.ds(t, quarter, stride=RADIX), :] = incl[t] + hf_ref[rows, cs]
            else:
                out_ref[rows, cs] = incl[t]

    if reverse:
        for n in range(nblk):
            cs = slice(n * RNN_BLOCK, (n + 1) * RNN_BLOCK)
            out_ref[:, cs] = (ubuf[n] * _silu_of_half(gr_ref[:, cs])).astype(out_ref.dtype)


def _rglru(xr, x_bf, w_m, b_m, conv_w, conv_b, wg, lam, reverse, h_fwd=None, gr_src=None):
    b, s, _ = xr.shape
    n_m = w_m.shape[1]
    tn_m = n_m // 2
    tile = RNN_TILE
    n_tiles = s // tile
    nslab = D_RNN // RNN_SLAB
    sub_per_tile = tile // SUBLANES
    n_sub = s // SUBLANES
    nblk = RNN_SLAB // RNN_BLOCK
    quarter = tile // RADIX

    def tidx(i):
        return (n_tiles - 1 - i) if reverse else i

    x_cur = pl.BlockSpec((None, tile, RNN_SLAB), lambda bb, c, i: (bb, tidx(i), c))
    x_prev = pl.BlockSpec((None, SUBLANES, RNN_SLAB),
                          lambda bb, c, i: (bb, jnp.maximum(tidx(i) * sub_per_tile - 1, 0), c))
    x_next = pl.BlockSpec((None, SUBLANES, RNN_SLAB),
                          lambda bb, c, i: (bb, jnp.minimum((tidx(i) + 1) * sub_per_tile, n_sub - 1), c))
    m_tiles = b * s // MM_TOKENS
    assert m_tiles * 2 == b * nslab * n_tiles

    def m_col(bb, c, i):
        return ((bb * nslab + c) * n_tiles + i) // m_tiles

    def m_row(bb, c, i):
        return ((bb * nslab + c) * n_tiles + i) % m_tiles

    time_tile = pl.BlockSpec((None, tile, RNN_SLAB), lambda bb, c, i: (bb, tidx(i), c))
    gate_w = pl.BlockSpec((nblk, 2 * RNN_BLOCK, 2 * RNN_BLOCK), lambda bb, c, i: (c, 0, 0))
    lam_spec = pl.BlockSpec((1, RNN_SLAB), lambda bb, c, i: (0, c))
    in_specs = [
        pl.BlockSpec((MM_TOKENS, D_MODEL), lambda bb, c, i: (m_row(bb, c, i), 0)),
        pl.BlockSpec((D_MODEL, tn_m), lambda bb, c, i: (0, m_col(bb, c, i))),
        pl.BlockSpec((1, tn_m), lambda bb, c, i: (0, m_col(bb, c, i))),
    ]
    args = [x_bf.reshape(b * s, D_MODEL), w_m, b_m]
    scan_scratch = [
        pltpu.VMEM((quarter, RNN_BLOCK), jnp.float32),
        pltpu.VMEM((quarter, RNN_BLOCK), jnp.float32),
        pltpu.VMEM((quarter, RNN_BLOCK), jnp.float32),
        pltpu.VMEM((1, RNN_SLAB), jnp.float32),
    ]
    out_specs = [pl.BlockSpec((MM_TOKENS, tn_m), lambda bb, c, i: (m_row(bb, c, i), m_col(bb, c, i))), time_tile]
    state_sds = jax.ShapeDtypeStruct((b, s, D_RNN), jnp.float32)
    if reverse:
        in_specs += [time_tile, gate_w, lam_spec, time_tile, time_tile]
        args += [xr, wg, lam, h_fwd, gr_src]
        scratch = scan_scratch + [pltpu.VMEM((nblk, tile, RNN_BLOCK), jnp.float32)]
        out_shape = [jax.ShapeDtypeStruct((b * s, n_m), jnp.float32),
                     jax.ShapeDtypeStruct((b, s, D_RNN), jnp.bfloat16)]
    else:
        in_specs += [x_prev, x_cur, x_next,
                     pl.BlockSpec((CONV_WIDTH, RNN_SLAB), lambda bb, c, i: (0, c)),
                     pl.BlockSpec((1, RNN_SLAB), lambda bb, c, i: (0, c)),
                     gate_w, lam_spec]
        args += [xr, xr, xr, conv_w, conv_b, wg, lam]
        scratch = [pltpu.VMEM((nblk, tile + 2 * SUBLANES, RNN_BLOCK), jnp.float32)] + scan_scratch
        out_specs.append(time_tile)
        out_shape = [jax.ShapeDtypeStruct((b * s, n_m), jnp.float32), state_sds, state_sds]
    outs = pl.pallas_call(
        functools.partial(_rglru_kernel, tile=tile, n_tiles=n_tiles, reverse=reverse),
        grid=(b, nslab, n_tiles),
        in_specs=in_specs,
        out_specs=out_specs,
        out_shape=out_shape,
        scratch_shapes=scratch,
        compiler_params=_params("arbitrary", "arbitrary", "arbitrary"),
        name="rglru_bwd" if reverse else "rglru_fwd",
    )(*args)
    return (outs[0].reshape(b, s, n_m),) + tuple(outs[1:])


def _merge_kernel(x_ref, o0_ref, o1_ref, o2_ref, l0_ref, l1_ref, l2_ref, ga_ref, ur_ref,
                  gma_lo_ref, gma_hi_ref, gmr_lo_ref, gmr_hi_ref,
                  wa_ref, wr_ref, wo_ref, lng_ref, lnb_ref, y_ref, ybf_ref):
    def heads(ref):
        return jnp.concatenate([ref[h] for h in range(HEADS_PER_GROUP)], axis=1)

    half_gma = jnp.concatenate([gma_lo_ref[...], gma_hi_ref[...]], axis=1)
    half_gmr = jnp.concatenate([gmr_lo_ref[...], gmr_hi_ref[...]], axis=1)

    l0, l1, l2 = heads(l0_ref), heads(l1_ref), heads(l2_ref)
    mx = jnp.maximum(jnp.maximum(l0, l1), l2)
    e0, e1, e2 = jnp.exp(l0 - mx), jnp.exp(l1 - mx), jnp.exp(l2 - mx)
    inv = 1.0 / (e0 + e1 + e2)
    oa = (e0 * inv) * heads(o0_ref) + (e1 * inv) * heads(o1_ref) + (e2 * inv) * heads(o2_ref)
    ua = (oa * _silu_of_half(ga_ref[...])).astype(jnp.bfloat16)
    ya = jnp.dot(ua, wa_ref[...], preferred_element_type=jnp.float32)
    yr = jnp.dot(ur_ref[...], wr_ref[...], preferred_element_type=jnp.float32)
    mixed = (0.5 + 0.5 * jnp.tanh(half_gma)) * ya + (0.5 + 0.5 * jnp.tanh(half_gmr)) * yr
    out = jnp.dot(mixed.astype(jnp.bfloat16), wo_ref[...], preferred_element_type=jnp.float32)
    v = ALPHA * x_ref[...] + out
    mu = jnp.mean(v, axis=-1, keepdims=True)
    vc = v - mu
    var = jnp.mean(vc * vc, axis=-1, keepdims=True)
    y = vc * lax.rsqrt(var + LN_EPS) * lng_ref[...] + lnb_ref[...]
    y_ref[...] = y
    ybf_ref[...] = y.astype(jnp.bfloat16)


def _merge_out(x, attn, z1, z2, u_r, wa, wr, wo, ln_g, ln_b):
    bsz, s, d = x.shape
    tm = MERGE_TM
    row = lambda w: pl.BlockSpec((None, tm, w), lambda bb, i: (bb, i, 0))
    head = pl.BlockSpec((None, HEADS_PER_GROUP, tm, HEAD_DIM), lambda bb, i: (bb, 0, i, 0))
    const = lambda shp: pl.BlockSpec(shp, lambda bb, i: (0, 0), pipeline_mode=pl.Buffered(1))
    half = lambda j: pl.BlockSpec((None, tm, GM_SPLIT), lambda bb, i: (bb, i, j))
    in_specs = [
        row(d),
        head, head, head, head, head, head,
        pl.BlockSpec((None, tm, D_GROUP), lambda bb, i: (bb, i, (GATE2_COLS - D_GROUP) // D_GROUP)),
        row(D_RNN),
        half(D_RNN // GM_SPLIT),
        half(0),
        half(1), half(2),
        const((D_GROUP, D_MODEL)), const((D_RNN, D_MODEL)), const((D_MODEL, D_MODEL)),
        const((1, D_MODEL)), const((1, D_MODEL)),
    ]
    (o0, l0), (o1, l1), (o2, l2) = attn
    return pl.pallas_call(
        _merge_kernel,
        grid=(bsz, s // tm),
        in_specs=in_specs,
        out_specs=[row(d), row(d)],
        out_shape=[jax.ShapeDtypeStruct((bsz, s, d), jnp.float32), jax.ShapeDtypeStruct((bsz, s, d), jnp.bfloat16)],
        compiler_params=_params("parallel", "parallel"),
        name="merge_out",
    )(x, o0, o1, o2, l0, l1, l2, z2, u_r, z1, z2, z2, z2, wa, wr, wo, ln_g, ln_b)


def _prep_layer(l, w_in, b_in, conv_w, conv_b, lru_w, lru_b, lru_lam, w_attn_o, w_rnn_o, w_out, ln_g, ln_b):
    bf = jnp.bfloat16
    a3 = 3 * D_ATTN
    c_ga, c_xr, c_gr, c_gm = a3, a3 + D_GROUP, a3 + D_GROUP + D_RNN, a3 + D_GROUP + 2 * D_RNN

    def gate1_cols(w):
        return w[..., c_gr:c_gm + GM_SPLIT]

    def gate2_cols(w):
        return jnp.concatenate([w[..., c_gm + GM_SPLIT:], w[..., c_ga:c_xr]], axis=-1)

    def group_cols(w, g):
        return jnp.concatenate([w[..., part * D_ATTN + g * D_GROUP:part * D_ATTN + (g + 1) * D_GROUP]
                                for part in range(3)], axis=-1)

    def gate_w(direction):
        w = 0.5 * jnp.concatenate([lru_w[l, direction, 0], lru_w[l, direction, 1]], axis=-1)
        bias = 0.5 * jnp.concatenate([lru_b[l, direction, 0], lru_b[l, direction, 1]], axis=-1)
        terms, rem = [], bias
        for _ in range(GATE_BIAS_ROWS):
            term = rem.astype(bf)
            terms.append(term)
            rem = rem - term.astype(jnp.float32)
        pad = jnp.zeros((N_RNN_BLOCKS, RNN_BLOCK - GATE_BIAS_ROWS, 2 * RNN_BLOCK), bf)
        return jnp.concatenate([w.astype(bf), jnp.stack(terms, axis=1), pad], axis=1)

    return dict(
        w_qkv=[group_cols(w_in[l], g).astype(bf) for g in range(N_GROUPS)],
        b_qkv=[group_cols(b_in[l], g)[None, :] for g in range(N_GROUPS)],
        w_xr=w_in[l, :, c_xr:c_gr].astype(bf), b_xr=b_in[l, None, c_xr:c_gr],
        w_g1=(0.5 * gate1_cols(w_in[l])).astype(bf), b_g1=0.5 * gate1_cols(b_in[l])[None, :],
        w_g2=(0.5 * gate2_cols(w_in[l])).astype(bf), b_g2=0.5 * gate2_cols(b_in[l])[None, :],
        conv_w=conv_w[l], conv_b=conv_b[l][None, :],
        wg=[gate_w(0), gate_w(1)],
        lam=[lru_lam[l, 0][None, :], lru_lam[l, 1][None, :]],
        wa=w_attn_o[l].astype(bf), wr=w_rnn_o[l].astype(bf), wo=w_out[l].astype(bf),
        ln_g=ln_g[l][None, :], ln_b=ln_b[l][None, :],
    )


def _layer(x, x_bf, p, biases):
    attn = []
    for g, (_, dil) in enumerate(DILATED_GROUPS):
        if x_bf is None:
            qkv, x_bf = _qkv_proj(x, p["w_qkv"][g], p["b_qkv"][g], dil, g)
        else:
            qkv = _qkv_proj(x_bf, p["w_qkv"][g], p["b_qkv"][g], dil, g)
        attn.append(_attention_group(qkv, biases[g], g, dil))
    xr = _xr_proj(x_bf, p["w_xr"], p["b_xr"])
    z1, h_fwd, xconv = _rglru(xr, x_bf, p["w_g1"], p["b_g1"], p["conv_w"], p["conv_b"],
                              p["wg"][0], p["lam"][0], False)
    z2, u_r = _rglru(xconv, x_bf, p["w_g2"], p["b_g2"], None, None,
                     p["wg"][1], p["lam"][1], True, h_fwd, z1)
    return _merge_out(x, attn, z1, z2, u_r, p["wa"], p["wr"], p["wo"], p["ln_g"], p["ln_b"])


def kernel(x_prompt, x_sample, w_in, b_in, conv_w, conv_b, lru_w, lru_b, lru_lam,
           w_attn_o, w_rnn_o, w_out, ln_g, ln_b, rel_bias):
    layers = [_prep_layer(l, w_in, b_in, conv_w, conv_b, lru_w, lru_b, lru_lam,
                          w_attn_o, w_rnn_o, w_out, ln_g, ln_b) for l in range(DEPTH)]
    biases = [_band_bias(rel_bias, g, dil) for g, (_, dil) in enumerate(DILATED_GROUPS)]

    def trunk(x):
        x_bf = None
        for p in layers:
            x, x_bf = _layer(x, x_bf, p, biases)
        return x

    return (trunk(x_prompt), trunk(x_sample))
```

```python
import functools

import numpy as np
import jax
import jax.numpy as jnp
from jax import lax
from jax.experimental import pallas as pl
from jax.experimental.pallas import tpu as pltpu

D_MODEL = 2048
DEPTH = 2
HEAD_DIM = 128
HEADS_PER_GROUP = 4
DILATED_GROUPS = ((128, 1), (512, 4), (2048, 16))
N_GROUPS = len(DILATED_GROUPS)
D_ATTN = N_GROUPS * HEADS_PER_GROUP * HEAD_DIM
D_GROUP = HEADS_PER_GROUP * HEAD_DIM
D_RNN = D_MODEL
N_RNN_BLOCKS = 16
RNN_BLOCK = D_RNN // N_RNN_BLOCKS
CONV_WIDTH = 4
CONV_LEFT = 2
LRU_C = 8.0
REL_BUCKETS = 32
REL_MAX_DIST = 1024
ALPHA = (2.0 * DEPTH) ** 0.25
LN_EPS = 1e-5
NEG_INF = -1e30

LANES = 128
SUBLANES = 8
VMEM_LIMIT_BYTES = 56 * 1024 * 1024

HALF = 64
Q_SUB = 128
K_WIN = Q_SUB + 2 * HALF
ATTN_TOKENS = 2048
ATTN_RESIDUES_PER_TRIP = 16

RNN_TILE = 512
RNN_SLAB = 1024
RADIX = 4
GATE_BIAS_ROWS = 3
LOG2_E = 1.4426950408889634

MM_TOKENS = RNN_TILE
MM_PIECE = 256

PROJ_TM = 2048
MERGE_TM = 256

GM_SPLIT = D_MODEL // 2
GATE1_COLS = D_RNN + GM_SPLIT
GATE2_COLS = (D_MODEL - GM_SPLIT) + D_MODEL + D_GROUP


def _params(*sem):
    return pltpu.CompilerParams(dimension_semantics=sem, vmem_limit_bytes=VMEM_LIMIT_BYTES)


def _silu_of_half(h):
    return h + h * jnp.tanh(h)


def _xr_proj_kernel(x_ref, w_ref, b_ref, o_ref):
    acc = jnp.dot(x_ref[...], w_ref[...], preferred_element_type=jnp.float32)
    o_ref[...] = acc + b_ref[...]


def _xr_proj(x_bf, w_bf, b):
    bsz, s, d = x_bf.shape
    n = w_bf.shape[1]
    tm, tn = PROJ_TM // 2, n
    return pl.pallas_call(
        _xr_proj_kernel,
        grid=(bsz, s // tm, n // tn),
        in_specs=[
            pl.BlockSpec((None, tm, d), lambda bb, i, j: (bb, i, 0)),
            pl.BlockSpec((d, tn), lambda bb, i, j: (0, j)),
            pl.BlockSpec((1, tn), lambda bb, i, j: (0, j)),
        ],
        out_specs=pl.BlockSpec((None, tm, tn), lambda bb, i, j: (bb, i, j)),
        out_shape=jax.ShapeDtypeStruct((bsz, s, n), jnp.float32),
        compiler_params=_params("parallel", "parallel", "arbitrary"),
        name="xr_proj",
    )(x_bf, w_bf, b)


def _qkv_proj_kernel(x_ref, w_ref, b_ref, o_ref, *rest, dil, tm, tn, emit_bf16):
    if emit_bf16:
        xbf_ref, rest = rest[0], rest[1:]

        @pl.when(pl.program_id(2) == 0)
        def _():
            xbf_ref[...] = x_ref[...].astype(jnp.bfloat16)

        x = xbf_ref[...]
    else:
        x = x_ref[...]
    acc = jnp.dot(x, w_ref[...], preferred_element_type=jnp.float32) + b_ref[...]
    if dil == 1:
        o_ref[0] = acc.astype(o_ref.dtype)
        return
    acc_s = rest[0]
    rows = tm // dil
    for c in range(tn // LANES):
        cs = slice(c * LANES, (c + 1) * LANES)
        acc_s[c] = acc[:, cs]
        if dil == RADIX:
            for r in range(dil):
                o_ref[r, :, cs] = acc_s[c, pl.ds(r, rows, stride=dil), :].astype(o_ref.dtype)
        else:
            assert dil == RADIX * RADIX
            mid_s = rest[1]
            for lo in range(RADIX):
                mid_s[lo] = acc_s[c, pl.ds(lo, tm // RADIX, stride=RADIX), :]
            for lo in range(RADIX):
                for hi in range(RADIX):
                    o_ref[lo + RADIX * hi, :, cs] = mid_s[lo, pl.ds(hi, rows, stride=RADIX), :].astype(o_ref.dtype)


def _qkv_proj(x, w_bf, b, dil, g):
    bsz, s, d = x.shape
    n = w_bf.shape[1]
    emit_bf16 = x.dtype != jnp.bfloat16
    tm, tn = PROJ_TM // 2, n
    scratch = [] if dil == 1 else [pltpu.VMEM((tn // LANES, tm, LANES), jnp.float32)]
    if dil > RADIX:
        scratch.append(pltpu.VMEM((RADIX, tm // RADIX, LANES), jnp.float32))
    out_specs = [pl.BlockSpec((None, dil, tm // dil, tn), lambda bb, i, j: (bb, 0, i, j))]
    out_shape = [jax.ShapeDtypeStruct((bsz, dil, s // dil, n), jnp.bfloat16)]
    if emit_bf16:
        out_specs.append(pl.BlockSpec((None, tm, d), lambda bb, i, j: (bb, i, 0)))
        out_shape.append(jax.ShapeDtypeStruct((bsz, s, d), jnp.bfloat16))
    outs = pl.pallas_call(
        functools.partial(_qkv_proj_kernel, dil=dil, tm=tm, tn=tn, emit_bf16=emit_bf16),
        grid=(bsz, s // tm, n // tn),
        in_specs=[
            pl.BlockSpec((None, tm, d), lambda bb, i, j: (bb, i, 0)),
            pl.BlockSpec((d, tn), lambda bb, i, j: (0, j)),
            pl.BlockSpec((1, tn), lambda bb, i, j: (0, j)),
        ],
        out_specs=out_specs,
        out_shape=out_shape,
        scratch_shapes=scratch,
        compiler_params=_params("parallel", "parallel", "arbitrary"),
        name=f"qkv_proj_g{g}",
    )(x, w_bf, b)
    return outs if emit_bf16 else outs[0]


def _t5_bucket(rel):
    nb = REL_BUCKETS // 2
    max_exact = nb // 2
    ret = (rel > 0).astype(np.int32) * nb
    n = np.abs(rel)
    large = max_exact + (np.log(np.maximum(n, max_exact) / max_exact)
                         / np.log(REL_MAX_DIST / max_exact) * (nb - max_exact)).astype(np.int32)
    large = np.minimum(large, nb - 1)
    return (ret + np.where(n < max_exact, n, large)).astype(np.int32)


def _band_bias(rel_bias, g, dil):
    tab = rel_bias[_t5_bucket(np.arange(-HALF, HALF + 1) * dil)][:, g * HEADS_PER_GROUP:(g + 1) * HEADS_PER_GROUP]
    tab = tab * (HEAD_DIM ** 0.5)
    period = Q_SUB + K_WIN + LANES
    vec = jnp.full((HEADS_PER_GROUP, period), NEG_INF, jnp.float32).at[:, :2 * HALF + 1].set(tab.T)
    mat = jnp.tile(vec, (1, Q_SUB))[:, :Q_SUB * (period - 1)].reshape(HEADS_PER_GROUP, Q_SUB, period - 1)
    return mat[:, :, :K_WIN]


def _attn_kernel(q_ref, kp_ref, kc_ref, kn_ref, vp_ref, vc_ref, vn_ref, bias_ref,
                 o_ref, lse_ref, kbuf, vbuf, *, dil, tile, length):
    i = pl.program_id(1)
    scale = HEAD_DIM ** -0.5

    def residue(r, slot):
        kbuf[slot, 0:HALF] = kp_ref[r]
        kbuf[slot, HALF:HALF + tile] = kc_ref[r]
        kbuf[slot, HALF + tile:] = kn_ref[r]
        vbuf[slot, 0:HALF] = vp_ref[r]
        vbuf[slot, HALF:HALF + tile] = vc_ref[r]
        vbuf[slot, HALF + tile:] = vn_ref[r]
        n_sub = tile // Q_SUB
        for sb in range(n_sub):
            r0 = sb * Q_SUB
            at_edge = sb == 0 or sb == n_sub - 1
            if at_edge:
                kpos = i * tile + (r0 - HALF) + lax.broadcasted_iota(jnp.int32, (Q_SUB, K_WIN), 1)
                in_seq = (kpos >= 0) & (kpos < length)
            if dil == 1:
                rows = slice(r0, r0 + Q_SUB)
            else:
                rows = pl.ds(r + r0 * dil, Q_SUB, stride=dil)
            for h in range(HEADS_PER_GROUP):
                cs = slice(h * HEAD_DIM, (h + 1) * HEAD_DIM)
                q = q_ref[r, r0:r0 + Q_SUB, cs]
                k = kbuf[slot, r0:r0 + K_WIN, cs]
                v = vbuf[slot, r0:r0 + K_WIN, cs]
                t = lax.dot_general(q, k, (((1,), (1,)), ((), ())),
                                    preferred_element_type=jnp.float32) + bias_ref[h]
                if at_edge:
                    t = jnp.where(in_seq, t, NEG_INF)
                mt = jnp.max(t, axis=-1, keepdims=True)
                p = jnp.exp2((t - mt) * (scale * LOG2_E))
                den = jnp.sum(p, axis=-1, keepdims=True)
                o = jnp.dot(p.astype(jnp.bfloat16), v, preferred_element_type=jnp.float32)
                o_ref[h, rows, :] = o / den
                lse_ref[h, rows, :] = jnp.broadcast_to(mt * scale + jnp.log(den), (Q_SUB, HEAD_DIM))

    per_trip = kbuf.shape[0]

    def trip(it, _):
        for slot in range(per_trip):
            residue(it * per_trip + slot, slot)
        return 0

    if dil == per_trip:
        trip(0, 0)
    else:
        lax.fori_loop(0, dil // per_trip, trip, 0)


def _attention_group(qkv, bias, g, dil):
    bsz, _, length, _ = qkv.shape
    s = length * dil
    tile = ATTN_TOKENS // dil
    per_trip = min(dil, ATTN_RESIDUES_PER_TRIP)
    hb = tile // HALF
    n_half_blocks = length // HALF

    def cur(part):
        return pl.BlockSpec((None, dil, tile, D_GROUP), lambda bb, i: (bb, 0, i, part))

    def prev(part):
        return pl.BlockSpec((None, dil, HALF, D_GROUP),
                            lambda bb, i: (bb, 0, jnp.maximum(i * hb - 1, 0), part))

    def nxt(part):
        return pl.BlockSpec((None, dil, HALF, D_GROUP),
                            lambda bb, i: (bb, 0, jnp.minimum((i + 1) * hb, n_half_blocks - 1), part))

    out_spec = pl.BlockSpec((None, HEADS_PER_GROUP, tile * dil, HEAD_DIM), lambda bb, i: (bb, 0, i, 0))
    out_sds = jax.ShapeDtypeStruct((bsz, HEADS_PER_GROUP, s, HEAD_DIM), jnp.float32)
    return pl.pallas_call(
        functools.partial(_attn_kernel, dil=dil, tile=tile, length=length),
        grid=(bsz, length // tile),
        in_specs=[cur(0), prev(1), cur(1), nxt(1), prev(2), cur(2), nxt(2),
                  pl.BlockSpec((HEADS_PER_GROUP, Q_SUB, K_WIN), lambda bb, i: (0, 0, 0))],
        out_specs=[out_spec, out_spec],
        out_shape=[out_sds, out_sds],
        scratch_shapes=[pltpu.VMEM((per_trip, tile + 2 * HALF, D_GROUP), jnp.bfloat16),
                        pltpu.VMEM((per_trip, tile + 2 * HALF, D_GROUP), jnp.bfloat16)],
        compiler_params=_params("parallel", "arbitrary"),
        name=f"attn_g{g}",
    )(qkv, qkv, qkv, qkv, qkv, qkv, qkv, bias)


def _vreg_scan(a, b, state, reverse):
    groups = a.shape[0] // SUBLANES
    a3 = a.reshape(groups, SUBLANES, LANES)
    b3 = b.reshape(groups, SUBLANES, LANES)
    sub = lax.broadcasted_iota(jnp.int32, (groups, SUBLANES, LANES), 1)
    for shift in (1, 2, 4):
        keep = (sub < SUBLANES - shift) if reverse else (sub >= shift)
        amount = SUBLANES - shift if reverse else shift
        a_sh = jnp.where(keep, pltpu.roll(a3, amount, axis=1), 1.0)
        b_sh = jnp.where(keep, pltpu.roll(b3, amount, axis=1), 0.0)
        b3 = a3 * b_sh + b3
        a3 = a3 * a_sh
    sub2 = lax.broadcasted_iota(jnp.int32, (SUBLANES, LANES), 0)
    entering = [None] * groups
    for gi in (range(groups - 1, -1, -1) if reverse else range(groups)):
        sb = jnp.broadcast_to(state, (SUBLANES, LANES))
        incl = b3[gi] + a3[gi] * sb
        if reverse:
            entering[gi] = jnp.where(sub2 == SUBLANES - 1, sb, pltpu.roll(incl, SUBLANES - 1, axis=0))
            state = incl[0:1]
        else:
            entering[gi] = jnp.where(sub2 == 0, sb, pltpu.roll(incl, 1, axis=0))
            state = incl[SUBLANES - 1:SUBLANES]
    return jnp.concatenate(entering, axis=0), state


def _radix_scan(av, bv, entering_fn, reverse):
    order = tuple(range(RADIX - 1, -1, -1)) if reverse else tuple(range(RADIX))
    h = [None] * RADIX
    p = [None] * RADIX
    h[order[0]], p[order[0]] = bv[order[0]], av[order[0]]
    for before, t in zip(order[:-1], order[1:]):
        h[t] = av[t] * h[before] + bv[t]
        p[t] = av[t] * p[before]
    entering, state = entering_fn(p[order[-1]], h[order[-1]])
    incl = [h[t] + p[t] * entering for t in range(RADIX)]
    excl = [None] * RADIX
    excl[order[0]] = entering
    for before, t in zip(order[:-1], order[1:]):
        excl[t] = incl[before]
    return incl, excl, state


def _rglru_kernel(*refs, tile, n_tiles, reverse):
    if reverse:
        (xm_ref, wm_ref, bm_ref, conv_ref, wg_ref, lam_ref, hf_ref, gr_ref,
         zm_ref, out_ref, sum_a, sum_b, ent_s, carry, ubuf) = refs
    else:
        (xm_ref, wm_ref, bm_ref, xp_ref, xc_ref, xn_ref, cw_ref, cb_ref, wg_ref, lam_ref,
         zm_ref, out_ref, conv_ref, ext, sum_a, sum_b, ent_s, carry) = refs

    def gate_proj_piece(k):
        cols = slice(k * MM_PIECE, (k + 1) * MM_PIECE)
        zm_ref[:, cols] = (jnp.dot(xm_ref[...], wm_ref[:, cols], preferred_element_type=jnp.float32)
                           + bm_ref[:, cols])

    n_pieces = zm_ref.shape[1] // MM_PIECE
    step = pl.program_id(2)
    ti = (n_tiles - 1 - step) if reverse else step
    quarter = tile // RADIX
    sixteenth = quarter // RADIX
    nblk = RNN_SLAB // RNN_BLOCK

    @pl.when(step == 0)
    def _():
        carry[...] = jnp.zeros_like(carry)

    lam = lam_ref[...]
    half_csp = (0.5 * LRU_C) * (jnp.maximum(-lam, 0.0) + jnp.log1p(jnp.exp(-jnp.abs(lam))))
    bias_taps = (lax.broadcasted_iota(jnp.int32, (tile, RNN_BLOCK), 1) < GATE_BIAS_ROWS).astype(jnp.bfloat16)

    assert n_pieces <= nblk
    for n in range(nblk):
        cs = slice(n * RNN_BLOCK, (n + 1) * RNN_BLOCK)
        if reverse:
            xc = conv_ref[:, cs]
        else:
            ext[n, 0:SUBLANES] = jnp.where(ti > 0, xp_ref[:, cs], 0.0)
            ext[n, SUBLANES:SUBLANES + tile] = xc_ref[:, cs]
            ext[n, SUBLANES + tile:] = jnp.where(ti < n_tiles - 1, xn_ref[:, cs], 0.0)
            taps = {u: ext[n, pl.ds(SUBLANES + u, quarter, stride=RADIX), :]
                    for u in range(-CONV_LEFT, RADIX + CONV_WIDTH - 1 - CONV_LEFT)}
            xc_t = []
            for t in range(RADIX):
                acc = cb_ref[:, cs]
                for j in range(CONV_WIDTH):
                    acc = acc + taps[t + j - CONV_LEFT] * cw_ref[j:j + 1, cs]
                xc_t.append(acc)
            xc = jnp.concatenate(xc_t, axis=0)
            conv_ref[:, cs] = xc
        lhs = jnp.concatenate([xc.astype(jnp.bfloat16), bias_taps], axis=1)
        z = jnp.dot(lhs, wg_ref[n], preferred_element_type=jnp.float32)
        if n < n_pieces:
            gate_proj_piece(n)
        th_r = jnp.tanh(z[:, :RNN_BLOCK])
        th_i = jnp.tanh(z[:, RNN_BLOCK:])
        hsp = half_csp[:, cs]
        neg_log_a = hsp + hsp * th_r
        a = jnp.exp2(neg_log_a * (-LOG2_E))
        half_xc = 0.5 * xc
        one_m_a2 = jnp.tanh(neg_log_a) * (1.0 + a * a)
        root = jnp.where(one_m_a2 > 0.0, one_m_a2 * lax.rsqrt(one_m_a2), 0.0)
        b = root * (half_xc + half_xc * th_i)
        av = [a[t * quarter:(t + 1) * quarter] for t in range(RADIX)]
        bv = [b[t * quarter:(t + 1) * quarter] for t in range(RADIX)]

        def level2(p, h, n=n, cs=cs):
            sum_a[...] = p
            sum_b[...] = h
            av2 = [sum_a[pl.ds(t, sixteenth, stride=RADIX), :] for t in range(RADIX)]
            bv2 = [sum_b[pl.ds(t, sixteenth, stride=RADIX), :] for t in range(RADIX)]
            _, excl2, state = _radix_scan(
                av2, bv2, lambda p3, h3: _vreg_scan(p3, h3, carry[:, cs], reverse), reverse)
            for t in range(RADIX):
                ent_s[pl.ds(t, sixteenth, stride=RADIX), :] = excl2[t]
            carry[:, cs] = state
            return ent_s[...], state

        incl, _, _ = _radix_scan(av, bv, level2, reverse)
        for t in range(RADIX):
            rows = slice(t * quarter, (t + 1) * quarter)
            if reverse:
                ubuf[n, pl.ds(t, quarter, stride=RADIX), :] = incl[t] + hf_ref[rows, cs]
            else:
                out_ref[rows, cs] = incl[t]

    if reverse:
        for n in range(nblk):
            cs = slice(n * RNN_BLOCK, (n + 1) * RNN_BLOCK)
            out_ref[:, cs] = (ubuf[n] * _silu_of_half(gr_ref[:, cs])).astype(out_ref.dtype)


def _rglru(xr, x_bf, w_m, b_m, conv_w, conv_b, wg, lam, reverse, h_fwd=None, gr_src=None):
    b, s, _ = xr.shape
    n_m = w_m.shape[1]
    tn_m = n_m // 2
    tile = RNN_TILE
    n_tiles = s // tile
    nslab = D_RNN // RNN_SLAB
    sub_per_tile = tile // SUBLANES
    n_sub = s // SUBLANES
    nblk = RNN_SLAB // RNN_BLOCK
    quarter = tile // RADIX

    def tidx(i):
        return (n_tiles - 1 - i) if reverse else i

    x_cur = pl.BlockSpec((None, tile, RNN_SLAB), lambda bb, c, i: (bb, tidx(i), c))
    x_prev = pl.BlockSpec((None, SUBLANES, RNN_SLAB),
                          lambda bb, c, i: (bb, jnp.maximum(tidx(i) * sub_per_tile - 1, 0), c))
    x_next = pl.BlockSpec((None, SUBLANES, RNN_SLAB),
                          lambda bb, c, i: (bb, jnp.minimum((tidx(i) + 1) * sub_per_tile, n_sub - 1), c))
    m_tiles = b * s // MM_TOKENS
    assert m_tiles * 2 == b * nslab * n_tiles

    def m_col(bb, c, i):
        return ((bb * nslab + c) * n_tiles + i) // m_tiles

    def m_row(bb, c, i):
        return ((bb * nslab + c) * n_tiles + i) % m_tiles

    time_tile = pl.BlockSpec((None, tile, RNN_SLAB), lambda bb, c, i: (bb, tidx(i), c))
    gate_w = pl.BlockSpec((nblk, 2 * RNN_BLOCK, 2 * RNN_BLOCK), lambda bb, c, i: (c, 0, 0))
    lam_spec = pl.BlockSpec((1, RNN_SLAB), lambda bb, c, i: (0, c))
    in_specs = [
        pl.BlockSpec((MM_TOKENS, D_MODEL), lambda bb, c, i: (m_row(bb, c, i), 0)),
        pl.BlockSpec((D_MODEL, tn_m), lambda bb, c, i: (0, m_col(bb, c, i))),
        pl.BlockSpec((1, tn_m), lambda bb, c, i: (0, m_col(bb, c, i))),
    ]
    args = [x_bf.reshape(b * s, D_MODEL), w_m, b_m]
    scan_scratch = [
        pltpu.VMEM((quarter, RNN_BLOCK), jnp.float32),
        pltpu.VMEM((quarter, RNN_BLOCK), jnp.float32),
        pltpu.VMEM((quarter, RNN_BLOCK), jnp.float32),
        pltpu.VMEM((1, RNN_SLAB), jnp.float32),
    ]
    out_specs = [pl.BlockSpec((MM_TOKENS, tn_m), lambda bb, c, i: (m_row(bb, c, i), m_col(bb, c, i))), time_tile]
    state_sds = jax.ShapeDtypeStruct((b, s, D_RNN), jnp.float32)
    if reverse:
        in_specs += [time_tile, gate_w, lam_spec, time_tile, time_tile]
        args += [xr, wg, lam, h_fwd, gr_src]
        scratch = scan_scratch + [pltpu.VMEM((nblk, tile, RNN_BLOCK), jnp.float32)]
        out_shape = [jax.ShapeDtypeStruct((b * s, n_m), jnp.float32),
                     jax.ShapeDtypeStruct((b, s, D_RNN), jnp.bfloat16)]
    else:
        in_specs += [x_prev, x_cur, x_next,
                     pl.BlockSpec((CONV_WIDTH, RNN_SLAB), lambda bb, c, i: (0, c)),
                     pl.BlockSpec((1, RNN_SLAB), lambda bb, c, i: (0, c)),
                     gate_w, lam_spec]
        args += [xr, xr, xr, conv_w, conv_b, wg, lam]
        scratch = [pltpu.VMEM((nblk, tile + 2 * SUBLANES, RNN_BLOCK), jnp.float32)] + scan_scratch
        out_specs.append(time_tile)
        out_shape = [jax.ShapeDtypeStruct((b * s, n_m), jnp.float32), state_sds, state_sds]
    outs = pl.pallas_call(
        functools.partial(_rglru_kernel, tile=tile, n_tiles=n_tiles, reverse=reverse),
        grid=(b, nslab, n_tiles),
        in_specs=in_specs,
        out_specs=out_specs,
        out_shape=out_shape,
        scratch_shapes=scratch,
        compiler_params=_params("arbitrary", "arbitrary", "arbitrary"),
        name="rglru_bwd" if reverse else "rglru_fwd",
    )(*args)
    return (outs[0].reshape(b, s, n_m),) + tuple(outs[1:])


def _merge_kernel(x_ref, o0_ref, o1_ref, o2_ref, l0_ref, l1_ref, l2_ref, ga_ref, ur_ref,
                  gma_lo_ref, gma_hi_ref, gmr_lo_ref, gmr_hi_ref,
                  wa_ref, wr_ref, wo_ref, lng_ref, lnb_ref, y_ref, ybf_ref):
    def heads(ref):
        return jnp.concatenate([ref[h] for h in range(HEADS_PER_GROUP)], axis=1)

    half_gma = jnp.concatenate([gma_lo_ref[...], gma_hi_ref[...]], axis=1)
    half_gmr = jnp.concatenate([gmr_lo_ref[...], gmr_hi_ref[...]], axis=1)

    l0, l1, l2 = heads(l0_ref), heads(l1_ref), heads(l2_ref)
    mx = jnp.maximum(jnp.maximum(l0, l1), l2)
    e0, e1, e2 = jnp.exp(l0 - mx), jnp.exp(l1 - mx), jnp.exp(l2 - mx)
    inv = 1.0 / (e0 + e1 + e2)
    oa = (e0 * inv) * heads(o0_ref) + (e1 * inv) * heads(o1_ref) + (e2 * inv) * heads(o2_ref)
    ua = (oa * _silu_of_half(ga_ref[...])).astype(jnp.bfloat16)
    ya = jnp.dot(ua, wa_ref[...], preferred_element_type=jnp.float32)
    yr = jnp.dot(ur_ref[...], wr_ref[...], preferred_element_type=jnp.float32)
    mixed = (0.5 + 0.5 * jnp.tanh(half_gma)) * ya + (0.5 + 0.5 * jnp.tanh(half_gmr)) * yr
    out = jnp.dot(mixed.astype(jnp.bfloat16), wo_ref[...], preferred_element_type=jnp.float32)
    v = ALPHA * x_ref[...] + out
    mu = jnp.mean(v, axis=-1, keepdims=True)
    vc = v - mu
    var = jnp.mean(vc * vc, axis=-1, keepdims=True)
    y = vc * lax.rsqrt(var + LN_EPS) * lng_ref[...] + lnb_ref[...]
    y_ref[...] = y
    ybf_ref[...] = y.astype(jnp.bfloat16)


def _merge_out(x, attn, z1, z2, u_r, wa, wr, wo, ln_g, ln_b):
    bsz, s, d = x.shape
    tm = MERGE_TM
    row = lambda w: pl.BlockSpec((None, tm, w), lambda bb, i: (bb, i, 0))
    head = pl.BlockSpec((None, HEADS_PER_GROUP, tm, HEAD_DIM), lambda bb, i: (bb, 0, i, 0))
    const = lambda shp: pl.BlockSpec(shp, lambda bb, i: (0, 0), pipeline_mode=pl.Buffered(1))
    half = lambda j: pl.BlockSpec((None, tm, GM_SPLIT), lambda bb, i: (bb, i, j))
    in_specs = [
        row(d),
        head, head, head, head, head, head,
        pl.BlockSpec((None, tm, D_GROUP), lambda bb, i: (bb, i, (GATE2_COLS - D_GROUP) // D_GROUP)),
        row(D_RNN),
        half(D_RNN // GM_SPLIT),
        half(0),
        half(1), half(2),
        const((D_GROUP, D_MODEL)), const((D_RNN, D_MODEL)), const((D_MODEL, D_MODEL)),
        const((1, D_MODEL)), const((1, D_MODEL)),
    ]
    (o0, l0), (o1, l1), (o2, l2) = attn
    return pl.pallas_call(
        _merge_kernel,
        grid=(bsz, s // tm),
        in_specs=in_specs,
        out_specs=[row(d), row(d)],
        out_shape=[jax.ShapeDtypeStruct((bsz, s, d), jnp.float32), jax.ShapeDtypeStruct((bsz, s, d), jnp.bfloat16)],
        compiler_params=_params("parallel", "parallel"),
        name="merge_out",
    )(x, o0, o1, o2, l0, l1, l2, z2, u_r, z1, z2, z2, z2, wa, wr, wo, ln_g, ln_b)


def _prep_layer(l, w_in, b_in, conv_w, conv_b, lru_w, lru_b, lru_lam, w_attn_o, w_rnn_o, w_out, ln_g, ln_b):
    bf = jnp.bfloat16
    half_bf = jnp.asarray(0.5, bf)
    a3 = 3 * D_ATTN
    c_ga, c_xr, c_gr, c_gm = a3, a3 + D_GROUP, a3 + D_GROUP + D_RNN, a3 + D_GROUP + 2 * D_RNN

    def gate1_cols(w):
        return w[..., c_gr:c_gm + GM_SPLIT]

    def gate2_cols(w):
        return jnp.concatenate([w[..., c_gm + GM_SPLIT:], w[..., c_ga:c_xr]], axis=-1)

    def group_cols(w, g):
        return jnp.concatenate([w[..., part * D_ATTN + g * D_GROUP:part * D_ATTN + (g + 1) * D_GROUP]
                                for part in range(3)], axis=-1)

    def gate_w(direction):
        w = 0.5 * jnp.concatenate([lru_w[l, direction, 0], lru_w[l, direction, 1]], axis=-1)
        bias = 0.5 * jnp.concatenate([lru_b[l, direction, 0], lru_b[l, direction, 1]], axis=-1)
        terms, rem = [], bias
        for _ in range(GATE_BIAS_ROWS):
            term = rem.astype(bf)
            terms.append(term)
            rem = rem - term.astype(jnp.float32)
        pad = jnp.zeros((N_RNN_BLOCKS, RNN_BLOCK - GATE_BIAS_ROWS, 2 * RNN_BLOCK), bf)
        return jnp.concatenate([w.astype(bf), jnp.stack(terms, axis=1), pad], axis=1)

    return dict(
        w_qkv=[group_cols(w_in[l], g).astype(bf) for g in range(N_GROUPS)],
        b_qkv=[group_cols(b_in[l], g)[None, :] for g in range(N_GROUPS)],
        w_xr=w_in[l, :, c_xr:c_gr].astype(bf), b_xr=b_in[l, None, c_xr:c_gr],
        w_g1=gate1_cols(w_in[l]).astype(bf) * half_bf, b_g1=0.5 * gate1_cols(b_in[l])[None, :],
        w_g2=gate2_cols(w_in[l]).astype(bf) * half_bf, b_g2=0.5 * gate2_cols(b_in[l])[None, :],
        conv_w=conv_w[l], conv_b=conv_b[l][None, :],
        wg=[gate_w(0), gate_w(1)],
        lam=[lru_lam[l, 0][None, :], lru_lam[l, 1][None, :]],
        wa=w_attn_o[l].astype(bf), wr=w_rnn_o[l].astype(bf), wo=w_out[l].astype(bf),
        ln_g=ln_g[l][None, :], ln_b=ln_b[l][None, :],
    )


def _layer(x, x_bf, p, biases):
    attn = []
    for g, (_, dil) in enumerate(DILATED_GROUPS):
        if x_bf is None:
            qkv, x_bf = _qkv_proj(x, p["w_qkv"][g], p["b_qkv"][g], dil, g)
        else:
            qkv = _qkv_proj(x_bf, p["w_qkv"][g], p["b_qkv"][g], dil, g)
        attn.append(_attention_group(qkv, biases[g], g, dil))
    xr = _xr_proj(x_bf, p["w_xr"], p["b_xr"])
    z1, h_fwd, xconv = _rglru(xr, x_bf, p["w_g1"], p["b_g1"], p["conv_w"], p["conv_b"],
                              p["wg"][0], p["lam"][0], False)
    z2, u_r = _rglru(xconv, x_bf, p["w_g2"], p["b_g2"], None, None,
                     p["wg"][1], p["lam"][1], True, h_fwd, z1)
    return _merge_out(x, attn, z1, z2, u_r, p["wa"], p["wr"], p["wo"], p["ln_g"], p["ln_b"])


def kernel(x_prompt, x_sample, w_in, b_in, conv_w, conv_b, lru_w, lru_b, lru_lam,
           w_attn_o, w_rnn_o, w_out, ln_g, ln_b, rel_bias):
    layers = [_prep_layer(l, w_in, b_in, conv_w, conv_b, lru_w, lru_b, lru_lam,
                          w_attn_o, w_rnn_o, w_out, ln_g, ln_b) for l in range(DEPTH)]
    biases = [_band_bias(rel_bias, g, dil) for g, (_, dil) in enumerate(DILATED_GROUPS)]

    def trunk(x):
        x_bf = None
        for p in layers:
            x, x_bf = _layer(x, x_bf, p, biases)
        return x

    return (trunk(x_prompt), trunk(x_sample))
```

```python
import functools

import numpy as np
import jax
import jax.numpy as jnp
from jax import lax
from jax.experimental import pallas as pl
from jax.experimental.pallas import tpu as pltpu

D_MODEL = 2048
DEPTH = 2
HEAD_DIM = 128
HEADS_PER_GROUP = 4
DILATED_GROUPS = ((128, 1), (512, 4), (2048, 16))
N_GROUPS = len(DILATED_GROUPS)
D_ATTN = N_GROUPS * HEADS_PER_GROUP * HEAD_DIM
D_GROUP = HEADS_PER_GROUP * HEAD_DIM
D_RNN = D_MODEL
N_RNN_BLOCKS = 16
RNN_BLOCK = D_RNN // N_RNN_BLOCKS
CONV_WIDTH = 4
CONV_LEFT = 2
LRU_C = 8.0
REL_BUCKETS = 32
REL_MAX_DIST = 1024
ALPHA = (2.0 * DEPTH) ** 0.25
LN_EPS = 1e-5
NEG_INF = -1e30

LANES = 128
SUBLANES = 8
VMEM_LIMIT_BYTES = 56 * 1024 * 1024

HALF = 64
Q_SUB = 128
K_WIN = Q_SUB + 2 * HALF
ATTN_TOKENS = 2048
ATTN_RESIDUES_PER_TRIP = 16

RNN_TILE = 512
RNN_SLAB = 1024
RADIX = 4
GATE_BIAS_ROWS = 3
LOG2_E = 1.4426950408889634

MM_TOKENS = RNN_TILE
MM_PIECE = 256

PROJ_TM = 2048
MERGE_TM = 256

GM_SPLIT = D_MODEL // 2
GATE1_COLS = D_RNN + GM_SPLIT
GATE2_COLS = (D_MODEL - GM_SPLIT) + D_MODEL + D_GROUP


def _params(*sem):
    return pltpu.CompilerParams(dimension_semantics=sem, vmem_limit_bytes=VMEM_LIMIT_BYTES)


def _silu_of_half(h):
    return h + h * jnp.tanh(h)


def _xr_proj_kernel(x_ref, w_ref, b_ref, o_ref):
    acc = jnp.dot(x_ref[...], w_ref[...], preferred_element_type=jnp.float32)
    o_ref[...] = acc + b_ref[...]


def _xr_proj(x_bf, w_bf, b):
    bsz, s, d = x_bf.shape
    n = w_bf.shape[1]
    tm, tn = PROJ_TM // 2, n
    return pl.pallas_call(
        _xr_proj_kernel,
        grid=(bsz, s // tm, n // tn),
        in_specs=[
            pl.BlockSpec((None, tm, d), lambda bb, i, j: (bb, i, 0)),
            pl.BlockSpec((d, tn), lambda bb, i, j: (0, j)),
            pl.BlockSpec((1, tn), lambda bb, i, j: (0, j)),
        ],
        out_specs=pl.BlockSpec((None, tm, tn), lambda bb, i, j: (bb, i, j)),
        out_shape=jax.ShapeDtypeStruct((bsz, s, n), jnp.float32),
        compiler_params=_params("parallel", "parallel", "arbitrary"),
        name="xr_proj",
    )(x_bf, w_bf, b)


def _qkv_proj_kernel(x_ref, w_ref, b_ref, o_ref, *rest, dil, tm, tn, emit_bf16):
    if emit_bf16:
        xbf_ref, rest = rest[0], rest[1:]

        @pl.when(pl.program_id(2) == 0)
        def _():
            xbf_ref[...] = x_ref[...].astype(jnp.bfloat16)

        x = xbf_ref[...]
    else:
        x = x_ref[...]
    acc = jnp.dot(x, w_ref[...], preferred_element_type=jnp.float32) + b_ref[...]
    if dil == 1:
        o_ref[0] = acc.astype(o_ref.dtype)
        return
    acc_s = rest[0]
    rows = tm // dil
    for c in range(tn // LANES):
        cs = slice(c * LANES, (c + 1) * LANES)
        acc_s[c] = acc[:, cs]
        if dil == RADIX:
            for r in range(dil):
                o_ref[r, :, cs] = acc_s[c, pl.ds(r, rows, stride=dil), :].astype(o_ref.dtype)
        else:
            assert dil == RADIX * RADIX
            mid_s = rest[1]
            for lo in range(RADIX):
                mid_s[lo] = acc_s[c, pl.ds(lo, tm // RADIX, stride=RADIX), :]
            for lo in range(RADIX):
                for hi in range(RADIX):
                    o_ref[lo + RADIX * hi, :, cs] = mid_s[lo, pl.ds(hi, rows, stride=RADIX), :].astype(o_ref.dtype)


def _qkv_proj(x, w_bf, b, dil, g):
    bsz, s, d = x.shape
    n = w_bf.shape[1]
    emit_bf16 = x.dtype != jnp.bfloat16
    tm, tn = PROJ_TM // 2, n
    scratch = [] if dil == 1 else [pltpu.VMEM((tn // LANES, tm, LANES), jnp.float32)]
    if dil > RADIX:
        scratch.append(pltpu.VMEM((RADIX, tm // RADIX, LANES), jnp.float32))
    out_specs = [pl.BlockSpec((None, dil, tm // dil, tn), lambda bb, i, j: (bb, 0, i, j))]
    out_shape = [jax.ShapeDtypeStruct((bsz, dil, s // dil, n), jnp.bfloat16)]
    if emit_bf16:
        out_specs.append(pl.BlockSpec((None, tm, d), lambda bb, i, j: (bb, i, 0)))
        out_shape.append(jax.ShapeDtypeStruct((bsz, s, d), jnp.bfloat16))
    outs = pl.pallas_call(
        functools.partial(_qkv_proj_kernel, dil=dil, tm=tm, tn=tn, emit_bf16=emit_bf16),
        grid=(bsz, s // tm, n // tn),
        in_specs=[
            pl.BlockSpec((None, tm, d), lambda bb, i, j: (bb, i, 0)),
            pl.BlockSpec((d, tn), lambda bb, i, j: (0, j)),
            pl.BlockSpec((1, tn), lambda bb, i, j: (0, j)),
        ],
        out_specs=out_specs,
        out_shape=out_shape,
        scratch_shapes=scratch,
        compiler_params=_params("parallel", "parallel", "arbitrary"),
        name=f"qkv_proj_g{g}",
    )(x, w_bf, b)
    return outs if emit_bf16 else outs[0]


def _t5_bucket(rel):
    nb = REL_BUCKETS // 2
    max_exact = nb // 2
    ret = (rel > 0).astype(np.int32) * nb
    n = np.abs(rel)
    large = max_exact + (np.log(np.maximum(n, max_exact) / max_exact)
                         / np.log(REL_MAX_DIST / max_exact) * (nb - max_exact)).astype(np.int32)
    large = np.minimum(large, nb - 1)
    return (ret + np.where(n < max_exact, n, large)).astype(np.int32)


def _band_bias(rel_bias, g, dil):
    tab = rel_bias[_t5_bucket(np.arange(-HALF, HALF + 1) * dil)][:, g * HEADS_PER_GROUP:(g + 1) * HEADS_PER_GROUP]
    tab = tab * (HEAD_DIM ** 0.5)
    period = Q_SUB + K_WIN + LANES
    vec = jnp.full((HEADS_PER_GROUP, period), NEG_INF, jnp.float32).at[:, :2 * HALF + 1].set(tab.T)
    mat = jnp.tile(vec, (1, Q_SUB))[:, :Q_SUB * (period - 1)].reshape(HEADS_PER_GROUP, Q_SUB, period - 1)
    return mat[:, :, :K_WIN]


def _attn_kernel(q_ref, kp_ref, kc_ref, kn_ref, vp_ref, vc_ref, vn_ref, bias_ref,
                 o_ref, lse_ref, kbuf, vbuf, *, dil, tile, length):
    i = pl.program_id(1)
    scale = HEAD_DIM ** -0.5

    def residue(r, slot):
        kbuf[slot, 0:HALF] = kp_ref[r]
        kbuf[slot, HALF:HALF + tile] = kc_ref[r]
        kbuf[slot, HALF + tile:] = kn_ref[r]
        vbuf[slot, 0:HALF] = vp_ref[r]
        vbuf[slot, HALF:HALF + tile] = vc_ref[r]
        vbuf[slot, HALF + tile:] = vn_ref[r]
        n_sub = tile // Q_SUB
        for sb in range(n_sub):
            r0 = sb * Q_SUB
            at_edge = sb == 0 or sb == n_sub - 1
            if at_edge:
                kpos = i * tile + (r0 - HALF) + lax.broadcasted_iota(jnp.int32, (Q_SUB, K_WIN), 1)
                in_seq = (kpos >= 0) & (kpos < length)
            if dil == 1:
                rows = slice(r0, r0 + Q_SUB)
            else:
                rows = pl.ds(r + r0 * dil, Q_SUB, stride=dil)
            for h in range(HEADS_PER_GROUP):
                cs = slice(h * HEAD_DIM, (h + 1) * HEAD_DIM)
                q = q_ref[r, r0:r0 + Q_SUB, cs]
                k = kbuf[slot, r0:r0 + K_WIN, cs]
                v = vbuf[slot, r0:r0 + K_WIN, cs]
                t = lax.dot_general(q, k, (((1,), (1,)), ((), ())),
                                    preferred_element_type=jnp.float32) + bias_ref[h]
                if at_edge:
                    t = jnp.where(in_seq, t, NEG_INF)
                mt = jnp.max(t, axis=-1, keepdims=True)
                p = jnp.exp2((t - mt) * (scale * LOG2_E))
                den = jnp.sum(p, axis=-1, keepdims=True)
                o = jnp.dot(p.astype(jnp.bfloat16), v, preferred_element_type=jnp.float32)
                o_ref[h, rows, :] = o / den
                lse_ref[h, rows, :] = jnp.broadcast_to(mt * scale + jnp.log(den), (Q_SUB, HEAD_DIM))

    per_trip = kbuf.shape[0]

    def trip(it, _):
        for slot in range(per_trip):
            residue(it * per_trip + slot, slot)
        return 0

    if dil == per_trip:
        trip(0, 0)
    else:
        lax.fori_loop(0, dil // per_trip, trip, 0)


def _attention_group(qkv, bias, g, dil):
    bsz, _, length, _ = qkv.shape
    s = length * dil
    tile = ATTN_TOKENS // dil
    per_trip = min(dil, ATTN_RESIDUES_PER_TRIP)
    hb = tile // HALF
    n_half_blocks = length // HALF

    def cur(part):
        return pl.BlockSpec((None, dil, tile, D_GROUP), lambda bb, i: (bb, 0, i, part))

    def prev(part):
        return pl.BlockSpec((None, dil, HALF, D_GROUP),
                            lambda bb, i: (bb, 0, jnp.maximum(i * hb - 1, 0), part))

    def nxt(part):
        return pl.BlockSpec((None, dil, HALF, D_GROUP),
                            lambda bb, i: (bb, 0, jnp.minimum((i + 1) * hb, n_half_blocks - 1), part))

    out_spec = pl.BlockSpec((None, HEADS_PER_GROUP, tile * dil, HEAD_DIM), lambda bb, i: (bb, 0, i, 0))
    out_sds = jax.ShapeDtypeStruct((bsz, HEADS_PER_GROUP, s, HEAD_DIM), jnp.float32)
    return pl.pallas_call(
        functools.partial(_attn_kernel, dil=dil, tile=tile, length=length),
        grid=(bsz, length // tile),
        in_specs=[cur(0), prev(1), cur(1), nxt(1), prev(2), cur(2), nxt(2),
                  pl.BlockSpec((HEADS_PER_GROUP, Q_SUB, K_WIN), lambda bb, i: (0, 0, 0))],
        out_specs=[out_spec, out_spec],
        out_shape=[out_sds, out_sds],
        scratch_shapes=[pltpu.VMEM((per_trip, tile + 2 * HALF, D_GROUP), jnp.bfloat16),
                        pltpu.VMEM((per_trip, tile + 2 * HALF, D_GROUP), jnp.bfloat16)],
        compiler_params=_params("parallel", "arbitrary"),
        name=f"attn_g{g}",
    )(qkv, qkv, qkv, qkv, qkv, qkv, qkv, bias)


def _vreg_scan(a, b, state, reverse):
    groups = a.shape[0] // SUBLANES
    a3 = a.reshape(groups, SUBLANES, LANES)
    b3 = b.reshape(groups, SUBLANES, LANES)
    sub = lax.broadcasted_iota(jnp.int32, (groups, SUBLANES, LANES), 1)
    for shift in (1, 2, 4):
        keep = (sub < SUBLANES - shift) if reverse else (sub >= shift)
        amount = SUBLANES - shift if reverse else shift
        a_sh = jnp.where(keep, pltpu.roll(a3, amount, axis=1), 1.0)
        b_sh = jnp.where(keep, pltpu.roll(b3, amount, axis=1), 0.0)
        b3 = a3 * b_sh + b3
        a3 = a3 * a_sh
    sub2 = lax.broadcasted_iota(jnp.int32, (SUBLANES, LANES), 0)
    entering = [None] * groups
    for gi in (range(groups - 1, -1, -1) if reverse else range(groups)):
        sb = jnp.broadcast_to(state, (SUBLANES, LANES))
        incl = b3[gi] + a3[gi] * sb
        if reverse:
            entering[gi] = jnp.where(sub2 == SUBLANES - 1, sb, pltpu.roll(incl, SUBLANES - 1, axis=0))
            state = incl[0:1]
        else:
            entering[gi] = jnp.where(sub2 == 0, sb, pltpu.roll(incl, 1, axis=0))
            state = incl[SUBLANES - 1:SUBLANES]
    return jnp.concatenate(entering, axis=0), state


def _radix_scan(av, bv, entering_fn, reverse):
    order = tuple(range(RADIX - 1, -1, -1)) if reverse else tuple(range(RADIX))
    h = [None] * RADIX
    p = [None] * RADIX
    h[order[0]], p[order[0]] = bv[order[0]], av[order[0]]
    for before, t in zip(order[:-1], order[1:]):
        h[t] = av[t] * h[before] + bv[t]
        p[t] = av[t] * p[before]
    entering, state = entering_fn(p[order[-1]], h[order[-1]])
    incl = [h[t] + p[t] * entering for t in range(RADIX)]
    excl = [None] * RADIX
    excl[order[0]] = entering
    for before, t in zip(order[:-1], order[1:]):
        excl[t] = incl[before]
    return incl, excl, state


def _rglru_kernel(*refs, tile, n_tiles, reverse):
    if reverse:
        (xm_ref, wm_ref, bm_ref, conv_ref, wg_ref, lam_ref, hf_ref, gr_ref,
         zm_ref, out_ref, sum_a, sum_b, ent_s, carry, ubuf) = refs
    else:
        (xm_ref, wm_ref, bm_ref, xp_ref, xc_ref, xn_ref, cw_ref, cb_ref, wg_ref, lam_ref,
         zm_ref, out_ref, conv_ref, ext, sum_a, sum_b, ent_s, carry) = refs

    def gate_proj_piece(k):
        cols = slice(k * MM_PIECE, (k + 1) * MM_PIECE)
        zm_ref[:, cols] = (jnp.dot(xm_ref[...], wm_ref[:, cols], preferred_element_type=jnp.float32)
                           + bm_ref[:, cols])

    n_pieces = zm_ref.shape[1] // MM_PIECE
    step = pl.program_id(2)
    ti = (n_tiles - 1 - step) if reverse else step
    quarter = tile // RADIX
    sixteenth = quarter // RADIX
    nblk = RNN_SLAB // RNN_BLOCK

    @pl.when(step == 0)
    def _():
        carry[...] = jnp.zeros_like(carry)

    lam = lam_ref[...]
    half_csp = (0.5 * LRU_C) * (jnp.maximum(-lam, 0.0) + jnp.log1p(jnp.exp(-jnp.abs(lam))))
    bias_taps = (lax.broadcasted_iota(jnp.int32, (tile, RNN_BLOCK), 1) < GATE_BIAS_ROWS).astype(jnp.bfloat16)

    assert n_pieces <= nblk
    for n in range(nblk):
        cs = slice(n * RNN_BLOCK, (n + 1) * RNN_BLOCK)
        if reverse:
            xc = conv_ref[:, cs]
        else:
            ext[n, 0:SUBLANES] = jnp.where(ti > 0, xp_ref[:, cs], 0.0)
            ext[n, SUBLANES:SUBLANES + tile] = xc_ref[:, cs]
            ext[n, SUBLANES + tile:] = jnp.where(ti < n_tiles - 1, xn_ref[:, cs], 0.0)
            taps = {u: ext[n, pl.ds(SUBLANES + u, quarter, stride=RADIX), :]
                    for u in range(-CONV_LEFT, RADIX + CONV_WIDTH - 1 - CONV_LEFT)}
            xc_t = []
            for t in range(RADIX):
                acc = cb_ref[:, cs]
                for j in range(CONV_WIDTH):
                    acc = acc + taps[t + j - CONV_LEFT] * cw_ref[j:j + 1, cs]
                xc_t.append(acc)
            xc = jnp.concatenate(xc_t, axis=0)
            conv_ref[:, cs] = xc
        lhs = jnp.concatenate([xc.astype(jnp.bfloat16), bias_taps], axis=1)
        z = jnp.dot(lhs, wg_ref[n], preferred_element_type=jnp.float32)
        if n < n_pieces:
            gate_proj_piece(n)
        th_r = jnp.tanh(z[:, :RNN_BLOCK])
        th_i = jnp.tanh(z[:, RNN_BLOCK:])
        hsp = half_csp[:, cs]
        neg_log_a = hsp + hsp * th_r
        a = jnp.exp2(neg_log_a * (-LOG2_E))
        half_xc = 0.5 * xc
        one_m_a2 = jnp.tanh(neg_log_a) * (1.0 + a * a)
        root = jnp.where(one_m_a2 > 0.0, one_m_a2 * lax.rsqrt(one_m_a2), 0.0)
        b = root * (half_xc + half_xc * th_i)
        av = [a[t * quarter:(t + 1) * quarter] for t in range(RADIX)]
        bv = [b[t * quarter:(t + 1) * quarter] for t in range(RADIX)]

        def level2(p, h, n=n, cs=cs):
            sum_a[...] = p
            sum_b[...] = h
            av2 = [sum_a[pl.ds(t, sixteenth, stride=RADIX), :] for t in range(RADIX)]
            bv2 = [sum_b[pl.ds(t, sixteenth, stride=RADIX), :] for t in range(RADIX)]
            _, excl2, state = _radix_scan(
                av2, bv2, lambda p3, h3: _vreg_scan(p3, h3, carry[:, cs], reverse), reverse)
            for t in range(RADIX):
                ent_s[pl.ds(t, sixteenth, stride=RADIX), :] = excl2[t]
            carry[:, cs] = state
            return ent_s[...], state

        incl, _, _ = _radix_scan(av, bv, level2, reverse)
        for t in range(RADIX):
            rows = slice(t * quarter, (t + 1) * quarter)
            if reverse:
                ubuf[n, pl.ds(t, quarter, stride=RADIX), :] = incl[t] + hf_ref[rows, cs]
            else:
                out_ref[rows, cs] = incl[t]

    if reverse:
        for n in range(nblk):
            cs = slice(n * RNN_BLOCK, (n + 1) * RNN_BLOCK)
            out_ref[:, cs] = (ubuf[n] * _silu_of_half(gr_ref[:, cs])).astype(out_ref.dtype)


def _rglru(xr, x_bf, w_m, b_m, conv_w, conv_b, wg, lam, reverse, h_fwd=None, gr_src=None):
    b, s, _ = xr.shape
    n_m = w_m.shape[1]
    tn_m = n_m // 2
    tile = RNN_TILE
    n_tiles = s // tile
    nslab = D_RNN // RNN_SLAB
    sub_per_tile = tile // SUBLANES
    n_sub = s // SUBLANES
    nblk = RNN_SLAB // RNN_BLOCK
    quarter = tile // RADIX

    def tidx(i):
        return (n_tiles - 1 - i) if reverse else i

    x_cur = pl.BlockSpec((None, tile, RNN_SLAB), lambda bb, c, i: (bb, tidx(i), c))
    x_prev = pl.BlockSpec((None, SUBLANES, RNN_SLAB),
                          lambda bb, c, i: (bb, jnp.maximum(tidx(i) * sub_per_tile - 1, 0), c))
    x_next = pl.BlockSpec((None, SUBLANES, RNN_SLAB),
                          lambda bb, c, i: (bb, jnp.minimum((tidx(i) + 1) * sub_per_tile, n_sub - 1), c))
    m_tiles = b * s // MM_TOKENS
    assert m_tiles * 2 == b * nslab * n_tiles

    def m_col(bb, c, i):
        return ((bb * nslab + c) * n_tiles + i) // m_tiles

    def m_row(bb, c, i):
        return ((bb * nslab + c) * n_tiles + i) % m_tiles

    time_tile = pl.BlockSpec((None, tile, RNN_SLAB), lambda bb, c, i: (bb, tidx(i), c))
    gate_w = pl.BlockSpec((nblk, 2 * RNN_BLOCK, 2 * RNN_BLOCK), lambda bb, c, i: (c, 0, 0))
    lam_spec = pl.BlockSpec((1, RNN_SLAB), lambda bb, c, i: (0, c))
    in_specs = [
        pl.BlockSpec((MM_TOKENS, D_MODEL), lambda bb, c, i: (m_row(bb, c, i), 0)),
        pl.BlockSpec((D_MODEL, tn_m), lambda bb, c, i: (0, m_col(bb, c, i))),
        pl.BlockSpec((1, tn_m), lambda bb, c, i: (0, m_col(bb, c, i))),
    ]
    args = [x_bf.reshape(b * s, D_MODEL), w_m, b_m]
    scan_scratch = [
        pltpu.VMEM((quarter, RNN_BLOCK), jnp.float32),
        pltpu.VMEM((quarter, RNN_BLOCK), jnp.float32),
        pltpu.VMEM((quarter, RNN_BLOCK), jnp.float32),
        pltpu.VMEM((1, RNN_SLAB), jnp.float32),
    ]
    out_specs = [pl.BlockSpec((MM_TOKENS, tn_m), lambda bb, c, i: (m_row(bb, c, i), m_col(bb, c, i))), time_tile]
    state_sds = jax.ShapeDtypeStruct((b, s, D_RNN), jnp.float32)
    if reverse:
        in_specs += [time_tile, gate_w, lam_spec, time_tile, time_tile]
        args += [xr, wg, lam, h_fwd, gr_src]
        scratch = scan_scratch + [pltpu.VMEM((nblk, tile, RNN_BLOCK), jnp.float32)]
        out_shape = [jax.ShapeDtypeStruct((b * s, n_m), jnp.float32),
                     jax.ShapeDtypeStruct((b, s, D_RNN), jnp.bfloat16)]
    else:
        in_specs += [x_prev, x_cur, x_next,
                     pl.BlockSpec((CONV_WIDTH, RNN_SLAB), lambda bb, c, i: (0, c)),
                     pl.BlockSpec((1, RNN_SLAB), lambda bb, c, i: (0, c)),
                     gate_w, lam_spec]
        args += [xr, xr, xr, conv_w, conv_b, wg, lam]
        scratch = [pltpu.VMEM((nblk, tile + 2 * SUBLANES, RNN_BLOCK), jnp.float32)] + scan_scratch
        out_specs.append(time_tile)
        out_shape = [jax.ShapeDtypeStruct((b * s, n_m), jnp.float32), state_sds, state_sds]
    outs = pl.pallas_call(
        functools.partial(_rglru_kernel, tile=tile, n_tiles=n_tiles, reverse=reverse),
        grid=(b, nslab, n_tiles),
        in_specs=in_specs,
        out_specs=out_specs,
        out_shape=out_shape,
        scratch_shapes=scratch,
        compiler_params=_params("arbitrary", "arbitrary", "arbitrary"),
        name="rglru_bwd" if reverse else "rglru_fwd",
    )(*args)
    return (outs[0].reshape(b, s, n_m),) + tuple(outs[1:])


def _merge_kernel(x_ref, o0_ref, o1_ref, o2_ref, l0_ref, l1_ref, l2_ref, ga_ref, ur_ref,
                  gma_lo_ref, gma_hi_ref, gmr_lo_ref, gmr_hi_ref,
                  wa_ref, wr_ref, wo_ref, lng_ref, lnb_ref, y_ref, ybf_ref):
    def heads(ref):
        return jnp.concatenate([ref[h] for h in range(HEADS_PER_GROUP)], axis=1)

    half_gma = jnp.concatenate([gma_lo_ref[...], gma_hi_ref[...]], axis=1)
    half_gmr = jnp.concatenate([gmr_lo_ref[...], gmr_hi_ref[...]], axis=1)

    l0, l1, l2 = heads(l0_ref), heads(l1_ref), heads(l2_ref)
    mx = jnp.maximum(jnp.maximum(l0, l1), l2)
    e0, e1, e2 = jnp.exp(l0 - mx), jnp.exp(l1 - mx), jnp.exp(l2 - mx)
    inv = 1.0 / (e0 + e1 + e2)
    oa = (e0 * inv) * heads(o0_ref) + (e1 * inv) * heads(o1_ref) + (e2 * inv) * heads(o2_ref)
    ua = (oa * _silu_of_half(ga_ref[...])).astype(jnp.bfloat16)
    ya = jnp.dot(ua, wa_ref[...], preferred_element_type=jnp.float32)
    yr = jnp.dot(ur_ref[...], wr_ref[...], preferred_element_type=jnp.float32)
    mixed = (0.5 + 0.5 * jnp.tanh(half_gma)) * ya + (0.5 + 0.5 * jnp.tanh(half_gmr)) * yr
    out = jnp.dot(mixed.astype(jnp.bfloat16), wo_ref[...], preferred_element_type=jnp.float32)
    v = ALPHA * x_ref[...] + out
    mu = jnp.mean(v, axis=-1, keepdims=True)
    vc = v - mu
    var = jnp.mean(vc * vc, axis=-1, keepdims=True)
    y = vc * lax.rsqrt(var + LN_EPS) * lng_ref[...] + lnb_ref[...]
    y_ref[...] = y
    ybf_ref[...] = y.astype(jnp.bfloat16)


def _merge_out(x, attn, z1, z2, u_r, wa, wr, wo, ln_g, ln_b):
    bsz, s, d = x.shape
    tm = MERGE_TM
    row = lambda w: pl.BlockSpec((None, tm, w), lambda bb, i: (bb, i, 0))
    head = pl.BlockSpec((None, HEADS_PER_GROUP, tm, HEAD_DIM), lambda bb, i: (bb, 0, i, 0))
    const = lambda shp: pl.BlockSpec(shp, lambda bb, i: (0, 0), pipeline_mode=pl.Buffered(1))
    half = lambda j: pl.BlockSpec((None, tm, GM_SPLIT), lambda bb, i: (bb, i, j))
    in_specs = [
        row(d),
        head, head, head, head, head, head,
        pl.BlockSpec((None, tm, D_GROUP), lambda bb, i: (bb, i, (GATE2_COLS - D_GROUP) // D_GROUP)),
        row(D_RNN),
        half(D_RNN // GM_SPLIT),
        half(0),
        half(1), half(2),
        const((D_GROUP, D_MODEL)), const((D_RNN, D_MODEL)), const((D_MODEL, D_MODEL)),
        const((1, D_MODEL)), const((1, D_MODEL)),
    ]
    (o0, l0), (o1, l1), (o2, l2) = attn
    return pl.pallas_call(
        _merge_kernel,
        grid=(bsz, s // tm),
        in_specs=in_specs,
        out_specs=[row(d), row(d)],
        out_shape=[jax.ShapeDtypeStruct((bsz, s, d), jnp.float32), jax.ShapeDtypeStruct((bsz, s, d), jnp.bfloat16)],
        compiler_params=_params("parallel", "parallel"),
        name="merge_out",
    )(x, o0, o1, o2, l0, l1, l2, z2, u_r, z1, z2, z2, z2, wa, wr, wo, ln_g, ln_b)


def _cast_cols_kernel(w_ref, o_ref, *, scale):
    w = w_ref[...]
    if scale != 1.0:
        w = w * scale
    o_ref[...] = w.astype(o_ref.dtype)


def _cast_cols(w_in, l, src_block, n_blocks, scale=1.0):
    d = w_in.shape[1]
    return pl.pallas_call(
        functools.partial(_cast_cols_kernel, scale=scale),
        grid=(n_blocks,),
        in_specs=[pl.BlockSpec((None, d, D_GROUP), lambda j: (l, 0, src_block(j)))],
        out_specs=pl.BlockSpec((d, D_GROUP), lambda j: (0, j)),
        out_shape=jax.ShapeDtypeStruct((d, n_blocks * D_GROUP), jnp.bfloat16),
        compiler_params=_params("parallel"),
        name="cast_w_in",
    )(w_in)


def _prep_layer(l, w_in, b_in, conv_w, conv_b, lru_w, lru_b, lru_lam, w_attn_o, w_rnn_o, w_out, ln_g, ln_b):
    bf = jnp.bfloat16
    a3 = 3 * D_ATTN
    c_ga, c_xr, c_gr, c_gm = a3, a3 + D_GROUP, a3 + D_GROUP + D_RNN, a3 + D_GROUP + 2 * D_RNN
    k_ga, k_xr, k_gr, k_gm = (c // D_GROUP for c in (c_ga, c_xr, c_gr, c_gm))
    k_split = k_gm + GM_SPLIT // D_GROUP
    n_g2_gm = (D_MODEL - GM_SPLIT + D_MODEL) // D_GROUP

    def gate1_cols(w):
        return w[..., c_gr:c_gm + GM_SPLIT]

    def gate2_cols(w):
        return jnp.concatenate([w[..., c_gm + GM_SPLIT:], w[..., c_ga:c_xr]], axis=-1)

    def group_cols(w, g):
        return jnp.concatenate([w[..., part * D_ATTN + g * D_GROUP:part * D_ATTN + (g + 1) * D_GROUP]
                                for part in range(3)], axis=-1)

    def gate_w(direction):
        w = 0.5 * jnp.concatenate([lru_w[l, direction, 0], lru_w[l, direction, 1]], axis=-1)
        bias = 0.5 * jnp.concatenate([lru_b[l, direction, 0], lru_b[l, direction, 1]], axis=-1)
        terms, rem = [], bias
        for _ in range(GATE_BIAS_ROWS):
            term = rem.astype(bf)
            terms.append(term)
            rem = rem - term.astype(jnp.float32)
        pad = jnp.zeros((N_RNN_BLOCKS, RNN_BLOCK - GATE_BIAS_ROWS, 2 * RNN_BLOCK), bf)
        return jnp.concatenate([w.astype(bf), jnp.stack(terms, axis=1), pad], axis=1)

    return dict(
        w_qkv=[_cast_cols(w_in, l, lambda j, g=g: j * (D_ATTN // D_GROUP) + g, 3) for g in range(N_GROUPS)],
        b_qkv=[group_cols(b_in[l], g)[None, :] for g in range(N_GROUPS)],
        w_xr=_cast_cols(w_in, l, lambda j: k_xr + j, D_RNN // D_GROUP), b_xr=b_in[l, None, c_xr:c_gr],
        w_g1=_cast_cols(w_in, l, lambda j: k_gr + j, GATE1_COLS // D_GROUP, 0.5),
        b_g1=0.5 * gate1_cols(b_in[l])[None, :],
        w_g2=_cast_cols(w_in, l, lambda j: jnp.where(j < n_g2_gm, k_split + j, k_ga), GATE2_COLS // D_GROUP, 0.5),
        b_g2=0.5 * gate2_cols(b_in[l])[None, :],
        conv_w=conv_w[l], conv_b=conv_b[l][None, :],
        wg=[gate_w(0), gate_w(1)],
        lam=[lru_lam[l, 0][None, :], lru_lam[l, 1][None, :]],
        wa=w_attn_o[l].astype(bf), wr=w_rnn_o[l].astype(bf), wo=w_out[l].astype(bf),
        ln_g=ln_g[l][None, :], ln_b=ln_b[l][None, :],
    )


def _layer(x, x_bf, p, biases):
    attn = []
    for g, (_, dil) in enumerate(DILATED_GROUPS):
        if x_bf is None:
            qkv, x_bf = _qkv_proj(x, p["w_qkv"][g], p["b_qkv"][g], dil, g)
        else:
            qkv = _qkv_proj(x_bf, p["w_qkv"][g], p["b_qkv"][g], dil, g)
        attn.append(_attention_group(qkv, biases[g], g, dil))
    xr = _xr_proj(x_bf, p["w_xr"], p["b_xr"])
    z1, h_fwd, xconv = _rglru(xr, x_bf, p["w_g1"], p["b_g1"], p["conv_w"], p["conv_b"],
                              p["wg"][0], p["lam"][0], False)
    z2, u_r = _rglru(xconv, x_bf, p["w_g2"], p["b_g2"], None, None,
                     p["wg"][1], p["lam"][1], True, h_fwd, z1)
    return _merge_out(x, attn, z1, z2, u_r, p["wa"], p["wr"], p["wo"], p["ln_g"], p["ln_b"])


def kernel(x_prompt, x_sample, w_in, b_in, conv_w, conv_b, lru_w, lru_b, lru_lam,
           w_attn_o, w_rnn_o, w_out, ln_g, ln_b, rel_bias):
    layers = [_prep_layer(l, w_in, b_in, conv_w, conv_b, lru_w, lru_b, lru_lam,
                          w_attn_o, w_rnn_o, w_out, ln_g, ln_b) for l in range(DEPTH)]
    biases = [_band_bias(rel_bias, g, dil) for g, (_, dil) in enumerate(DILATED_GROUPS)]

    def trunk(x):
        x_bf = None
        for p in layers:
            x, x_bf = _layer(x, x_bf, p, biases)
        return x

    return (trunk(x_prompt), trunk(x_sample))
```

```python
import functools

import numpy as np
import jax
import jax.numpy as jnp
from jax import lax
from jax.experimental import pallas as pl
from jax.experimental.pallas import tpu as pltpu

D_MODEL = 2048
DEPTH = 2
HEAD_DIM = 128
HEADS_PER_GROUP = 4
DILATED_GROUPS = ((128, 1), (512, 4), (2048, 16))
N_GROUPS = len(DILATED_GROUPS)
D_ATTN = N_GROUPS * HEADS_PER_GROUP * HEAD_DIM
D_GROUP = HEADS_PER_GROUP * HEAD_DIM
D_RNN = D_MODEL
N_RNN_BLOCKS = 16
RNN_BLOCK = D_RNN // N_RNN_BLOCKS
CONV_WIDTH = 4
CONV_LEFT = 2
LRU_C = 8.0
REL_BUCKETS = 32
REL_MAX_DIST = 1024
ALPHA = (2.0 * DEPTH) ** 0.25
LN_EPS = 1e-5
NEG_INF = -1e30

LANES = 128
SUBLANES = 8
VMEM_LIMIT_BYTES = 56 * 1024 * 1024

HALF = 64
Q_SUB = 128
K_WIN = Q_SUB + 2 * HALF
ATTN_TOKENS = 2048
ATTN_RESIDUES_PER_TRIP = 16

RNN_TILE = 512
RNN_SLAB = 1024
RADIX = 4
GATE_BIAS_ROWS = 3
LOG2_E = 1.4426950408889634

MM_TOKENS = RNN_TILE
MM_PIECE = 256

PROJ_TM = 2048
MERGE_TM = 256

GM_SPLIT = D_MODEL // 2
GATE1_COLS = D_RNN + GM_SPLIT
GATE2_COLS = (D_MODEL - GM_SPLIT) + D_MODEL + D_GROUP


def _params(*sem):
    return pltpu.CompilerParams(dimension_semantics=sem, vmem_limit_bytes=VMEM_LIMIT_BYTES)


def _silu_of_half(h):
    return h + h * jnp.tanh(h)


def _xr_proj_kernel(x_ref, w_ref, b_ref, o_ref):
    acc = jnp.dot(x_ref[...], w_ref[...], preferred_element_type=jnp.float32)
    o_ref[...] = acc + b_ref[...]


def _xr_proj(x_bf, w_bf, b):
    bsz, s, d = x_bf.shape
    n = w_bf.shape[1]
    tm, tn = PROJ_TM // 2, n
    return pl.pallas_call(
        _xr_proj_kernel,
        grid=(bsz, s // tm, n // tn),
        in_specs=[
            pl.BlockSpec((None, tm, d), lambda bb, i, j: (bb, i, 0)),
            pl.BlockSpec((d, tn), lambda bb, i, j: (0, j)),
            pl.BlockSpec((1, tn), lambda bb, i, j: (0, j)),
        ],
        out_specs=pl.BlockSpec((None, tm, tn), lambda bb, i, j: (bb, i, j)),
        out_shape=jax.ShapeDtypeStruct((bsz, s, n), jnp.float32),
        compiler_params=_params("parallel", "parallel", "arbitrary"),
        name="xr_proj",
    )(x_bf, w_bf, b)


def _qkv_proj_kernel(x_ref, w_ref, b_ref, o_ref, *rest, dil, tm, tn, emit_bf16):
    if emit_bf16:
        xbf_ref, rest = rest[0], rest[1:]

        @pl.when(pl.program_id(2) == 0)
        def _():
            xbf_ref[...] = x_ref[...].astype(jnp.bfloat16)

        x = xbf_ref[...]
    else:
        x = x_ref[...]
    acc = jnp.dot(x, w_ref[...], preferred_element_type=jnp.float32) + b_ref[...]
    if dil == 1:
        o_ref[0] = acc.astype(o_ref.dtype)
        return
    acc_s = rest[0]
    rows = tm // dil
    for c in range(tn // LANES):
        cs = slice(c * LANES, (c + 1) * LANES)
        acc_s[c] = acc[:, cs]
        if dil == RADIX:
            for r in range(dil):
                o_ref[r, :, cs] = acc_s[c, pl.ds(r, rows, stride=dil), :].astype(o_ref.dtype)
        else:
            assert dil == RADIX * RADIX
            mid_s = rest[1]
            for lo in range(RADIX):
                mid_s[lo] = acc_s[c, pl.ds(lo, tm // RADIX, stride=RADIX), :]
            for lo in range(RADIX):
                for hi in range(RADIX):
                    o_ref[lo + RADIX * hi, :, cs] = mid_s[lo, pl.ds(hi, rows, stride=RADIX), :].astype(o_ref.dtype)


def _qkv_proj(x, w_bf, b, dil, g):
    bsz, s, d = x.shape
    n = w_bf.shape[1]
    emit_bf16 = x.dtype != jnp.bfloat16
    tm, tn = PROJ_TM // 2, n
    scratch = [] if dil == 1 else [pltpu.VMEM((tn // LANES, tm, LANES), jnp.float32)]
    if dil > RADIX:
        scratch.append(pltpu.VMEM((RADIX, tm // RADIX, LANES), jnp.float32))
    out_specs = [pl.BlockSpec((None, dil, tm // dil, tn), lambda bb, i, j: (bb, 0, i, j))]
    out_shape = [jax.ShapeDtypeStruct((bsz, dil, s // dil, n), jnp.bfloat16)]
    if emit_bf16:
        out_specs.append(pl.BlockSpec((None, tm, d), lambda bb, i, j: (bb, i, 0)))
        out_shape.append(jax.ShapeDtypeStruct((bsz, s, d), jnp.bfloat16))
    outs = pl.pallas_call(
        functools.partial(_qkv_proj_kernel, dil=dil, tm=tm, tn=tn, emit_bf16=emit_bf16),
        grid=(bsz, s // tm, n // tn),
        in_specs=[
            pl.BlockSpec((None, tm, d), lambda bb, i, j: (bb, i, 0)),
            pl.BlockSpec((d, tn), lambda bb, i, j: (0, j)),
            pl.BlockSpec((1, tn), lambda bb, i, j: (0, j)),
        ],
        out_specs=out_specs,
        out_shape=out_shape,
        scratch_shapes=scratch,
        compiler_params=_params("parallel", "parallel", "arbitrary"),
        name=f"qkv_proj_g{g}",
    )(x, w_bf, b)
    return outs if emit_bf16 else outs[0]


def _t5_bucket(rel):
    nb = REL_BUCKETS // 2
    max_exact = nb // 2
    ret = (rel > 0).astype(np.int32) * nb
    n = np.abs(rel)
    large = max_exact + (np.log(np.maximum(n, max_exact) / max_exact)
                         / np.log(REL_MAX_DIST / max_exact) * (nb - max_exact)).astype(np.int32)
    large = np.minimum(large, nb - 1)
    return (ret + np.where(n < max_exact, n, large)).astype(np.int32)


def _band_bias(rel_bias, g, dil):
    tab = rel_bias[_t5_bucket(np.arange(-HALF, HALF + 1) * dil)][:, g * HEADS_PER_GROUP:(g + 1) * HEADS_PER_GROUP]
    tab = tab * (HEAD_DIM ** 0.5)
    period = Q_SUB + K_WIN + LANES
    vec = jnp.full((HEADS_PER_GROUP, period), NEG_INF, jnp.float32).at[:, :2 * HALF + 1].set(tab.T)
    mat = jnp.tile(vec, (1, Q_SUB))[:, :Q_SUB * (period - 1)].reshape(HEADS_PER_GROUP, Q_SUB, period - 1)
    return mat[:, :, :K_WIN]


def _attn_kernel(q_ref, kp_ref, kc_ref, kn_ref, vp_ref, vc_ref, vn_ref, bias_ref,
                 o_ref, lse_ref, kbuf, vbuf, *, dil, tile, length):
    i = pl.program_id(1)
    scale = HEAD_DIM ** -0.5

    def residue(r, slot):
        kbuf[slot, 0:HALF] = kp_ref[r]
        kbuf[slot, HALF:HALF + tile] = kc_ref[r]
        kbuf[slot, HALF + tile:] = kn_ref[r]
        vbuf[slot, 0:HALF] = vp_ref[r]
        vbuf[slot, HALF:HALF + tile] = vc_ref[r]
        vbuf[slot, HALF + tile:] = vn_ref[r]
        n_sub = tile // Q_SUB
        for sb in range(n_sub):
            r0 = sb * Q_SUB
            at_edge = sb == 0 or sb == n_sub - 1
            if at_edge:
                kpos = i * tile + (r0 - HALF) + lax.broadcasted_iota(jnp.int32, (Q_SUB, K_WIN), 1)
                in_seq = (kpos >= 0) & (kpos < length)
            if dil == 1:
                rows = slice(r0, r0 + Q_SUB)
            else:
                rows = pl.ds(r + r0 * dil, Q_SUB, stride=dil)
            for h in range(HEADS_PER_GROUP):
                cs = slice(h * HEAD_DIM, (h + 1) * HEAD_DIM)
                q = q_ref[r, r0:r0 + Q_SUB, cs]
                k = kbuf[slot, r0:r0 + K_WIN, cs]
                v = vbuf[slot, r0:r0 + K_WIN, cs]
                t = lax.dot_general(q, k, (((1,), (1,)), ((), ())),
                                    preferred_element_type=jnp.float32) + bias_ref[h]
                if at_edge:
                    t = jnp.where(in_seq, t, NEG_INF)
                mt = jnp.max(t, axis=-1, keepdims=True)
                p = jnp.exp2((t - mt) * (scale * LOG2_E))
                den = jnp.sum(p, axis=-1, keepdims=True)
                o = jnp.dot(p.astype(jnp.bfloat16), v, preferred_element_type=jnp.float32)
                o_ref[h, rows, :] = o / den
                lse_ref[h, rows, :] = jnp.broadcast_to(mt * scale + jnp.log(den), (Q_SUB, HEAD_DIM))

    per_trip = kbuf.shape[0]

    def trip(it, _):
        for slot in range(per_trip):
            residue(it * per_trip + slot, slot)
        return 0

    if dil == per_trip:
        trip(0, 0)
    else:
        lax.fori_loop(0, dil // per_trip, trip, 0)


def _attention_group(qkv, bias, g, dil):
    bsz, _, length, _ = qkv.shape
    s = length * dil
    tile = ATTN_TOKENS // dil
    per_trip = min(dil, ATTN_RESIDUES_PER_TRIP)
    hb = tile // HALF
    n_half_blocks = length // HALF

    def cur(part):
        return pl.BlockSpec((None, dil, tile, D_GROUP), lambda bb, i: (bb, 0, i, part))

    def prev(part):
        return pl.BlockSpec((None, dil, HALF, D_GROUP),
                            lambda bb, i: (bb, 0, jnp.maximum(i * hb - 1, 0), part))

    def nxt(part):
        return pl.BlockSpec((None, dil, HALF, D_GROUP),
                            lambda bb, i: (bb, 0, jnp.minimum((i + 1) * hb, n_half_blocks - 1), part))

    out_spec = pl.BlockSpec((None, HEADS_PER_GROUP, tile * dil, HEAD_DIM), lambda bb, i: (bb, 0, i, 0))
    out_sds = jax.ShapeDtypeStruct((bsz, HEADS_PER_GROUP, s, HEAD_DIM), jnp.float32)
    return pl.pallas_call(
        functools.partial(_attn_kernel, dil=dil, tile=tile, length=length),
        grid=(bsz, length // tile),
        in_specs=[cur(0), prev(1), cur(1), nxt(1), prev(2), cur(2), nxt(2),
                  pl.BlockSpec((HEADS_PER_GROUP, Q_SUB, K_WIN), lambda bb, i: (0, 0, 0))],
        out_specs=[out_spec, out_spec],
        out_shape=[out_sds, out_sds],
        scratch_shapes=[pltpu.VMEM((per_trip, tile + 2 * HALF, D_GROUP), jnp.bfloat16),
                        pltpu.VMEM((per_trip, tile + 2 * HALF, D_GROUP), jnp.bfloat16)],
        compiler_params=_params("parallel", "arbitrary"),
        name=f"attn_g{g}",
    )(qkv, qkv, qkv, qkv, qkv, qkv, qkv, bias)


def _vreg_scan(a, b, state, reverse):
    groups = a.shape[0] // SUBLANES
    a3 = a.reshape(groups, SUBLANES, LANES)
    b3 = b.reshape(groups, SUBLANES, LANES)
    sub = lax.broadcasted_iota(jnp.int32, (groups, SUBLANES, LANES), 1)
    for shift in (1, 2, 4):
        keep = (sub < SUBLANES - shift) if reverse else (sub >= shift)
        amount = SUBLANES - shift if reverse else shift
        a_sh = jnp.where(keep, pltpu.roll(a3, amount, axis=1), 1.0)
        b_sh = jnp.where(keep, pltpu.roll(b3, amount, axis=1), 0.0)
        b3 = a3 * b_sh + b3
        a3 = a3 * a_sh
    sub2 = lax.broadcasted_iota(jnp.int32, (SUBLANES, LANES), 0)
    entering = [None] * groups
    for gi in (range(groups - 1, -1, -1) if reverse else range(groups)):
        sb = jnp.broadcast_to(state, (SUBLANES, LANES))
        incl = b3[gi] + a3[gi] * sb
        if reverse:
            entering[gi] = jnp.where(sub2 == SUBLANES - 1, sb, pltpu.roll(incl, SUBLANES - 1, axis=0))
            state = incl[0:1]
        else:
            entering[gi] = jnp.where(sub2 == 0, sb, pltpu.roll(incl, 1, axis=0))
            state = incl[SUBLANES - 1:SUBLANES]
    return jnp.concatenate(entering, axis=0), state


def _radix_scan(av, bv, entering_fn, reverse):
    order = tuple(range(RADIX - 1, -1, -1)) if reverse else tuple(range(RADIX))
    h = [None] * RADIX
    p = [None] * RADIX
    h[order[0]], p[order[0]] = bv[order[0]], av[order[0]]
    for before, t in zip(order[:-1], order[1:]):
        h[t] = av[t] * h[before] + bv[t]
        p[t] = av[t] * p[before]
    entering, state = entering_fn(p[order[-1]], h[order[-1]])
    incl = [h[t] + p[t] * entering for t in range(RADIX)]
    excl = [None] * RADIX
    excl[order[0]] = entering
    for before, t in zip(order[:-1], order[1:]):
        excl[t] = incl[before]
    return incl, excl, state


def _rglru_kernel(*refs, tile, n_tiles, reverse):
    if reverse:
        (xm_ref, wm_ref, bm_ref, conv_ref, wg_ref, lam_ref, hf_ref, gr_ref,
         zm_ref, out_ref, sum_a, sum_b, ent_s, carry, ubuf) = refs
    else:
        (xm_ref, wm_ref, bm_ref, xp_ref, xc_ref, xn_ref, cw_ref, cb_ref, wg_ref, lam_ref,
         zm_ref, out_ref, conv_ref, ext, sum_a, sum_b, ent_s, carry) = refs

    def gate_proj_piece(k):
        cols = slice(k * MM_PIECE, (k + 1) * MM_PIECE)
        zm_ref[:, cols] = (jnp.dot(xm_ref[...], wm_ref[:, cols], preferred_element_type=jnp.float32)
                           + bm_ref[:, cols])

    n_pieces = zm_ref.shape[1] // MM_PIECE
    step = pl.program_id(2)
    ti = (n_tiles - 1 - step) if reverse else step
    quarter = tile // RADIX
    sixteenth = quarter // RADIX
    nblk = RNN_SLAB // RNN_BLOCK

    @pl.when(step == 0)
    def _():
        carry[...] = jnp.zeros_like(carry)

    lam = lam_ref[...]
    half_csp = (0.5 * LRU_C) * (jnp.maximum(-lam, 0.0) + jnp.log1p(jnp.exp(-jnp.abs(lam))))
    bias_taps = (lax.broadcasted_iota(jnp.int32, (tile, RNN_BLOCK), 1) < GATE_BIAS_ROWS).astype(jnp.bfloat16)

    assert n_pieces <= nblk
    for n in range(nblk):
        cs = slice(n * RNN_BLOCK, (n + 1) * RNN_BLOCK)
        if reverse:
            xc = conv_ref[:, cs]
        else:
            ext[n, 0:SUBLANES] = jnp.where(ti > 0, xp_ref[:, cs], 0.0)
            ext[n, SUBLANES:SUBLANES + tile] = xc_ref[:, cs]
            ext[n, SUBLANES + tile:] = jnp.where(ti < n_tiles - 1, xn_ref[:, cs], 0.0)
            taps = {u: ext[n, pl.ds(SUBLANES + u, quarter, stride=RADIX), :]
                    for u in range(-CONV_LEFT, RADIX + CONV_WIDTH - 1 - CONV_LEFT)}
            xc_t = []
            for t in range(RADIX):
                acc = cb_ref[:, cs]
                for j in range(CONV_WIDTH):
                    acc = acc + taps[t + j - CONV_LEFT] * cw_ref[j:j + 1, cs]
                xc_t.append(acc)
            xc = jnp.concatenate(xc_t, axis=0)
            conv_ref[:, cs] = xc
        lhs = jnp.concatenate([xc.astype(jnp.bfloat16), bias_taps], axis=1)
        z = jnp.dot(lhs, wg_ref[n], preferred_element_type=jnp.float32)
        if n < n_pieces:
            gate_proj_piece(n)
        th_r = jnp.tanh(z[:, :RNN_BLOCK])
        th_i = jnp.tanh(z[:, RNN_BLOCK:])
        hsp = half_csp[:, cs]
        neg_log_a = hsp + hsp * th_r
        a = jnp.exp2(neg_log_a * (-LOG2_E))
        half_xc = 0.5 * xc
        one_m_a2 = jnp.tanh(neg_log_a) * (1.0 + a * a)
        root = jnp.where(one_m_a2 > 0.0, one_m_a2 * lax.rsqrt(one_m_a2), 0.0)
        b = root * (half_xc + half_xc * th_i)
        av = [a[t * quarter:(t + 1) * quarter] for t in range(RADIX)]
        bv = [b[t * quarter:(t + 1) * quarter] for t in range(RADIX)]

        def level2(p, h, n=n, cs=cs):
            sum_a[...] = p
            sum_b[...] = h
            av2 = [sum_a[pl.ds(t, sixteenth, stride=RADIX), :] for t in range(RADIX)]
            bv2 = [sum_b[pl.ds(t, sixteenth, stride=RADIX), :] for t in range(RADIX)]
            _, excl2, state = _radix_scan(
                av2, bv2, lambda p3, h3: _vreg_scan(p3, h3, carry[:, cs], reverse), reverse)
            for t in range(RADIX):
                ent_s[pl.ds(t, sixteenth, stride=RADIX), :] = excl2[t]
            carry[:, cs] = state
            return ent_s[...], state

        incl, _, _ = _radix_scan(av, bv, level2, reverse)
        for t in range(RADIX):
            rows = slice(t * quarter, (t + 1) * quarter)
            if reverse:
                ubuf[n, pl.ds(t, quarter, stride=RADIX), :] = incl[t] + hf_ref[rows, cs]
            else:
                out_ref[rows, cs] = incl[t]

    if reverse:
        for n in range(nblk):
            cs = slice(n * RNN_BLOCK, (n + 1) * RNN_BLOCK)
            out_ref[:, cs] = (ubuf[n] * _silu_of_half(gr_ref[:, cs])).astype(out_ref.dtype)


def _rglru(xr, x_bf, w_m, b_m, conv_w, conv_b, wg, lam, reverse, h_fwd=None, gr_src=None):
    b, s, _ = xr.shape
    n_m = w_m.shape[1]
    tn_m = n_m // 2
    tile = RNN_TILE
    n_tiles = s // tile
    nslab = D_RNN // RNN_SLAB
    sub_per_tile = tile // SUBLANES
    n_sub = s // SUBLANES
    nblk = RNN_SLAB // RNN_BLOCK
    quarter = tile // RADIX

    def tidx(i):
        return (n_tiles - 1 - i) if reverse else i

    x_cur = pl.BlockSpec((None, tile, RNN_SLAB), lambda bb, c, i: (bb, tidx(i), c))
    x_prev = pl.BlockSpec((None, SUBLANES, RNN_SLAB),
                          lambda bb, c, i: (bb, jnp.maximum(tidx(i) * sub_per_tile - 1, 0), c))
    x_next = pl.BlockSpec((None, SUBLANES, RNN_SLAB),
                          lambda bb, c, i: (bb, jnp.minimum((tidx(i) + 1) * sub_per_tile, n_sub - 1), c))
    m_tiles = b * s // MM_TOKENS
    assert m_tiles * 2 == b * nslab * n_tiles

    def m_col(bb, c, i):
        return ((bb * nslab + c) * n_tiles + i) // m_tiles

    def m_row(bb, c, i):
        return ((bb * nslab + c) * n_tiles + i) % m_tiles

    time_tile = pl.BlockSpec((None, tile, RNN_SLAB), lambda bb, c, i: (bb, tidx(i), c))
    gate_w = pl.BlockSpec((nblk, 2 * RNN_BLOCK, 2 * RNN_BLOCK), lambda bb, c, i: (c, 0, 0))
    lam_spec = pl.BlockSpec((1, RNN_SLAB), lambda bb, c, i: (0, c))
    in_specs = [
        pl.BlockSpec((MM_TOKENS, D_MODEL), lambda bb, c, i: (m_row(bb, c, i), 0)),
        pl.BlockSpec((D_MODEL, tn_m), lambda bb, c, i: (0, m_col(bb, c, i))),
        pl.BlockSpec((1, tn_m), lambda bb, c, i: (0, m_col(bb, c, i))),
    ]
    args = [x_bf.reshape(b * s, D_MODEL), w_m, b_m]
    scan_scratch = [
        pltpu.VMEM((quarter, RNN_BLOCK), jnp.float32),
        pltpu.VMEM((quarter, RNN_BLOCK), jnp.float32),
        pltpu.VMEM((quarter, RNN_BLOCK), jnp.float32),
        pltpu.VMEM((1, RNN_SLAB), jnp.float32),
    ]
    out_specs = [pl.BlockSpec((MM_TOKENS, tn_m), lambda bb, c, i: (m_row(bb, c, i), m_col(bb, c, i))), time_tile]
    state_sds = jax.ShapeDtypeStruct((b, s, D_RNN), jnp.float32)
    if reverse:
        in_specs += [time_tile, gate_w, lam_spec, time_tile, time_tile]
        args += [xr, wg, lam, h_fwd, gr_src]
        scratch = scan_scratch + [pltpu.VMEM((nblk, tile, RNN_BLOCK), jnp.float32)]
        out_shape = [jax.ShapeDtypeStruct((b * s, n_m), jnp.float32),
                     jax.ShapeDtypeStruct((b, s, D_RNN), jnp.bfloat16)]
    else:
        in_specs += [x_prev, x_cur, x_next,
                     pl.BlockSpec((CONV_WIDTH, RNN_SLAB), lambda bb, c, i: (0, c)),
                     pl.BlockSpec((1, RNN_SLAB), lambda bb, c, i: (0, c)),
                     gate_w, lam_spec]
        args += [xr, xr, xr, conv_w, conv_b, wg, lam]
        scratch = [pltpu.VMEM((nblk, tile + 2 * SUBLANES, RNN_BLOCK), jnp.float32)] + scan_scratch
        out_specs.append(time_tile)
        out_shape = [jax.ShapeDtypeStruct((b * s, n_m), jnp.float32), state_sds, state_sds]
    outs = pl.pallas_call(
        functools.partial(_rglru_kernel, tile=tile, n_tiles=n_tiles, reverse=reverse),
        grid=(b, nslab, n_tiles),
        in_specs=in_specs,
        out_specs=out_specs,
        out_shape=out_shape,
        scratch_shapes=scratch,
        compiler_params=_params("arbitrary", "arbitrary", "arbitrary"),
        name="rglru_bwd" if reverse else "rglru_fwd",
    )(*args)
    return (outs[0].reshape(b, s, n_m),) + tuple(outs[1:])


def _merge_kernel(x_ref, o0_ref, o1_ref, o2_ref, l0_ref, l1_ref, l2_ref, ga_ref, ur_ref,
                  gma_lo_ref, gma_hi_ref, gmr_lo_ref, gmr_hi_ref,
                  wa_ref, wr_ref, wo_ref, lng_ref, lnb_ref, y_ref, ybf_ref):
    def heads(ref):
        return jnp.concatenate([ref[h] for h in range(HEADS_PER_GROUP)], axis=1)

    half_gma = jnp.concatenate([gma_lo_ref[...], gma_hi_ref[...]], axis=1)
    half_gmr = jnp.concatenate([gmr_lo_ref[...], gmr_hi_ref[...]], axis=1)

    l0, l1, l2 = heads(l0_ref), heads(l1_ref), heads(l2_ref)
    mx = jnp.maximum(jnp.maximum(l0, l1), l2)
    e0, e1, e2 = jnp.exp(l0 - mx), jnp.exp(l1 - mx), jnp.exp(l2 - mx)
    inv = 1.0 / (e0 + e1 + e2)
    oa = (e0 * inv) * heads(o0_ref) + (e1 * inv) * heads(o1_ref) + (e2 * inv) * heads(o2_ref)
    ua = (oa * _silu_of_half(ga_ref[...])).astype(jnp.bfloat16)
    ya = jnp.dot(ua, wa_ref[...], preferred_element_type=jnp.float32)
    yr = jnp.dot(ur_ref[...], wr_ref[...], preferred_element_type=jnp.float32)
    mixed = (0.5 + 0.5 * jnp.tanh(half_gma)) * ya + (0.5 + 0.5 * jnp.tanh(half_gmr)) * yr
    out = jnp.dot(mixed.astype(jnp.bfloat16), wo_ref[...], preferred_element_type=jnp.float32)
    v = ALPHA * x_ref[...] + out
    mu = jnp.mean(v, axis=-1, keepdims=True)
    vc = v - mu
    var = jnp.mean(vc * vc, axis=-1, keepdims=True)
    y = vc * lax.rsqrt(var + LN_EPS) * lng_ref[...] + lnb_ref[...]
    y_ref[...] = y
    ybf_ref[...] = y.astype(jnp.bfloat16)


def _merge_out(x, attn, z1, z2, u_r, wa, wr, wo, ln_g, ln_b):
    bsz, s, d = x.shape
    tm = MERGE_TM
    row = lambda w: pl.BlockSpec((None, tm, w), lambda bb, i: (bb, i, 0))
    head = pl.BlockSpec((None, HEADS_PER_GROUP, tm, HEAD_DIM), lambda bb, i: (bb, 0, i, 0))
    const = lambda shp: pl.BlockSpec(shp, lambda bb, i: (0, 0), pipeline_mode=pl.Buffered(1))
    half = lambda j: pl.BlockSpec((None, tm, GM_SPLIT), lambda bb, i: (bb, i, j))
    in_specs = [
        row(d),
        head, head, head, head, head, head,
        pl.BlockSpec((None, tm, D_GROUP), lambda bb, i: (bb, i, (GATE2_COLS - D_GROUP) // D_GROUP)),
        row(D_RNN),
        half(D_RNN // GM_SPLIT),
        half(0),
        half(1), half(2),
        const((D_GROUP, D_MODEL)), const((D_RNN, D_MODEL)), const((D_MODEL, D_MODEL)),
        const((1, D_MODEL)), const((1, D_MODEL)),
    ]
    (o0, l0), (o1, l1), (o2, l2) = attn
    return pl.pallas_call(
        _merge_kernel,
        grid=(bsz, s // tm),
        in_specs=in_specs,
        out_specs=[row(d), row(d)],
        out_shape=[jax.ShapeDtypeStruct((bsz, s, d), jnp.float32), jax.ShapeDtypeStruct((bsz, s, d), jnp.bfloat16)],
        compiler_params=_params("parallel", "parallel"),
        name="merge_out",
    )(x, o0, o1, o2, l0, l1, l2, z2, u_r, z1, z2, z2, z2, wa, wr, wo, ln_g, ln_b)


def _cast_cols_kernel(w_ref, o_ref, *, scale):
    w = w_ref[...]
    if scale != 1.0:
        w = w * scale
    o_ref[...] = w.astype(o_ref.dtype)


def _cast_cols(w, l, src_block, n_blocks, scale=1.0):
    k = w.shape[1]
    return pl.pallas_call(
        functools.partial(_cast_cols_kernel, scale=scale),
        grid=(n_blocks,),
        in_specs=[pl.BlockSpec((None, k, D_GROUP), lambda j: (l, 0, src_block(j)))],
        out_specs=pl.BlockSpec((k, D_GROUP), lambda j: (0, j)),
        out_shape=jax.ShapeDtypeStruct((k, n_blocks * D_GROUP), jnp.bfloat16),
        compiler_params=_params("parallel"),
        name="cast_w",
    )(w)


def _prep_layer(l, w_in, b_in, conv_w, conv_b, lru_w, lru_b, lru_lam, w_attn_o, w_rnn_o, w_out, ln_g, ln_b):
    bf = jnp.bfloat16
    a3 = 3 * D_ATTN
    c_ga, c_xr, c_gr, c_gm = a3, a3 + D_GROUP, a3 + D_GROUP + D_RNN, a3 + D_GROUP + 2 * D_RNN
    k_ga, k_xr, k_gr, k_gm = (c // D_GROUP for c in (c_ga, c_xr, c_gr, c_gm))
    k_split = k_gm + GM_SPLIT // D_GROUP
    n_g2_gm = (D_MODEL - GM_SPLIT + D_MODEL) // D_GROUP

    def gate1_cols(w):
        return w[..., c_gr:c_gm + GM_SPLIT]

    def gate2_cols(w):
        return jnp.concatenate([w[..., c_gm + GM_SPLIT:], w[..., c_ga:c_xr]], axis=-1)

    def group_cols(w, g):
        return jnp.concatenate([w[..., part * D_ATTN + g * D_GROUP:part * D_ATTN + (g + 1) * D_GROUP]
                                for part in range(3)], axis=-1)

    def gate_w(direction):
        w = 0.5 * jnp.concatenate([lru_w[l, direction, 0], lru_w[l, direction, 1]], axis=-1)
        bias = 0.5 * jnp.concatenate([lru_b[l, direction, 0], lru_b[l, direction, 1]], axis=-1)
        terms, rem = [], bias
        for _ in range(GATE_BIAS_ROWS):
            term = rem.astype(bf)
            terms.append(term)
            rem = rem - term.astype(jnp.float32)
        bias_rows = jnp.pad(jnp.stack(terms, axis=1), ((0, 0), (0, RNN_BLOCK - GATE_BIAS_ROWS), (0, 0)))
        return jnp.concatenate([w.astype(bf), bias_rows], axis=1)

    return dict(
        w_qkv=[_cast_cols(w_in, l, lambda j, g=g: j * (D_ATTN // D_GROUP) + g, 3) for g in range(N_GROUPS)],
        b_qkv=[group_cols(b_in[l], g)[None, :] for g in range(N_GROUPS)],
        w_xr=_cast_cols(w_in, l, lambda j: k_xr + j, D_RNN // D_GROUP), b_xr=b_in[l, None, c_xr:c_gr],
        w_g1=_cast_cols(w_in, l, lambda j: k_gr + j, GATE1_COLS // D_GROUP, 0.5),
        b_g1=0.5 * gate1_cols(b_in[l])[None, :],
        w_g2=_cast_cols(w_in, l, lambda j: jnp.where(j < n_g2_gm, k_split + j, k_ga), GATE2_COLS // D_GROUP, 0.5),
        b_g2=0.5 * gate2_cols(b_in[l])[None, :],
        conv_w=conv_w[l], conv_b=conv_b[l][None, :],
        wg=[gate_w(0), gate_w(1)],
        lam=[lru_lam[l, 0][None, :], lru_lam[l, 1][None, :]],
        wa=_cast_cols(w_attn_o, l, lambda j: j, D_MODEL // D_GROUP),
        wr=_cast_cols(w_rnn_o, l, lambda j: j, D_MODEL // D_GROUP),
        wo=_cast_cols(w_out, l, lambda j: j, D_MODEL // D_GROUP),
        ln_g=ln_g[l][None, :], ln_b=ln_b[l][None, :],
    )


def _layer(x, x_bf, p, biases):
    attn = []
    for g, (_, dil) in enumerate(DILATED_GROUPS):
        if x_bf is None:
            qkv, x_bf = _qkv_proj(x, p["w_qkv"][g], p["b_qkv"][g], dil, g)
        else:
            qkv = _qkv_proj(x_bf, p["w_qkv"][g], p["b_qkv"][g], dil, g)
        attn.append(_attention_group(qkv, biases[g], g, dil))
    xr = _xr_proj(x_bf, p["w_xr"], p["b_xr"])
    z1, h_fwd, xconv = _rglru(xr, x_bf, p["w_g1"], p["b_g1"], p["conv_w"], p["conv_b"],
                              p["wg"][0], p["lam"][0], False)
    z2, u_r = _rglru(xconv, x_bf, p["w_g2"], p["b_g2"], None, None,
                     p["wg"][1], p["lam"][1], True, h_fwd, z1)
    return _merge_out(x, attn, z1, z2, u_r, p["wa"], p["wr"], p["wo"], p["ln_g"], p["ln_b"])


def kernel(x_prompt, x_sample, w_in, b_in, conv_w, conv_b, lru_w, lru_b, lru_lam,
           w_attn_o, w_rnn_o, w_out, ln_g, ln_b, rel_bias):
    layers = [_prep_layer(l, w_in, b_in, conv_w, conv_b, lru_w, lru_b, lru_lam,
                          w_attn_o, w_rnn_o, w_out, ln_g, ln_b) for l in range(DEPTH)]
    biases = [_band_bias(rel_bias, g, dil) for g, (_, dil) in enumerate(DILATED_GROUPS)]

    def trunk(x):
        x_bf = None
        for p in layers:
            x, x_bf = _layer(x, x_bf, p, biases)
        return x

    return (trunk(x_prompt), trunk(x_sample))
```

```python
import functools

import numpy as np
import jax
import jax.numpy as jnp
from jax import lax
from jax.experimental import pallas as pl
from jax.experimental.pallas import tpu as pltpu

D_MODEL = 2048
DEPTH = 2
HEAD_DIM = 128
HEADS_PER_GROUP = 4
DILATED_GROUPS = ((128, 1), (512, 4), (2048, 16))
N_GROUPS = len(DILATED_GROUPS)
D_ATTN = N_GROUPS * HEADS_PER_GROUP * HEAD_DIM
D_GROUP = HEADS_PER_GROUP * HEAD_DIM
D_RNN = D_MODEL
N_RNN_BLOCKS = 16
RNN_BLOCK = D_RNN // N_RNN_BLOCKS
CONV_WIDTH = 4
CONV_LEFT = 2
LRU_C = 8.0
REL_BUCKETS = 32
REL_MAX_DIST = 1024
ALPHA = (2.0 * DEPTH) ** 0.25
LN_EPS = 1e-5
NEG_INF = -1e30

LANES = 128
SUBLANES = 8
VMEM_LIMIT_BYTES = 56 * 1024 * 1024

HALF = 64
Q_SUB = 128
K_WIN = Q_SUB + 2 * HALF
ATTN_TOKENS = 2048
ATTN_RESIDUES_PER_TRIP = 16

RNN_TILE = 512
RNN_SLAB = 1024
RADIX = 4
GATE_BIAS_ROWS = 3
LOG2_E = 1.4426950408889634

MM_TOKENS = RNN_TILE
MM_PIECE = 256

PROJ_TM = 2048
MERGE_TM = 256

GM_SPLIT = D_MODEL // 2
GATE1_COLS = D_RNN + GM_SPLIT
GATE2_COLS = (D_MODEL - GM_SPLIT) + D_MODEL + D_GROUP


def _params(*sem):
    return pltpu.CompilerParams(dimension_semantics=sem, vmem_limit_bytes=VMEM_LIMIT_BYTES)


def _silu_of_half(h):
    return h + h * jnp.tanh(h)


def _xr_proj_kernel(x_ref, w_ref, b_ref, o_ref):
    acc = jnp.dot(x_ref[...], w_ref[...], preferred_element_type=jnp.float32)
    o_ref[...] = acc + b_ref[...]


def _xr_proj(x_bf, w_bf, b):
    bsz, s, d = x_bf.shape
    n = w_bf.shape[1]
    tm, tn = PROJ_TM // 2, n
    return pl.pallas_call(
        _xr_proj_kernel,
        grid=(bsz, s // tm, n // tn),
        in_specs=[
            pl.BlockSpec((None, tm, d), lambda bb, i, j: (bb, i, 0)),
            pl.BlockSpec((d, tn), lambda bb, i, j: (0, j)),
            pl.BlockSpec((1, tn), lambda bb, i, j: (0, j)),
        ],
        out_specs=pl.BlockSpec((None, tm, tn), lambda bb, i, j: (bb, i, j)),
        out_shape=jax.ShapeDtypeStruct((bsz, s, n), jnp.float32),
        compiler_params=_params("parallel", "parallel", "arbitrary"),
        name="xr_proj",
    )(x_bf, w_bf, b)


def _qkv_proj_kernel(x_ref, w_ref, b_ref, o_ref, *rest, dil, tm, tn, emit_bf16):
    if emit_bf16:
        xbf_ref, rest = rest[0], rest[1:]

        @pl.when(pl.program_id(2) == 0)
        def _():
            xbf_ref[...] = x_ref[...].astype(jnp.bfloat16)

        x = xbf_ref[...]
    else:
        x = x_ref[...]
    acc = jnp.dot(x, w_ref[...], preferred_element_type=jnp.float32) + b_ref[...]
    if dil == 1:
        o_ref[0] = acc.astype(o_ref.dtype)
        return
    acc_s = rest[0]
    rows = tm // dil
    for c in range(tn // LANES):
        cs = slice(c * LANES, (c + 1) * LANES)
        acc_s[c] = acc[:, cs]
        if dil == RADIX:
            for r in range(dil):
                o_ref[r, :, cs] = acc_s[c, pl.ds(r, rows, stride=dil), :].astype(o_ref.dtype)
        else:
            assert dil == RADIX * RADIX
            mid_s = rest[1]
            for lo in range(RADIX):
                mid_s[lo] = acc_s[c, pl.ds(lo, tm // RADIX, stride=RADIX), :]
            for lo in range(RADIX):
                for hi in range(RADIX):
                    o_ref[lo + RADIX * hi, :, cs] = mid_s[lo, pl.ds(hi, rows, stride=RADIX), :].astype(o_ref.dtype)


def _qkv_proj(x, w_bf, b, dil, g):
    bsz, s, d = x.shape
    n = w_bf.shape[1]
    emit_bf16 = x.dtype != jnp.bfloat16
    tm, tn = PROJ_TM // 2, n
    scratch = [] if dil == 1 else [pltpu.VMEM((tn // LANES, tm, LANES), jnp.float32)]
    if dil > RADIX:
        scratch.append(pltpu.VMEM((RADIX, tm // RADIX, LANES), jnp.float32))
    out_specs = [pl.BlockSpec((None, dil, tm // dil, tn), lambda bb, i, j: (bb, 0, i, j))]
    out_shape = [jax.ShapeDtypeStruct((bsz, dil, s // dil, n), jnp.bfloat16)]
    if emit_bf16:
        out_specs.append(pl.BlockSpec((None, tm, d), lambda bb, i, j: (bb, i, 0)))
        out_shape.append(jax.ShapeDtypeStruct((bsz, s, d), jnp.bfloat16))
    outs = pl.pallas_call(
        functools.partial(_qkv_proj_kernel, dil=dil, tm=tm, tn=tn, emit_bf16=emit_bf16),
        grid=(bsz, s // tm, n // tn),
        in_specs=[
            pl.BlockSpec((None, tm, d), lambda bb, i, j: (bb, i, 0)),
            pl.BlockSpec((d, tn), lambda bb, i, j: (0, j)),
            pl.BlockSpec((1, tn), lambda bb, i, j: (0, j)),
        ],
        out_specs=out_specs,
        out_shape=out_shape,
        scratch_shapes=scratch,
        compiler_params=_params("parallel", "parallel", "arbitrary"),
        name=f"qkv_proj_g{g}",
    )(x, w_bf, b)
    return outs if emit_bf16 else outs[0]


def _t5_bucket(rel):
    nb = REL_BUCKETS // 2
    max_exact = nb // 2
    ret = (rel > 0).astype(np.int32) * nb
    n = np.abs(rel)
    large = max_exact + (np.log(np.maximum(n, max_exact) / max_exact)
                         / np.log(REL_MAX_DIST / max_exact) * (nb - max_exact)).astype(np.int32)
    large = np.minimum(large, nb - 1)
    return (ret + np.where(n < max_exact, n, large)).astype(np.int32)


def _band_bias(rel_bias, g, dil):
    tab = rel_bias[_t5_bucket(np.arange(-HALF, HALF + 1) * dil)][:, g * HEADS_PER_GROUP:(g + 1) * HEADS_PER_GROUP]
    tab = tab * (HEAD_DIM ** 0.5)
    period = Q_SUB + K_WIN + LANES
    vec = jnp.full((HEADS_PER_GROUP, period), NEG_INF, jnp.float32).at[:, :2 * HALF + 1].set(tab.T)
    mat = jnp.tile(vec, (1, Q_SUB))[:, :Q_SUB * (period - 1)].reshape(HEADS_PER_GROUP, Q_SUB, period - 1)
    return mat[:, :, :K_WIN]


def _attn_kernel(q_ref, kp_ref, kc_ref, kn_ref, vp_ref, vc_ref, vn_ref, bias_ref,
                 o_ref, lse_ref, kbuf, vbuf, *, dil, tile, length):
    i = pl.program_id(1)
    scale = HEAD_DIM ** -0.5

    def residue(r, slot):
        kbuf[slot, 0:HALF] = kp_ref[r]
        kbuf[slot, HALF:HALF + tile] = kc_ref[r]
        kbuf[slot, HALF + tile:] = kn_ref[r]
        vbuf[slot, 0:HALF] = vp_ref[r]
        vbuf[slot, HALF:HALF + tile] = vc_ref[r]
        vbuf[slot, HALF + tile:] = vn_ref[r]
        n_sub = tile // Q_SUB
        for sb in range(n_sub):
            r0 = sb * Q_SUB
            at_edge = sb == 0 or sb == n_sub - 1
            if at_edge:
                kpos = i * tile + (r0 - HALF) + lax.broadcasted_iota(jnp.int32, (Q_SUB, K_WIN), 1)
                in_seq = (kpos >= 0) & (kpos < length)
            if dil == 1:
                rows = slice(r0, r0 + Q_SUB)
            else:
                rows = pl.ds(r + r0 * dil, Q_SUB, stride=dil)
            for h in range(HEADS_PER_GROUP):
                cs = slice(h * HEAD_DIM, (h + 1) * HEAD_DIM)
                q = q_ref[r, r0:r0 + Q_SUB, cs]
                k = kbuf[slot, r0:r0 + K_WIN, cs]
                v = vbuf[slot, r0:r0 + K_WIN, cs]
                t = lax.dot_general(q, k, (((1,), (1,)), ((), ())),
                                    preferred_element_type=jnp.float32) + bias_ref[h]
                if at_edge:
                    t = jnp.where(in_seq, t, NEG_INF)
                mt = jnp.max(t, axis=-1, keepdims=True)
                p = jnp.exp2((t - mt) * (scale * LOG2_E))
                den = jnp.sum(p, axis=-1, keepdims=True)
                o = jnp.dot(p.astype(jnp.bfloat16), v, preferred_element_type=jnp.float32)
                o_ref[h, rows, :] = o / den
                lse_ref[h, rows, :] = jnp.broadcast_to(mt * scale + jnp.log(den), (Q_SUB, HEAD_DIM))

    per_trip = kbuf.shape[0]

    def trip(it, _):
        for slot in range(per_trip):
            residue(it * per_trip + slot, slot)
        return 0

    if dil == per_trip:
        trip(0, 0)
    else:
        lax.fori_loop(0, dil // per_trip, trip, 0)


def _attention_group(qkv, bias, g, dil):
    bsz, _, length, _ = qkv.shape
    s = length * dil
    tile = ATTN_TOKENS // dil
    per_trip = min(dil, ATTN_RESIDUES_PER_TRIP)
    hb = tile // HALF
    n_half_blocks = length // HALF

    def cur(part):
        return pl.BlockSpec((None, dil, tile, D_GROUP), lambda bb, i: (bb, 0, i, part))

    def prev(part):
        return pl.BlockSpec((None, dil, HALF, D_GROUP),
                            lambda bb, i: (bb, 0, jnp.maximum(i * hb - 1, 0), part))

    def nxt(part):
        return pl.BlockSpec((None, dil, HALF, D_GROUP),
                            lambda bb, i: (bb, 0, jnp.minimum((i + 1) * hb, n_half_blocks - 1), part))

    out_spec = pl.BlockSpec((None, HEADS_PER_GROUP, tile * dil, HEAD_DIM), lambda bb, i: (bb, 0, i, 0))
    out_sds = jax.ShapeDtypeStruct((bsz, HEADS_PER_GROUP, s, HEAD_DIM), jnp.float32)
    return pl.pallas_call(
        functools.partial(_attn_kernel, dil=dil, tile=tile, length=length),
        grid=(bsz, length // tile),
        in_specs=[cur(0), prev(1), cur(1), nxt(1), prev(2), cur(2), nxt(2),
                  pl.BlockSpec((HEADS_PER_GROUP, Q_SUB, K_WIN), lambda bb, i: (0, 0, 0))],
        out_specs=[out_spec, out_spec],
        out_shape=[out_sds, out_sds],
        scratch_shapes=[pltpu.VMEM((per_trip, tile + 2 * HALF, D_GROUP), jnp.bfloat16),
                        pltpu.VMEM((per_trip, tile + 2 * HALF, D_GROUP), jnp.bfloat16)],
        compiler_params=_params("parallel", "arbitrary"),
        name=f"attn_g{g}",
    )(qkv, qkv, qkv, qkv, qkv, qkv, qkv, bias)


def _vreg_scan(a, b, state, reverse):
    groups = a.shape[0] // SUBLANES
    a3 = a.reshape(groups, SUBLANES, LANES)
    b3 = b.reshape(groups, SUBLANES, LANES)
    sub = lax.broadcasted_iota(jnp.int32, (groups, SUBLANES, LANES), 1)
    for shift in (1, 2, 4):
        keep = (sub < SUBLANES - shift) if reverse else (sub >= shift)
        amount = SUBLANES - shift if reverse else shift
        a_sh = jnp.where(keep, pltpu.roll(a3, amount, axis=1), 1.0)
        b_sh = jnp.where(keep, pltpu.roll(b3, amount, axis=1), 0.0)
        b3 = a3 * b_sh + b3
        a3 = a3 * a_sh
    sub2 = lax.broadcasted_iota(jnp.int32, (SUBLANES, LANES), 0)
    entering = [None] * groups
    for gi in (range(groups - 1, -1, -1) if reverse else range(groups)):
        sb = jnp.broadcast_to(state, (SUBLANES, LANES))
        incl = b3[gi] + a3[gi] * sb
        if reverse:
            entering[gi] = jnp.where(sub2 == SUBLANES - 1, sb, pltpu.roll(incl, SUBLANES - 1, axis=0))
            state = incl[0:1]
        else:
            entering[gi] = jnp.where(sub2 == 0, sb, pltpu.roll(incl, 1, axis=0))
            state = incl[SUBLANES - 1:SUBLANES]
    return jnp.concatenate(entering, axis=0), state


def _radix_scan(av, bv, entering_fn, reverse):
    order = tuple(range(RADIX - 1, -1, -1)) if reverse else tuple(range(RADIX))
    h = [None] * RADIX
    p = [None] * RADIX
    h[order[0]], p[order[0]] = bv[order[0]], av[order[0]]
    for before, t in zip(order[:-1], order[1:]):
        h[t] = av[t] * h[before] + bv[t]
        p[t] = av[t] * p[before]
    entering, state = entering_fn(p[order[-1]], h[order[-1]])
    incl = [h[t] + p[t] * entering for t in range(RADIX)]
    excl = [None] * RADIX
    excl[order[0]] = entering
    for before, t in zip(order[:-1], order[1:]):
        excl[t] = incl[before]
    return incl, excl, state


def _rglru_kernel(*refs, tile, n_tiles, reverse):
    if reverse:
        (xm_ref, wm_ref, bm_ref, conv_ref, wg_ref, lam_ref, hf_ref, gr_ref,
         zm_ref, out_ref, sum_a, sum_b, ent_s, carry, ubuf) = refs
    else:
        (xm_ref, wm_ref, bm_ref, xp_ref, xc_ref, xn_ref, cw_ref, cb_ref, wg_ref, lam_ref,
         zm_ref, out_ref, conv_ref, ext, sum_a, sum_b, ent_s, carry) = refs

    def gate_proj_piece(k):
        cols = slice(k * MM_PIECE, (k + 1) * MM_PIECE)
        zm_ref[:, cols] = (jnp.dot(xm_ref[...], wm_ref[:, cols], preferred_element_type=jnp.float32)
                           + bm_ref[:, cols])

    n_pieces = zm_ref.shape[1] // MM_PIECE
    step = pl.program_id(2)
    ti = (n_tiles - 1 - step) if reverse else step
    quarter = tile // RADIX
    sixteenth = quarter // RADIX
    nblk = RNN_SLAB // RNN_BLOCK

    @pl.when(step == 0)
    def _():
        carry[...] = jnp.zeros_like(carry)

    lam = lam_ref[...]
    half_csp = (0.5 * LRU_C) * (jnp.maximum(-lam, 0.0) + jnp.log1p(jnp.exp(-jnp.abs(lam))))
    bias_taps = (lax.broadcasted_iota(jnp.int32, (tile, RNN_BLOCK), 1) < GATE_BIAS_ROWS).astype(jnp.bfloat16)

    assert n_pieces <= nblk
    for n in range(nblk):
        cs = slice(n * RNN_BLOCK, (n + 1) * RNN_BLOCK)
        if reverse:
            xc = conv_ref[:, cs]
        else:
            ext[n, 0:SUBLANES] = jnp.where(ti > 0, xp_ref[:, cs], 0.0)
            ext[n, SUBLANES:SUBLANES + tile] = xc_ref[:, cs]
            ext[n, SUBLANES + tile:] = jnp.where(ti < n_tiles - 1, xn_ref[:, cs], 0.0)
            taps = {u: ext[n, pl.ds(SUBLANES + u, quarter, stride=RADIX), :]
                    for u in range(-CONV_LEFT, RADIX + CONV_WIDTH - 1 - CONV_LEFT)}
            xc_t = []
            for t in range(RADIX):
                acc = cb_ref[:, cs]
                for j in range(CONV_WIDTH):
                    acc = acc + taps[t + j - CONV_LEFT] * cw_ref[j:j + 1, cs]
                xc_t.append(acc)
            xc = jnp.concatenate(xc_t, axis=0)
            conv_ref[:, cs] = xc
        lhs = jnp.concatenate([xc.astype(jnp.bfloat16), bias_taps], axis=1)
        z = jnp.dot(lhs, wg_ref[n], preferred_element_type=jnp.float32)
        if n < n_pieces:
            gate_proj_piece(n)
        th_r = jnp.tanh(z[:, :RNN_BLOCK])
        th_i = jnp.tanh(z[:, RNN_BLOCK:])
        hsp = half_csp[:, cs]
        neg_log_a = hsp + hsp * th_r
        a = jnp.exp2(neg_log_a * (-LOG2_E))
        half_xc = 0.5 * xc
        one_m_a2 = jnp.tanh(neg_log_a) * (1.0 + a * a)
        root = jnp.where(one_m_a2 > 0.0, one_m_a2 * lax.rsqrt(one_m_a2), 0.0)
        b = root * (half_xc + half_xc * th_i)
        av = [a[t * quarter:(t + 1) * quarter] for t in range(RADIX)]
        bv = [b[t * quarter:(t + 1) * quarter] for t in range(RADIX)]

        def level2(p, h, n=n, cs=cs):
            sum_a[...] = p
            sum_b[...] = h
            av2 = [sum_a[pl.ds(t, sixteenth, stride=RADIX), :] for t in range(RADIX)]
            bv2 = [sum_b[pl.ds(t, sixteenth, stride=RADIX), :] for t in range(RADIX)]
            _, excl2, state = _radix_scan(
                av2, bv2, lambda p3, h3: _vreg_scan(p3, h3, carry[:, cs], reverse), reverse)
            for t in range(RADIX):
                ent_s[pl.ds(t, sixteenth, stride=RADIX), :] = excl2[t]
            carry[:, cs] = state
            return ent_s[...], state

        incl, _, _ = _radix_scan(av, bv, level2, reverse)
        for t in range(RADIX):
            rows = slice(t * quarter, (t + 1) * quarter)
            if reverse:
                ubuf[n, pl.ds(t, quarter, stride=RADIX), :] = incl[t] + hf_ref[rows, cs]
            else:
                out_ref[rows, cs] = incl[t]

    if reverse:
        for n in range(nblk):
            cs = slice(n * RNN_BLOCK, (n + 1) * RNN_BLOCK)
            out_ref[:, cs] = (ubuf[n] * _silu_of_half(gr_ref[:, cs])).astype(out_ref.dtype)


def _rglru(xr, x_bf, w_m, b_m, conv_w, conv_b, wg, lam, reverse, h_fwd=None, gr_src=None):
    b, s, _ = xr.shape
    n_m = w_m.shape[1]
    tn_m = n_m // 2
    tile = RNN_TILE
    n_tiles = s // tile
    nslab = D_RNN // RNN_SLAB
    sub_per_tile = tile // SUBLANES
    n_sub = s // SUBLANES
    nblk = RNN_SLAB // RNN_BLOCK
    quarter = tile // RADIX

    def tidx(i):
        return (n_tiles - 1 - i) if reverse else i

    x_cur = pl.BlockSpec((None, tile, RNN_SLAB), lambda bb, c, i: (bb, tidx(i), c))
    x_prev = pl.BlockSpec((None, SUBLANES, RNN_SLAB),
                          lambda bb, c, i: (bb, jnp.maximum(tidx(i) * sub_per_tile - 1, 0), c))
    x_next = pl.BlockSpec((None, SUBLANES, RNN_SLAB),
                          lambda bb, c, i: (bb, jnp.minimum((tidx(i) + 1) * sub_per_tile, n_sub - 1), c))
    m_tiles = b * s // MM_TOKENS
    assert m_tiles * 2 == b * nslab * n_tiles

    def m_col(bb, c, i):
        return ((bb * nslab + c) * n_tiles + i) // m_tiles

    def m_row(bb, c, i):
        return ((bb * nslab + c) * n_tiles + i) % m_tiles

    time_tile = pl.BlockSpec((None, tile, RNN_SLAB), lambda bb, c, i: (bb, tidx(i), c))
    gate_w = pl.BlockSpec((nblk, 2 * RNN_BLOCK, 2 * RNN_BLOCK), lambda bb, c, i: (c, 0, 0))
    lam_spec = pl.BlockSpec((1, RNN_SLAB), lambda bb, c, i: (0, c))
    in_specs = [
        pl.BlockSpec((MM_TOKENS, D_MODEL), lambda bb, c, i: (m_row(bb, c, i), 0)),
        pl.BlockSpec((D_MODEL, tn_m), lambda bb, c, i: (0, m_col(bb, c, i))),
        pl.BlockSpec((1, tn_m), lambda bb, c, i: (0, m_col(bb, c, i))),
    ]
    args = [x_bf.reshape(b * s, D_MODEL), w_m, b_m]
    scan_scratch = [
        pltpu.VMEM((quarter, RNN_BLOCK), jnp.float32),
        pltpu.VMEM((quarter, RNN_BLOCK), jnp.float32),
        pltpu.VMEM((quarter, RNN_BLOCK), jnp.float32),
        pltpu.VMEM((1, RNN_SLAB), jnp.float32),
    ]
    out_specs = [pl.BlockSpec((MM_TOKENS, tn_m), lambda bb, c, i: (m_row(bb, c, i), m_col(bb, c, i))), time_tile]
    state_sds = jax.ShapeDtypeStruct((b, s, D_RNN), jnp.float32)
    if reverse:
        in_specs += [time_tile, gate_w, lam_spec, time_tile, time_tile]
        args += [xr, wg, lam, h_fwd, gr_src]
        scratch = scan_scratch + [pltpu.VMEM((nblk, tile, RNN_BLOCK), jnp.float32)]
        out_shape = [jax.ShapeDtypeStruct((b * s, n_m), jnp.float32),
                     jax.ShapeDtypeStruct((b, s, D_RNN), jnp.bfloat16)]
    else:
        in_specs += [x_prev, x_cur, x_next,
                     pl.BlockSpec((CONV_WIDTH, RNN_SLAB), lambda bb, c, i: (0, c)),
                     pl.BlockSpec((1, RNN_SLAB), lambda bb, c, i: (0, c)),
                     gate_w, lam_spec]
        args += [xr, xr, xr, conv_w, conv_b, wg, lam]
        scratch = [pltpu.VMEM((nblk, tile + 2 * SUBLANES, RNN_BLOCK), jnp.float32)] + scan_scratch
        out_specs.append(time_tile)
        out_shape = [jax.ShapeDtypeStruct((b * s, n_m), jnp.float32), state_sds, state_sds]
    outs = pl.pallas_call(
        functools.partial(_rglru_kernel, tile=tile, n_tiles=n_tiles, reverse=reverse),
        grid=(b, nslab, n_tiles),
        in_specs=in_specs,
        out_specs=out_specs,
        out_shape=out_shape,
        scratch_shapes=scratch,
        compiler_params=_params("arbitrary", "arbitrary", "arbitrary"),
        name="rglru_bwd" if reverse else "rglru_fwd",
    )(*args)
    return (outs[0].reshape(b, s, n_m),) + tuple(outs[1:])


def _merge_kernel(x_ref, o0_ref, o1_ref, o2_ref, l0_ref, l1_ref, l2_ref, ga_ref, ur_ref,
                  gma_lo_ref, gma_hi_ref, gmr_lo_ref, gmr_hi_ref,
                  wa_ref, wr_ref, wo_ref, lng_ref, lnb_ref, y_ref, ybf_ref=None):
    def heads(ref):
        return jnp.concatenate([ref[h] for h in range(HEADS_PER_GROUP)], axis=1)

    half_gma = jnp.concatenate([gma_lo_ref[...], gma_hi_ref[...]], axis=1)
    half_gmr = jnp.concatenate([gmr_lo_ref[...], gmr_hi_ref[...]], axis=1)

    l0, l1, l2 = heads(l0_ref), heads(l1_ref), heads(l2_ref)
    mx = jnp.maximum(jnp.maximum(l0, l1), l2)
    e0, e1, e2 = jnp.exp(l0 - mx), jnp.exp(l1 - mx), jnp.exp(l2 - mx)
    inv = 1.0 / (e0 + e1 + e2)
    oa = (e0 * inv) * heads(o0_ref) + (e1 * inv) * heads(o1_ref) + (e2 * inv) * heads(o2_ref)
    ua = (oa * _silu_of_half(ga_ref[...])).astype(jnp.bfloat16)
    ya = jnp.dot(ua, wa_ref[...], preferred_element_type=jnp.float32)
    yr = jnp.dot(ur_ref[...], wr_ref[...], preferred_element_type=jnp.float32)
    mixed = (0.5 + 0.5 * jnp.tanh(half_gma)) * ya + (0.5 + 0.5 * jnp.tanh(half_gmr)) * yr
    out = jnp.dot(mixed.astype(jnp.bfloat16), wo_ref[...], preferred_element_type=jnp.float32)
    v = ALPHA * x_ref[...] + out
    mu = jnp.mean(v, axis=-1, keepdims=True)
    vc = v - mu
    var = jnp.mean(vc * vc, axis=-1, keepdims=True)
    y = vc * lax.rsqrt(var + LN_EPS) * lng_ref[...] + lnb_ref[...]
    y_ref[...] = y
    if ybf_ref is not None:
        ybf_ref[...] = y.astype(jnp.bfloat16)


def _merge_out(x, attn, z1, z2, u_r, wa, wr, wo, ln_g, ln_b, want_bf16):
    bsz, s, d = x.shape
    tm = MERGE_TM
    row = lambda w: pl.BlockSpec((None, tm, w), lambda bb, i: (bb, i, 0))
    head = pl.BlockSpec((None, HEADS_PER_GROUP, tm, HEAD_DIM), lambda bb, i: (bb, 0, i, 0))
    const = lambda shp: pl.BlockSpec(shp, lambda bb, i: (0, 0), pipeline_mode=pl.Buffered(1))
    half = lambda j: pl.BlockSpec((None, tm, GM_SPLIT), lambda bb, i: (bb, i, j))
    in_specs = [
        row(d),
        head, head, head, head, head, head,
        pl.BlockSpec((None, tm, D_GROUP), lambda bb, i: (bb, i, (GATE2_COLS - D_GROUP) // D_GROUP)),
        row(D_RNN),
        half(D_RNN // GM_SPLIT),
        half(0),
        half(1), half(2),
        const((D_GROUP, D_MODEL)), const((D_RNN, D_MODEL)), const((D_MODEL, D_MODEL)),
        const((1, D_MODEL)), const((1, D_MODEL)),
    ]
    (o0, l0), (o1, l1), (o2, l2) = attn
    out_dtypes = (jnp.float32, jnp.bfloat16) if want_bf16 else (jnp.float32,)
    outs = pl.pallas_call(
        _merge_kernel,
        grid=(bsz, s // tm),
        in_specs=in_specs,
        out_specs=[row(d) for _ in out_dtypes],
        out_shape=[jax.ShapeDtypeStruct((bsz, s, d), dt) for dt in out_dtypes],
        compiler_params=_params("parallel", "parallel"),
        name="merge_out",
    )(x, o0, o1, o2, l0, l1, l2, z2, u_r, z1, z2, z2, z2, wa, wr, wo, ln_g, ln_b)
    return (outs[0], outs[1]) if want_bf16 else (outs[0], None)


def _cast_cols_kernel(w_ref, o_ref, *, scale):
    w = w_ref[...]
    if scale != 1.0:
        w = w * scale
    o_ref[...] = w.astype(o_ref.dtype)


def _cast_cols(w, l, src_block, n_blocks, scale=1.0):
    k = w.shape[1]
    return pl.pallas_call(
        functools.partial(_cast_cols_kernel, scale=scale),
        grid=(n_blocks,),
        in_specs=[pl.BlockSpec((None, k, D_GROUP), lambda j: (l, 0, src_block(j)))],
        out_specs=pl.BlockSpec((k, D_GROUP), lambda j: (0, j)),
        out_shape=jax.ShapeDtypeStruct((k, n_blocks * D_GROUP), jnp.bfloat16),
        compiler_params=_params("parallel"),
        name="cast_w",
    )(w)


def _prep_layer(l, w_in, b_in, conv_w, conv_b, lru_w, lru_b, lru_lam, w_attn_o, w_rnn_o, w_out, ln_g, ln_b):
    bf = jnp.bfloat16
    a3 = 3 * D_ATTN
    c_ga, c_xr, c_gr, c_gm = a3, a3 + D_GROUP, a3 + D_GROUP + D_RNN, a3 + D_GROUP + 2 * D_RNN
    k_ga, k_xr, k_gr, k_gm = (c // D_GROUP for c in (c_ga, c_xr, c_gr, c_gm))
    k_split = k_gm + GM_SPLIT // D_GROUP
    n_g2_gm = (D_MODEL - GM_SPLIT + D_MODEL) // D_GROUP

    def gate1_cols(w):
        return w[..., c_gr:c_gm + GM_SPLIT]

    def gate2_cols(w):
        return jnp.concatenate([w[..., c_gm + GM_SPLIT:], w[..., c_ga:c_xr]], axis=-1)

    def group_cols(w, g):
        return jnp.concatenate([w[..., part * D_ATTN + g * D_GROUP:part * D_ATTN + (g + 1) * D_GROUP]
                                for part in range(3)], axis=-1)

    def gate_w(direction):
        w = 0.5 * jnp.concatenate([lru_w[l, direction, 0], lru_w[l, direction, 1]], axis=-1)
        bias = 0.5 * jnp.concatenate([lru_b[l, direction, 0], lru_b[l, direction, 1]], axis=-1)
        terms, rem = [], bias
        for _ in range(GATE_BIAS_ROWS):
            term = rem.astype(bf)
            terms.append(term)
            rem = rem - term.astype(jnp.float32)
        bias_rows = jnp.pad(jnp.stack(terms, axis=1), ((0, 0), (0, RNN_BLOCK - GATE_BIAS_ROWS), (0, 0)))
        return jnp.concatenate([w.astype(bf), bias_rows], axis=1)

    return dict(
        w_qkv=[_cast_cols(w_in, l, lambda j, g=g: j * (D_ATTN // D_GROUP) + g, 3) for g in range(N_GROUPS)],
        b_qkv=[group_cols(b_in[l], g)[None, :] for g in range(N_GROUPS)],
        w_xr=_cast_cols(w_in, l, lambda j: k_xr + j, D_RNN // D_GROUP), b_xr=b_in[l, None, c_xr:c_gr],
        w_g1=_cast_cols(w_in, l, lambda j: k_gr + j, GATE1_COLS // D_GROUP, 0.5),
        b_g1=0.5 * gate1_cols(b_in[l])[None, :],
        w_g2=_cast_cols(w_in, l, lambda j: jnp.where(j < n_g2_gm, k_split + j, k_ga), GATE2_COLS // D_GROUP, 0.5),
        b_g2=0.5 * gate2_cols(b_in[l])[None, :],
        conv_w=conv_w[l], conv_b=conv_b[l][None, :],
        wg=[gate_w(0), gate_w(1)],
        lam=[lru_lam[l, 0][None, :], lru_lam[l, 1][None, :]],
        wa=_cast_cols(w_attn_o, l, lambda j: j, D_MODEL // D_GROUP),
        wr=_cast_cols(w_rnn_o, l, lambda j: j, D_MODEL // D_GROUP),
        wo=_cast_cols(w_out, l, lambda j: j, D_MODEL // D_GROUP),
        ln_g=ln_g[l][None, :], ln_b=ln_b[l][None, :],
    )


def _layer(x, x_bf, p, biases, want_bf16):
    attn = []
    for g, (_, dil) in enumerate(DILATED_GROUPS):
        if x_bf is None:
            qkv, x_bf = _qkv_proj(x, p["w_qkv"][g], p["b_qkv"][g], dil, g)
        else:
            qkv = _qkv_proj(x_bf, p["w_qkv"][g], p["b_qkv"][g], dil, g)
        attn.append(_attention_group(qkv, biases[g], g, dil))
    xr = _xr_proj(x_bf, p["w_xr"], p["b_xr"])
    z1, h_fwd, xconv = _rglru(xr, x_bf, p["w_g1"], p["b_g1"], p["conv_w"], p["conv_b"],
                              p["wg"][0], p["lam"][0], False)
    z2, u_r = _rglru(xconv, x_bf, p["w_g2"], p["b_g2"], None, None,
                     p["wg"][1], p["lam"][1], True, h_fwd, z1)
    return _merge_out(x, attn, z1, z2, u_r, p["wa"], p["wr"], p["wo"], p["ln_g"], p["ln_b"], want_bf16)


def kernel(x_prompt, x_sample, w_in, b_in, conv_w, conv_b, lru_w, lru_b, lru_lam,
           w_attn_o, w_rnn_o, w_out, ln_g, ln_b, rel_bias):
    layers = [_prep_layer(l, w_in, b_in, conv_w, conv_b, lru_w, lru_b, lru_lam,
                          w_attn_o, w_rnn_o, w_out, ln_g, ln_b) for l in range(DEPTH)]
    biases = [_band_bias(rel_bias, g, dil) for g, (_, dil) in enumerate(DILATED_GROUPS)]

    def trunk(x):
        x_bf = None
        for l, p in enumerate(layers):
            x, x_bf = _layer(x, x_bf, p, biases, want_bf16=l + 1 < DEPTH)
        return x

    return (trunk(x_prompt), trunk(x_sample))
```
